```python
import math
import jax, jax.numpy as jnp
from jax import lax
import numpy as np

D_MODEL = 1024
BATCH = 2
SEQ = 8192
DEPTH = 2

GRID_W = 64
CTX_LEN = 256
EPS = 1e-6

DA_HEADS = 4
DA_QK = 64
DA_V = 2 * DA_QK
DA_WIDTH = DA_HEADS * DA_V
ROPE_THETA = 10000.0
Q_BLOCK = 128

SSD_HEADS = 4
SSD_HEAD_DIM = 64
SSD_WIDTH = SSD_HEADS * SSD_HEAD_DIM
SSD_GROUPS = 2
SSD_STATE = 64
SSD_CONV = 3
SSD_CONV_CH = SSD_WIDTH + 2 * SSD_GROUPS * SSD_STATE
SSD_CHUNK = 128

GLA_HEADS = 4
GLA_DK = 32
GLA_DV = 64
GLA_KEY_WIDTH = GLA_HEADS * GLA_DK
GLA_WIDTH = GLA_HEADS * GLA_DV
GLA_RANK = 16
GLA_NORMALIZER = 16.0
GLA_CHUNK = 64

MIX_WIDTH = DA_WIDTH + SSD_WIDTH + GLA_WIDTH

IN_SIZES = (
    DA_HEADS * 2 * DA_QK,
    DA_HEADS * 2 * DA_QK,
    DA_WIDTH,
    SSD_WIDTH,
    SSD_CONV_CH,
    2 * SSD_HEADS,
    GLA_KEY_WIDTH,
    GLA_KEY_WIDTH,
    GLA_WIDTH,
    GLA_WIDTH,
    2 * GLA_RANK,
)
IN_WIDTH = 3112

N_EXPERTS = 32
TOP_K = 4
D_EXPERT = 1024
SWIGLU_LIMIT = 7.0
SWIGLU_ALPHA = 1.702
MOE_BLOCK = 128

kernel_name = "hybrid_diffattn_ssd_gla_moe_block"


def _rms(xf):
    return xf * lax.rsqrt(jnp.mean(xf * xf, axis=-1, keepdims=True) + EPS)


def rms_norm(x, w):
    return (_rms(x.astype(jnp.float32)) * w.astype(jnp.float32)).astype(x.dtype)


def split_cols(p):
    outs = []
    start = 0
    for n in IN_SIZES:
        outs.append(p[..., start:start + n])
        start += n
    return outs


def rope_1d(x, pos):
    f = x.shape[-1] // 2
    inv = ROPE_THETA ** (-jnp.arange(f, dtype=jnp.float32) / f)
    ang = pos.astype(jnp.float32)[:, None] * inv
    cos = jnp.cos(ang)[:, None, None, :]
    sin = jnp.sin(ang)[:, None, None, :]
    xf = x.astype(jnp.float32)
    x1, x2 = xf[..., :f], xf[..., f:]
    return jnp.concatenate([x1 * cos - x2 * sin, x2 * cos + x1 * sin], axis=-1).astype(x.dtype)


def axial_rope(x, row, col):
    half = x.shape[-1] // 2
    return jnp.concatenate([rope_1d(x[..., :half], row), rope_1d(x[..., half:], col)], axis=-1)


def diff_softmax_attend(q, k, v, lam):
    s = jnp.einsum('bqhmd,bkhmd->bhmqk', q, k, preferred_element_type=jnp.float32) * (DA_QK ** -0.5)
    p = jax.nn.softmax(s, axis=-1)
    pd = p[:, :, 0] - lam * p[:, :, 1]
    return jnp.einsum('bhqk,bkhe->bqhe', pd.astype(v.dtype), v)


def diff_attention_latent(q, k_all, v_all, lam):
    Bsz, S = q.shape[0], q.shape[1]
    nb = S // Q_BLOCK
    qb = jnp.moveaxis(q.reshape(Bsz, nb, Q_BLOCK, DA_HEADS, 2, DA_QK), 1, 0)
    ob = lax.map(lambda qi: diff_softmax_attend(qi, k_all, v_all, lam), qb)
    return jnp.moveaxis(ob, 0, 1).reshape(Bsz, S, DA_HEADS, DA_V)


def centred_dwconv(x, w, b):
    K = w.shape[0]
    y = lax.conv_general_dilated(x, w[:, None, :].astype(x.dtype), window_strides=(1,),
                                 padding=[(K // 2, K // 2)],
                                 dimension_numbers=('NWC', 'WIO', 'NWC'),
                                 feature_group_count=x.shape[-1])
    return y + b.astype(x.dtype)


def ssd_chunked(x, dt, A, Bm, Cm, h0):
    Bsz, L, H, P = x.shape
    G, N = Bm.shape[2], Bm.shape[3]
    Q = SSD_CHUNK
    nc = L // Q
    rep = H // G
    Bh = jnp.repeat(Bm, rep, axis=2).reshape(Bsz, nc, Q, H, N)
    Ch = jnp.repeat(Cm, rep, axis=2).reshape(Bsz, nc, Q, H, N)
    xc = x.reshape(Bsz, nc, Q, H, P)
    dtc = dt.reshape(Bsz, nc, Q, H)
    acs = jnp.cumsum(dtc * A, axis=2)
    seg = acs[:, :, :, None, :] - acs[:, :, None, :, :]
    lower = jnp.tril(jnp.ones((Q, Q), dtype=bool))[None, None, :, :, None]
    decay = jnp.exp(jnp.where(lower, seg, -jnp.inf))
    scores = jnp.einsum('bcihn,bcjhn->bcijh', Ch, Bh) * decay * dtc[:, :, None, :, :]
    y = jnp.einsum('bcijh,bcjhp->bcihp', scores, xc)
    w_end = jnp.exp(acs[:, :, -1:, :] - acs) * dtc
    states = jnp.einsum('bcjh,bcjhn,bcjhp->bchpn', w_end, Bh, xc)
    chunk_decay = jnp.exp(acs[:, :, -1, :])

    def step(h, inp):
        s, d = inp
        return h * d[:, :, None, None] + s, h

    hT, h_in = lax.scan(step, h0, (jnp.moveaxis(states, 1, 0), jnp.moveaxis(chunk_decay, 1, 0)))
    h_in = jnp.moveaxis(h_in, 0, 1)
    y = y + jnp.einsum('bcihn,bcih,bchpn->bcihp', Ch, jnp.exp(acs), h_in)
    return y.reshape(Bsz, L, H, P), hT


def ssd_sequence(xbc, dt_raw, conv_w, conv_b, a_log, dt_bias, d_skip, h0):
    Bsz, L = xbc.shape[0], xbc.shape[1]
    GN = SSD_GROUPS * SSD_STATE
    xbc = jax.nn.silu(centred_dwconv(xbc, conv_w, conv_b)).astype(jnp.float32)
    xs = xbc[..., :SSD_WIDTH].reshape(Bsz, L, SSD_HEADS, SSD_HEAD_DIM)
    Bm = xbc[..., SSD_WIDTH:SSD_WIDTH + GN].reshape(Bsz, L, SSD_GROUPS, SSD_STATE)
    Cm = xbc[..., SSD_WIDTH + GN:].reshape(Bsz, L, SSD_GROUPS, SSD_STATE)
    dt = jax.nn.softplus(dt_raw.astype(jnp.float32).reshape(Bsz, L, 2, SSD_HEADS) + dt_bias.astype(jnp.float32))
    A = -jnp.exp(a_log.astype(jnp.float32))
    fl = lambda t: jnp.flip(t, axis=1)
    y_f, h_f = ssd_chunked(xs, dt[:, :, 0], A[0], Bm, Cm, h0[0])
    y_b, h_b = ssd_chunked(fl(xs), fl(dt[:, :, 1]), A[1], fl(Bm), fl(Cm), h0[1])
    y = y_f + fl(y_b) + d_skip.astype(jnp.float32)[:, None] * xs
    return y.reshape(Bsz, L, SSD_WIDTH), (h_f, h_b)


def ssd_gated_norm(y, z, w):
    g = y * jax.nn.silu(z.astype(jnp.float32))
    gs = _rms(g.reshape(*g.shape[:-1], SSD_GROUPS, SSD_WIDTH // SSD_GROUPS)).reshape(g.shape)
    return (gs * w.astype(jnp.float32)).astype(z.dtype)


def gla_chunked(q, k, v, gk, h0):
    Bsz, L, H, dk = q.shape
    dv = v.shape[-1]
    Q = GLA_CHUNK
    nc = L // Q
    qc = q.reshape(Bsz, nc, Q, H, dk)
    kc = k.reshape(Bsz, nc, Q, H, dk)
    vc = v.reshape(Bsz, nc, Q, H, dv)
    b = jnp.cumsum(gk.reshape(Bsz, nc, Q, H, dk), axis=2)
    q_t = qc * jnp.exp(b)
    k_t = kc * jnp.exp(-b)
    lower = jnp.tril(jnp.ones((Q, Q), dtype=bool))
    att = jnp.where(lower, jnp.einsum('bcihd,bcjhd->bchij', q_t, k_t), 0.0)
    o = jnp.einsum('bchij,bcjhv->bcihv', att, vc)
    k_end = kc * jnp.exp(b[:, :, -1:] - b)
    states = jnp.einsum('bcjhd,bcjhv->bchdv', k_end, vc)
    chunk_decay = jnp.exp(b[:, :, -1])

    def step(h, inp):
        s, d = inp
        return h * d[..., None] + s, h

    hT, h_in = lax.scan(step, h0, (jnp.moveaxis(states, 1, 0), jnp.moveaxis(chunk_decay, 1, 0)))
    h_in = jnp.moveaxis(h_in, 0, 1)
    o = o + jnp.einsum('bcihd,bchdv->bcihv', q_t, h_in)
    return o.reshape(Bsz, L, H, dv), hT


def gla_sequence(q, k, v, code, gk_up, gk_b, h0):
    Bsz, L = q.shape[0], q.shape[1]
    f32 = jnp.float32
    q = q.astype(f32).reshape(Bsz, L, GLA_HEADS, GLA_DK) * (GLA_DK ** -0.5)
    k = k.astype(f32).reshape(Bsz, L, GLA_HEADS, GLA_DK)
    v = v.astype(f32).reshape(Bsz, L, GLA_HEADS, GLA_DV)
    code = code.astype(f32).reshape(Bsz, L, 2, GLA_RANK)
    gk = jax.nn.log_sigmoid(jnp.einsum('bldr,drk->bldk', code, gk_up.astype(f32)) + gk_b.astype(f32)) / GLA_NORMALIZER
    gk = gk.reshape(Bsz, L, 2, GLA_HEADS, GLA_DK)
    fl = lambda t: jnp.flip(t, axis=1)
    o_f, s_f = gla_chunked(q, k, v, gk[:, :, 0], h0[0])
    o_b, s_b = gla_chunked(fl(q), fl(k), fl(v), fl(gk[:, :, 1]), h0[1])
    return o_f + fl(o_b), (s_f, s_b)


def gla_finish(o, g, w):
    Bsz, L = o.shape[0], o.shape[1]
    on = _rms(o) * w.astype(jnp.float32)
    gg = jax.nn.silu(g.astype(jnp.float32).reshape(Bsz, L, GLA_HEADS, GLA_DV))
    return (on * gg).reshape(Bsz, L, GLA_WIDTH).astype(g.dtype)


def moe_ffn(h, w_router, b_router, w_gate_up, b_gate_up, w_down, b_down):
    T, D = h.shape
    logits = jnp.dot(h, w_router).astype(jnp.float32) + b_router.astype(jnp.float32)
    top_v, top_i = lax.top_k(logits, TOP_K)
    gates = jax.nn.softmax(top_v, axis=-1)
    TK = T * TOP_K
    flat_e = top_i.reshape(-1).astype(jnp.int32)
    flat_tok = jnp.arange(TK, dtype=jnp.int32) // TOP_K
    order = jnp.argsort(flat_e)
    sorted_e = flat_e[order]
    sorted_tok = flat_tok[order]
    sorted_gate = gates.reshape(-1)[order]
    counts = jnp.bincount(flat_e, length=N_EXPERTS).astype(jnp.int32)
    starts = jnp.cumsum(counts) - counts
    padded = (counts + MOE_BLOCK - 1) // MOE_BLOCK * MOE_BLOCK
    pad_ends = jnp.cumsum(padded)
    pad_starts = pad_ends - padded
    dest = pad_starts[sorted_e] + (jnp.arange(TK, dtype=jnp.int32) - starts[sorted_e])
    n_blocks = -(-TK // MOE_BLOCK) + N_EXPERTS
    P = n_blocks * MOE_BLOCK
    row_tok = jnp.full((P,), T, dtype=jnp.int32).at[dest].set(sorted_tok)
    h_pad = jnp.concatenate([h, jnp.zeros((1, D), h.dtype)], axis=0)
    xb = h_pad[row_tok].reshape(n_blocks, MOE_BLOCK, D)
    block_e = jnp.minimum(jnp.searchsorted(pad_ends, jnp.arange(n_blocks, dtype=jnp.int32) * MOE_BLOCK, side='right'), N_EXPERTS - 1)

    def expert_block(args):
        xblk, e = args
        gu = xblk @ w_gate_up[e] + b_gate_up[e]
        glu = jnp.minimum(gu[:, :D_EXPERT], SWIGLU_LIMIT)
        lin = jnp.clip(gu[:, D_EXPERT:], -SWIGLU_LIMIT, SWIGLU_LIMIT)
        act = glu * jax.nn.sigmoid(SWIGLU_ALPHA * glu) * (lin + 1.0)
        return act @ w_down[e] + b_down[e]

    yb = lax.map(expert_block, (xb, block_e)).reshape(P, D)
    y_assign = yb[dest] * sorted_gate[:, None].astype(yb.dtype)
    return jax.ops.segment_sum(y_assign, sorted_tok, num_segments=T)


def setup_inputs(seed: int = 0) -> dict:
    key = jax.random.key(seed)
    ks = iter(jax.random.split(key, 32))
    L, D = DEPTH, D_MODEL

    def nrm(shape, scale):
        return jax.random.normal(next(ks), shape, jnp.float32) * scale

    u = jax.random.uniform(next(ks), (L, 2, SSD_HEADS), jnp.float32)
    dt0 = jnp.exp(u * (math.log(0.1) - math.log(0.001)) + math.log(0.001))
    ssd_dt_bias = dt0 + jnp.log(-jnp.expm1(-dt0))
    ssd_a_log = jnp.log(jax.random.uniform(next(ks), (L, 2, SSD_HEADS), jnp.float32, minval=1.0, maxval=16.0))
    return {
        "x": nrm((BATCH, SEQ, D), 1.0),
        "c": nrm((BATCH, D), 1.0),
        "ctx": nrm((BATCH, CTX_LEN, D), 1.0),
        "c_ctx": nrm((D,), 1.0),
        "w_ada": nrm((L, D, 6 * D), 0.5 * D ** -0.5),
        "b_ada": nrm((L, 6 * D), 0.01),
        "norm1_w": 1.0 + nrm((L, D), 0.02),
        "w_in": nrm((L, D, IN_WIDTH), D ** -0.5),
        "da_lambda": nrm((L, 4, DA_QK), 0.1),
        "da_subln_w": 1.0 + nrm((L, DA_V), 0.02),
        "ssd_conv_w": nrm((L, SSD_CONV, SSD_CONV_CH), SSD_CONV ** -0.5),
        "ssd_conv_b": nrm((L, SSD_CONV_CH), 0.01),
        "ssd_a_log": ssd_a_log,
        "ssd_dt_bias": ssd_dt_bias,
        "ssd_d": 1.0 + nrm((L, SSD_HEADS), 0.02),
        "ssd_norm_w": 1.0 + nrm((L, SSD_WIDTH), 0.02),
        "gla_gk_up": nrm((L, 2, GLA_RANK, GLA_KEY_WIDTH), GLA_RANK ** -0.5),
        "gla_gk_b": nrm((L, 2, GLA_KEY_WIDTH), 0.01),
        "gla_norm_w": 1.0 + nrm((L, GLA_DV), 0.02),
        "w_out": nrm((L, MIX_WIDTH, D), MIX_WIDTH ** -0.5),
        "norm2_w": 1.0 + nrm((L, D), 0.02),
        "w_router": nrm((L, D, N_EXPERTS), D ** -0.5),
        "b_router": nrm((L, N_EXPERTS), 0.01),
        "w_gate_up": nrm((L, N_EXPERTS, D, 2 * D_EXPERT), D ** -0.5),
        "b_gate_up": nrm((L, N_EXPERTS, 2 * D_EXPERT), 0.01),
        "w_down": nrm((L, N_EXPERTS, D_EXPERT, D), D_EXPERT ** -0.5),
        "b_down": nrm((L, N_EXPERTS, D), 0.01),
        "final_norm_w": 1.0 + nrm((D,), 0.02),
    }


def reference(x, c, ctx, c_ctx, w_ada, b_ada, norm1_w, w_in, da_lambda, da_subln_w,
              ssd_conv_w, ssd_conv_b, ssd_a_log, ssd_dt_bias, ssd_d, ssd_norm_w,
              gla_gk_up, gla_gk_b, gla_norm_w, w_out, norm2_w, w_router, b_router,
              w_gate_up, b_gate_up, w_down, b_down, final_norm_w):
    f32 = jnp.float32
    Bsz, S, D = x.shape
    Lc = ctx.shape[1]
    ROWS = S // GRID_W
    row = jnp.repeat(jnp.arange(ROWS, dtype=jnp.int32), GRID_W)
    col = jnp.arange(S, dtype=jnp.int32) % GRID_W
    zero_ssd = jnp.zeros((Bsz, SSD_HEADS, SSD_HEAD_DIM, SSD_STATE), f32)
    zero_gla = jnp.zeros((Bsz, GLA_HEADS, GLA_DK, GLA_DV), f32)

    for l in range(DEPTH):
        last = l == DEPTH - 1
        lambda_init = 0.8 - 0.6 * math.exp(-0.3 * l)
        mod = jax.nn.silu(c) @ w_ada[l] + b_ada[l]
        mod_c = jax.nn.silu(c_ctx) @ w_ada[l] + b_ada[l]
        sh1, sc1, g1, sh2, sc2, g2 = [m[:, None, :] for m in jnp.split(mod, 6, axis=-1)]
        sh1c, sc1c, g1c, sh2c, sc2c, g2c = jnp.split(mod_c, 6)

        h = rms_norm(x, norm1_w[l]) * (1.0 + sc1) + sh1
        hc = rms_norm(ctx, norm1_w[l]) * (1.0 + sc1c) + sh1c
        pl = split_cols(h @ w_in[l])
        pc = split_cols(hc @ w_in[l])

        lp = da_lambda[l].astype(f32)
        lam = jnp.exp(jnp.sum(lp[0] * lp[1])) - jnp.exp(jnp.sum(lp[2] * lp[3])) + lambda_init
        q_l = axial_rope(pl[0].reshape(Bsz, S, DA_HEADS, 2, DA_QK), row, col)
        k_l = axial_rope(pl[1].reshape(Bsz, S, DA_HEADS, 2, DA_QK), row, col)
        v_l = pl[2].reshape(Bsz, S, DA_HEADS, DA_V)
        q_c = pc[0].reshape(Bsz, Lc, DA_HEADS, 2, DA_QK)
        k_c = pc[1].reshape(Bsz, Lc, DA_HEADS, 2, DA_QK)
        v_c = pc[2].reshape(Bsz, Lc, DA_HEADS, DA_V)
        k_all = jnp.concatenate([k_c, k_l], axis=1)
        v_all = jnp.concatenate([v_c, v_l], axis=1)
        o_a = diff_attention_latent(q_l, k_all, v_all, lam)
        a_l = (rms_norm(o_a, da_subln_w[l]) * (1.0 - lambda_init)).reshape(Bsz, S, DA_WIDTH).astype(x.dtype)

        y_c, st_c = ssd_sequence(pc[4], pc[5], ssd_conv_w[l], ssd_conv_b[l], ssd_a_log[l],
                                 ssd_dt_bias[l], ssd_d[l], (zero_ssd, zero_ssd))
        y_l, _ = ssd_sequence(pl[4], pl[5], ssd_conv_w[l], ssd_conv_b[l], ssd_a_log[l],
                              ssd_dt_bias[l], ssd_d[l], st_c)
        s_l = ssd_gated_norm(y_l, pl[3], ssd_norm_w[l]).astype(x.dtype)

        o_gc, gs_c = gla_sequence(pc[6], pc[7], pc[8], pc[10], gla_gk_up[l], gla_gk_b[l], (zero_gla, zero_gla))
        o_gl, _ = gla_sequence(pl[6], pl[7], pl[8], pl[10], gla_gk_up[l], gla_gk_b[l], gs_c)
        c_l = gla_finish(o_gl, pl[9], gla_norm_w[l]).astype(x.dtype)

        mix_l = jnp.concatenate([a_l, s_l, c_l], axis=-1)
        x = x + g1 * (mix_l @ w_out[l])

        if not last:
            o_ac = diff_softmax_attend(q_c, k_c, v_c, lam)
            a_c = (rms_norm(o_ac, da_subln_w[l]) * (1.0 - lambda_init)).reshape(Bsz, Lc, DA_WIDTH).astype(ctx.dtype)
            s_c = ssd_gated_norm(y_c, pc[3], ssd_norm_w[l]).astype(ctx.dtype)
            c_c = gla_finish(o_gc, pc[9], gla_norm_w[l]).astype(ctx.dtype)
            mix_c = jnp.concatenate([a_c, s_c, c_c], axis=-1)
            ctx = ctx + g1c * (mix_c @ w_out[l])

            h2 = rms_norm(x, norm2_w[l]) * (1.0 + sc2) + sh2
            h2c = rms_norm(ctx, norm2_w[l]) * (1.0 + sc2c) + sh2c
            tokens = jnp.concatenate([h2.reshape(Bsz * S, D), h2c.reshape(Bsz * Lc, D)], axis=0)
            ff = moe_ffn(tokens, w_router[l], b_router[l], w_gate_up[l], b_gate_up[l], w_down[l], b_down[l])
            x = x + g2 * ff[:Bsz * S].reshape(Bsz, S, D)
            ctx = ctx + g2c * ff[Bsz * S:].reshape(Bsz, Lc, D)
        else:
            h2 = rms_norm(x, norm2_w[l]) * (1.0 + sc2) + sh2
            ff = moe_ffn(h2.reshape(Bsz * S, D), w_router[l], b_router[l], w_gate_up[l], b_gate_up[l], w_down[l], b_down[l])
            x = x + g2 * ff.reshape(Bsz, S, D)

    return rms_norm(x, final_norm_w)
```

```python
import functools
import math

import numpy as np
import jax
import jax.numpy as jnp
from jax import lax
from jax.experimental import pallas as pl
from jax.experimental.pallas import tpu as pltpu

F32 = jnp.float32
BF16 = jnp.bfloat16
I32 = jnp.int32

GRID_W = 64
EPS = 1e-6
DA_HEADS = 4
DA_QK = 64
DA_V = 128
DA_WIDTH = 512
ROPE_THETA = 10000.0
SSD_HEADS = 4
SSD_HEAD_DIM = 64
SSD_WIDTH = 256
SSD_STATE = 64
SSD_CONV_CH = 512
GLA_HEADS = 4
GLA_DK = 32
GLA_DV = 64
GLA_KEY_WIDTH = 128
GLA_WIDTH = 256
GLA_RANK = 16
GLA_NORMALIZER = 16.0
N_EXPERTS = 32
TOP_K = 4
D_EXPERT = 1024
SWIGLU_LIMIT = 7.0
SWIGLU_ALPHA = 1.702

LANES = 128
TM = 256
SSD_Q = 128
GLA_Q = 64
GLA_SUB = 4
ATT_TQ = 256
ATT_TK = 256
MOE_BLK = 256
VMEM_LIMIT = 56 * 1024 * 1024

C_Q, C_K, C_V, C_Z, C_XBC, C_GLA, C_MISC, C_END = 0, 512, 1024, 1536, 1792, 2304, 3072, 3200
MISC_DT = 0
MISC_CODE = 8


def _sigmoid(x):
    return 1.0 / (1.0 + jnp.exp(-x))


def _softplus(x):
    return jnp.maximum(x, 0.0) + jnp.log(1.0 + jnp.exp(-jnp.abs(x)))


def _split2(a):
    hi = a.astype(BF16)
    lo = (a - hi.astype(F32)).astype(BF16)
    return hi, lo


def _split3(a):
    a1 = a.astype(BF16)
    r1 = a - a1.astype(F32)
    a2 = r1.astype(BF16)
    a3 = (r1 - a2.astype(F32)).astype(BF16)
    return a1, a2, a3


def _dot(a, b):
    return jnp.dot(a, b, preferred_element_type=F32)


def _dot_hi(a, b):
    a1, a2 = _split2(a)
    b1, b2 = _split2(b)
    return _dot(a1, b1) + (_dot(a1, b2) + _dot(a2, b1))


def _dot_exact_lhs(m, a):
    a1, a2, a3 = _split3(a)
    mb = m.astype(BF16)
    return _dot(mb, a1) + (_dot(mb, a2) + _dot(mb, a3))


def _dot_exact_rhs(a, m):
    a1, a2, a3 = _split3(a)
    mb = m.astype(BF16)
    return _dot(a1, mb) + (_dot(a2, mb) + _dot(a3, mb))


def _params(*sem):
    return pltpu.CompilerParams(dimension_semantics=sem, vmem_limit_bytes=VMEM_LIMIT)


def _ada_kernel(c_ref, w_ref, b_ref, o_ref):
    c = c_ref[...]
    s = c * _sigmoid(c)
    o_ref[0] = _dot_hi(s, w_ref[0]) + b_ref[0]


def _ada(cc, w_ada, b_ada):
    L, D, N = w_ada.shape
    tn = 512
    return pl.pallas_call(
        _ada_kernel,
        grid=(L, N // tn),
        in_specs=[pl.BlockSpec((8, D), lambda l, j: (0, 0)),
                  pl.BlockSpec((1, D, tn), lambda l, j: (l, 0, j)),
                  pl.BlockSpec((1, 1, tn), lambda l, j: (l, 0, j))],
        out_specs=pl.BlockSpec((1, 8, tn), lambda l, j: (l, 0, j)),
        out_shape=jax.ShapeDtypeStruct((L, 8, N), F32),
        compiler_params=_params("parallel", "parallel"),
        name="ada",
    )(cc, w_ada, b_ada.reshape(L, 1, N))


def _inproj_kernel(x_ref, nw_ref, sh_ref, sc_ref, cos_ref, sin_ref, w_ref,
                   q_ref, k_ref, v_ref, z_ref, xbc_ref, gla_ref, misc_ref):
    xf = x_ref[...]
    ms = jnp.mean(xf * xf, axis=-1, keepdims=True)
    h = (xf * lax.rsqrt(ms + EPS) * nw_ref[...]) * (1.0 + sc_ref[0]) + sh_ref[0]
    hb = h.astype(BF16)
    cos = cos_ref[...]
    sin = sin_ref[...]
    lane = lax.broadcasted_iota(I32, cos.shape, 1)
    first = (lane % 32) < 16

    def rope(p):
        outs = []
        for hd in range(DA_HEADS):
            ph = p[:, hd * LANES:(hd + 1) * LANES]
            sw = jnp.where(first, pltpu.roll(ph, LANES - 16, 1), pltpu.roll(ph, 16, 1))
            outs.append(ph * cos + sw * sin)
        return jnp.concatenate(outs, axis=1)

    q = rope(_dot(hb, w_ref[:, C_Q:C_K])) * (DA_QK ** -0.5)
    q_ref[...] = q.astype(BF16)
    k_ref[...] = rope(_dot(hb, w_ref[:, C_K:C_V])).astype(BF16)
    v_ref[...] = _dot(hb, w_ref[:, C_V:C_Z]).astype(BF16)
    z_ref[...] = _dot(hb, w_ref[:, C_Z:C_XBC])
    xbc_ref[...] = _dot(hb, w_ref[:, C_XBC:C_GLA])
    gla_ref[...] = _dot(hb, w_ref[:, C_GLA:C_MISC])
    misc_ref[...] = _dot(hb, w_ref[:, C_MISC:C_END])


def _inproj(x, nw, mod, cos_t, sin_t, w, lay):
    T, D = x.shape
    nlb, bpb, B = lay["nlb"], lay["bpb"], lay["B"]
    mrow = lambda i: jnp.where(i < nlb, i // bpb, B)
    prow = lambda i: jnp.where(i < nlb, i % bpb, bpb)
    row = lambda n: pl.BlockSpec((TM, n), lambda i: (i, 0))
    outs = [(512, BF16), (512, BF16), (512, BF16), (256, F32), (512, F32), (768, F32), (128, F32)]
    return pl.pallas_call(
        _inproj_kernel,
        grid=(T // TM,),
        in_specs=[row(D),
                  pl.BlockSpec((1, D), lambda i: (0, 0)),
                  pl.BlockSpec((1, 1, D), lambda i: (mrow(i), 0, 0)),
                  pl.BlockSpec((1, 1, D), lambda i: (mrow(i), 0, 1)),
                  pl.BlockSpec((TM, LANES), lambda i: (prow(i), 0)),
                  pl.BlockSpec((TM, LANES), lambda i: (prow(i), 0)),
                  pl.BlockSpec((D, C_END), lambda i: (0, 0))],
        out_specs=[row(n) for n, _ in outs],
        out_shape=[jax.ShapeDtypeStruct((T, n), dt) for n, dt in outs],
        compiler_params=_params("parallel"),
        name="inproj",
    )(x, nw, mod, mod, cos_t, sin_t, w)


def _attn_kernel(q_ref, k_ref, v_ref, lam_ref, w_ref, o_ref, q2_sc, m_sc, l_sc, acc_sc,
                 *, nq, nkc, nk, lam_init):
    qi = pl.program_id(2)
    ki = pl.program_id(3)
    tq = q_ref.shape[0]

    @pl.when(ki == 0)
    def _():
        q = q_ref[...]
        lane = lax.broadcasted_iota(I32, q.shape, 1)
        zero = jnp.zeros_like(q)
        q2_sc[0:tq, :] = jnp.where(lane < DA_QK, q, zero)
        q2_sc[tq:2 * tq, :] = jnp.where(lane >= DA_QK, q, zero)
        m_sc[...] = jnp.full(m_sc.shape, -jnp.inf, F32)
        l_sc[...] = jnp.zeros(l_sc.shape, F32)
        acc_sc[...] = jnp.zeros(acc_sc.shape, F32)

    @pl.when((qi < nq) | (ki < nkc))
    def _():
        s = lax.dot_general(q2_sc[...], k_ref[...], (((1,), (1,)), ((), ())),
                            preferred_element_type=F32)
        m_old = m_sc[...]
        m_new = jnp.maximum(m_old, jnp.max(s, axis=1, keepdims=True))
        alpha = jnp.exp(m_old - m_new)
        p = jnp.exp(s - m_new)
        l_sc[...] = alpha * l_sc[...] + jnp.sum(p, axis=1, keepdims=True)
        acc_sc[...] = alpha * acc_sc[...] + _dot(p.astype(BF16), v_ref[...])
        m_sc[...] = m_new

    @pl.when(ki == nk - 1)
    def _():
        lp = lam_ref[...]
        lam = (jnp.exp(jnp.sum(lp[0:1] * lp[1:2], axis=1, keepdims=True))
               - jnp.exp(jnp.sum(lp[2:3] * lp[3:4], axis=1, keepdims=True)) + lam_init)
        acc = acc_sc[...]
        l = l_sc[...]
        o = acc[0:tq] / l[0:tq] - lam * (acc[tq:2 * tq] / l[tq:2 * tq])
        ms = jnp.mean(o * o, axis=-1, keepdims=True)
        o_ref[...] = (o * lax.rsqrt(ms + EPS) * w_ref[...]) * (1.0 - lam_init)


def _attention(q, k, v, lam_p, subln_w, lay, lam_init, *, ctx_queries):
    B, S, Lc = lay["B"], lay["S"], lay["Lc"]
    nkc = Lc // ATT_TK
    lat_kblk = (B * S) // ATT_TK
    lat_qblk = (B * S) // ATT_TQ
    nq = S // ATT_TQ
    nqc = Lc // ATT_TQ if ctx_queries else 0
    nk = nkc + S // ATT_TK
    qmap = lambda b, h, qi, ki: (jnp.where(qi < nq, b * nq + qi, lat_qblk + b * nqc + qi - nq), h)
    kmap = lambda b, h, qi, ki: (jnp.where(ki < nkc, lat_kblk + b * nkc + ki,
                                           b * (S // ATT_TK) + ki - nkc), h)
    rows = B * S + (B * Lc if ctx_queries else 0)
    return pl.pallas_call(
        functools.partial(_attn_kernel, nq=nq, nkc=nkc, nk=nk, lam_init=lam_init),
        grid=(B, DA_HEADS, nq + nqc, nk),
        in_specs=[pl.BlockSpec((ATT_TQ, LANES), qmap),
                  pl.BlockSpec((ATT_TK, LANES), kmap),
                  pl.BlockSpec((ATT_TK, LANES), kmap),
                  pl.BlockSpec((4, DA_QK), lambda b, h, qi, ki: (0, 0)),
                  pl.BlockSpec((1, DA_V), lambda b, h, qi, ki: (0, 0))],
        out_specs=pl.BlockSpec((ATT_TQ, LANES), qmap),
        out_shape=jax.ShapeDtypeStruct((rows, DA_WIDTH), F32),
        scratch_shapes=[pltpu.VMEM((2 * ATT_TQ, LANES), BF16),
                        pltpu.VMEM((2 * ATT_TQ, 1), F32),
                        pltpu.VMEM((2 * ATT_TQ, 1), F32),
                        pltpu.VMEM((2 * ATT_TQ, LANES), F32)],
        compiler_params=_params("parallel", "parallel", "parallel", "arbitrary"),
        name="attn",
    )(q, k, v, lam_p, subln_w)


def _conv_kernel(flag_ref, x_ref, p_ref, n_ref, w_ref, b_ref, o_ref):
    i = pl.program_id(0)
    x = x_ref[...]
    keep_prev = (flag_ref[i, 0] == 0).astype(F32)
    keep_next = (flag_ref[i, 1] == 0).astype(F32)
    prev_row = p_ref[7:8, :] * keep_prev
    next_row = n_ref[0:1, :] * keep_next
    ridx = lax.broadcasted_iota(I32, x.shape, 0)
    xm1 = jnp.where(ridx == 0, prev_row, pltpu.roll(x, 1, 0))
    xp1 = jnp.where(ridx == x.shape[0] - 1, next_row, pltpu.roll(x, x.shape[0] - 1, 0))
    w = w_ref[...]
    y = w[0:1] * xm1 + w[1:2] * x + w[2:3] * xp1 + b_ref[...]
    o_ref[...] = y * _sigmoid(y)


def _conv(xbc, cw, cb, lay):
    T, C = xbc.shape
    nb = T // TM
    r8 = TM // 8
    flags = np.zeros((nb, 2), np.int32)
    for i in range(nb):
        if i < lay["nlb"]:
            flags[i, 0] = (i % lay["bpb"]) == 0
            flags[i, 1] = (i % lay["bpb"]) == lay["bpb"] - 1
        else:
            j = (i - lay["nlb"]) % lay["cpb"]
            flags[i, 0] = j == 0
            flags[i, 1] = j == lay["cpb"] - 1
    return pl.pallas_call(
        _conv_kernel,
        grid_spec=pltpu.PrefetchScalarGridSpec(
            num_scalar_prefetch=1,
            grid=(nb,),
            in_specs=[pl.BlockSpec((TM, C), lambda i, f: (i, 0)),
                      pl.BlockSpec((8, C), lambda i, f: (jnp.maximum(i * r8 - 1, 0), 0)),
                      pl.BlockSpec((8, C), lambda i, f: (jnp.minimum((i + 1) * r8, T // 8 - 1), 0)),
                      pl.BlockSpec((3, C), lambda i, f: (0, 0)),
                      pl.BlockSpec((1, C), lambda i, f: (0, 0))],
            out_specs=pl.BlockSpec((TM, C), lambda i, f: (i, 0))),
        out_shape=jax.ShapeDtypeStruct((T, C), F32),
        compiler_params=_params("parallel"),
        name="ssd_conv",
    )(jnp.asarray(flags), xbc, xbc, xbc, cw, cb)


def _ssd_dir(xbc, misc, a128, bias128, hs_ref, dirn, rev):
    Q = xbc.shape[0]
    dt = _softplus(misc + bias128)
    a = dt * a128
    ii = lax.broadcasted_iota(I32, (Q, Q), 0)
    jj = lax.broadcasted_iota(I32, (Q, Q), 1)
    M = (jj >= ii) if rev else (jj <= ii)
    acs = _dot_exact_lhs(M.astype(F32), a)
    acs_t = acs.T
    dt_t = dt.T
    xs = xbc[:, 0:SSD_WIDTH]
    Bt = xbc[:, SSD_WIDTH:SSD_WIDTH + 128].T
    Cm = xbc[:, SSD_WIDTH + 128:SSD_WIDTH + 256]
    last = 0 if rev else Q - 1
    ys = []
    for g in range(2):
        Cg = Cm[:, g * SSD_STATE:(g + 1) * SSD_STATE]
        Btg = Bt[g * SSD_STATE:(g + 1) * SSD_STATE, :]
        CB = _dot(Cg, Btg)
        for hh in range(2):
            h = 2 * g + hh
            li = dirn * SSD_HEADS + h
            acs_c = acs[:, li:li + 1]
            acs_r = acs_t[li:li + 1, :]
            dt_r = dt_t[li:li + 1, :]
            decay = jnp.exp(jnp.where(M, acs_c - acs_r, -jnp.inf))
            sc = CB * decay * dt_r
            x_h = xs[:, h * SSD_HEAD_DIM:(h + 1) * SSD_HEAD_DIM]
            hs = hs_ref[h]
            y = _dot(sc, x_h) + _dot(Cg * jnp.exp(acs_c), hs)
            tot = acs_r[:, last:last + 1]
            w_end = jnp.exp(tot - acs_r) * dt_r
            hs_ref[h] = jnp.exp(tot) * hs + _dot(Btg * w_end, x_h)
            ys.append(y)
    return jnp.concatenate(ys, axis=1)


def _ssd_kernel(xf_ref, mf_ref, xb_ref, mb_ref, a_ref, bias_ref, yf_ref, yb_ref, hf_sc, hb_sc):
    @pl.when(pl.program_id(1) == 0)
    def _():
        hf_sc[...] = jnp.zeros(hf_sc.shape, F32)
        hb_sc[...] = jnp.zeros(hb_sc.shape, F32)

    a128 = a_ref[...]
    bias128 = bias_ref[...]
    yf_ref[...] = _ssd_dir(xf_ref[...], mf_ref[...], a128, bias128, hf_sc, 0, False)
    yb_ref[...] = _ssd_dir(xb_ref[...], mb_ref[...], a128, bias128, hb_sc, 1, True)


def _scan_maps(lay, rows):
    ncc, ncl = lay["Lc"] // rows, lay["S"] // rows
    base = (lay["B"] * lay["S"]) // rows
    fmap = lambda b, c: (jnp.where(c < ncc, base + b * ncc + c, b * ncl + c - ncc), 0)
    bmap = lambda b, c: (jnp.where(c < ncc, base + b * ncc + (ncc - 1 - c),
                                   b * ncl + (ncl - 1 - (c - ncc))), 0)
    return ncc + ncl, fmap, bmap


def _ssd(xbc_act, misc, a128, bias128, lay):
    T = xbc_act.shape[0]
    nc, fmap, bmap = _scan_maps(lay, SSD_Q)
    vec = pl.BlockSpec((1, LANES), lambda b, c: (0, 0))
    return pl.pallas_call(
        _ssd_kernel,
        grid=(lay["B"], nc),
        in_specs=[pl.BlockSpec((SSD_Q, SSD_CONV_CH), fmap), pl.BlockSpec((SSD_Q, LANES), fmap),
                  pl.BlockSpec((SSD_Q, SSD_CONV_CH), bmap), pl.BlockSpec((SSD_Q, LANES), bmap),
                  vec, vec],
        out_specs=[pl.BlockSpec((SSD_Q, SSD_WIDTH), fmap), pl.BlockSpec((SSD_Q, SSD_WIDTH), bmap)],
        out_shape=[jax.ShapeDtypeStruct((T, SSD_WIDTH), F32)] * 2,
        scratch_shapes=[pltpu.VMEM((SSD_HEADS, SSD_STATE, SSD_HEAD_DIM), F32)] * 2,
        compiler_params=_params("parallel", "arbitrary"),
        name="ssd",
    )(xbc_act, misc, xbc_act, misc, a128, bias128)


def _gla_chunk(q, k, v, misc, wgk, bgk, s_ref, rev):
    Q = q.shape[0]
    pre = _dot_hi(misc, wgk) + bgk
    gk = (jnp.minimum(pre, 0.0) - jnp.log(1.0 + jnp.exp(-jnp.abs(pre)))) / GLA_NORMALIZER
    ii = lax.broadcasted_iota(I32, (Q, Q), 0)
    jj = lax.broadcasted_iota(I32, (Q, Q), 1)
    M = (jj >= ii) if rev else (jj <= ii)
    b = _dot_exact_lhs(M.astype(F32), gk)
    q_t = (q * (GLA_DK ** -0.5)) * jnp.exp(b)
    k_t = k * jnp.exp(-b)
    last = 0 if rev else Q - 1
    tot_row = b[last:last + 1, :]
    tot_col = jnp.sum(gk.T, axis=1, keepdims=True)
    k_end = k * jnp.exp(tot_row - b)
    lane_k = lax.broadcasted_iota(I32, (1, GLA_KEY_WIDTH), 1) // GLA_DK
    lane_v = lax.broadcasted_iota(I32, (1, GLA_WIDTH), 1) // GLA_DV
    S = s_ref[...]
    o = _dot(q_t, S)
    for h in range(GLA_HEADS):
        qh = jnp.where(lane_k == h, q_t, 0.0)
        att = lax.dot_general(qh, k_t, (((1,), (1,)), ((), ())), preferred_element_type=F32)
        att = jnp.where(M, att, 0.0)
        o = o + _dot(att, jnp.where(lane_v == h, v, 0.0))
    bd = (lax.broadcasted_iota(I32, S.shape, 0) // GLA_DK) == (lax.broadcasted_iota(I32, S.shape, 1) // GLA_DV)
    s_ref[...] = jnp.where(bd, S * jnp.exp(tot_col) + _dot(k_end.T, v), 0.0)
    return o


def _gla_kernel(gf_ref, mf_ref, gb_ref, mb_ref, wgk_ref, bgk_ref, of_ref, ob_ref, sf_sc, sb_sc):
    @pl.when(pl.program_id(1) == 0)
    def _():
        sf_sc[...] = jnp.zeros(sf_sc.shape, F32)
        sb_sc[...] = jnp.zeros(sb_sc.shape, F32)

    kw = GLA_KEY_WIDTH
    for j in range(GLA_SUB):
        r0 = j * GLA_Q
        blk = gf_ref[r0:r0 + GLA_Q, :]
        of_ref[r0:r0 + GLA_Q, :] = _gla_chunk(blk[:, 0:kw], blk[:, kw:2 * kw], blk[:, 2 * kw:2 * kw + GLA_WIDTH],
                                              mf_ref[r0:r0 + GLA_Q, :], wgk_ref[0], bgk_ref[0], sf_sc, False)
        r1 = (GLA_SUB - 1 - j) * GLA_Q
        blk = gb_ref[r1:r1 + GLA_Q, :]
        ob_ref[r1:r1 + GLA_Q, :] = _gla_chunk(blk[:, 0:kw], blk[:, kw:2 * kw], blk[:, 2 * kw:2 * kw + GLA_WIDTH],
                                              mb_ref[r1:r1 + GLA_Q, :], wgk_ref[1], bgk_ref[1], sb_sc, True)


def _gla(gla_in, misc, wgk, bgk, lay):
    T = gla_in.shape[0]
    R = GLA_Q * GLA_SUB
    nc, fmap, bmap = _scan_maps(lay, R)
    return pl.pallas_call(
        _gla_kernel,
        grid=(lay["B"], nc),
        in_specs=[pl.BlockSpec((R, 512), fmap), pl.BlockSpec((R, LANES), fmap),
                  pl.BlockSpec((R, 512), bmap), pl.BlockSpec((R, LANES), bmap),
                  pl.BlockSpec((2, LANES, GLA_KEY_WIDTH), lambda b, c: (0, 0, 0)),
                  pl.BlockSpec((2, 1, GLA_KEY_WIDTH), lambda b, c: (0, 0, 0))],
        out_specs=[pl.BlockSpec((R, GLA_WIDTH), fmap), pl.BlockSpec((R, GLA_WIDTH), bmap)],
        out_shape=[jax.ShapeDtypeStruct((T, GLA_WIDTH), F32)] * 2,
        scratch_shapes=[pltpu.VMEM((GLA_KEY_WIDTH, GLA_WIDTH), F32)] * 2,
        compiler_params=_params("parallel", "arbitrary"),
        name="gla",
    )(gla_in, misc, gla_in, misc, wgk, bgk)


def _outproj_kernel(x_ref, a_ref, yf_ref, yb_ref, xs_ref, z_ref, of_ref, ob_ref, g_ref,
                    dsk_ref, snw_ref, gnw_ref, w_ref, g1_ref, o_ref):
    y = yf_ref[...] + yb_ref[...] + dsk_ref[...] * xs_ref[...]
    z = z_ref[...]
    gs = y * (z * _sigmoid(z))
    half = SSD_WIDTH // 2
    parts = []
    for grp in range(2):
        seg = gs[:, grp * half:(grp + 1) * half]
        parts.append(seg * lax.rsqrt(jnp.mean(seg * seg, axis=-1, keepdims=True) + EPS))
    s = jnp.concatenate(parts, axis=1) * snw_ref[...]
    o = of_ref[...] + ob_ref[...]
    bd = ((lax.broadcasted_iota(I32, (GLA_WIDTH, GLA_WIDTH), 0) // GLA_DV)
          == (lax.broadcasted_iota(I32, (GLA_WIDTH, GLA_WIDTH), 1) // GLA_DV))
    ms = _dot_exact_rhs(o * o, jnp.where(bd, 1.0 / GLA_DV, 0.0))
    g = g_ref[...]
    c = (o * lax.rsqrt(ms + EPS) * gnw_ref[...]) * (g * _sigmoid(g))
    mix = jnp.concatenate([a_ref[...], s, c], axis=1).astype(BF16)
    o_ref[...] = x_ref[...] + g1_ref[0] * _dot(mix, w_ref[...])


def _outproj(x, a, yf, yb, xbc_act, z, of, ob, gla_in, dsk, snw, gnw, w_out, mod, lay, nrows):
    D = x.shape[1]
    nlb, bpb, B = lay["nlb"], lay["bpb"], lay["B"]
    mrow = lambda i: jnp.where(i < nlb, i // bpb, B)
    row = lambda n, cb=0: pl.BlockSpec((TM, n), lambda i: (i, cb))
    vec = lambda n: pl.BlockSpec((1, n), lambda i: (0, 0))
    return pl.pallas_call(
        _outproj_kernel,
        grid=(nrows // TM,),
        in_specs=[row(D), row(DA_WIDTH), row(SSD_WIDTH), row(SSD_WIDTH), row(SSD_WIDTH), row(SSD_WIDTH),
                  row(GLA_WIDTH), row(GLA_WIDTH), row(GLA_WIDTH, 2),
                  vec(SSD_WIDTH), vec(SSD_WIDTH), vec(GLA_WIDTH),
                  pl.BlockSpec((D, D), lambda i: (0, 0)),
                  pl.BlockSpec((1, 1, D), lambda i: (mrow(i), 0, 2))],
        out_specs=row(D),
        out_shape=jax.ShapeDtypeStruct((nrows, D), F32),
        compiler_params=_params("parallel"),
        name="outproj",
    )(x, a, yf, yb, xbc_act, z, of, ob, gla_in, dsk, snw, gnw, w_out, mod)


def _route_kernel(x_ref, nw_ref, sh_ref, sc_ref, wr_ref, br_ref,
                  h_ref, mi_ref, mf_ref, cnt_ref, carry_sc):
    i = pl.program_id(0)

    @pl.when(i == 0)
    def _():
        carry_sc[...] = jnp.zeros(carry_sc.shape, F32)

    xf = x_ref[...]
    ms = jnp.mean(xf * xf, axis=-1, keepdims=True)
    h = (xf * lax.rsqrt(ms + EPS) * nw_ref[...]) * (1.0 + sc_ref[0]) + sh_ref[0]
    h_ref[...] = h
    tm = h.shape[0]
    lane = lax.broadcasted_iota(I32, (tm, LANES), 1)
    logits = jnp.where(lane < N_EXPERTS, _dot_hi(h, wr_ref[...]) + br_ref[...], -jnp.inf)
    vals, idxs, hots = [], [], []
    l = logits
    for _ in range(TOP_K):
        m = jnp.max(l, axis=1, keepdims=True)
        idx = jnp.min(jnp.where(l == m, lane, LANES), axis=1, keepdims=True)
        hot = lane == idx
        vals.append(m)
        idxs.append(idx)
        hots.append(hot)
        l = jnp.where(hot, -jnp.inf, l)
    es = [jnp.exp(v - vals[0]) for v in vals]
    den = es[0] + es[1] + es[2] + es[3]
    hot_all = jnp.zeros((tm, LANES), F32)
    for hot in hots:
        hot_all = hot_all + hot.astype(F32)
    ii = lax.broadcasted_iota(I32, (tm, tm), 0)
    jj = lax.broadcasted_iota(I32, (tm, tm), 1)
    before = _dot((jj < ii).astype(BF16), hot_all.astype(BF16))
    rank_e = carry_sc[...] + before
    mi = jnp.zeros((tm, LANES), I32)
    mf = jnp.zeros((tm, LANES), F32)
    for kk in range(TOP_K):
        rank = jnp.sum(jnp.where(hots[kk], rank_e, 0.0), axis=1, keepdims=True).astype(I32)
        mi = jnp.where(lane == kk, idxs[kk], mi)
        mi = jnp.where(lane == TOP_K + kk, rank, mi)
        mf = jnp.where(lane == kk, es[kk] / den, mf)
    mi_ref[...] = mi
    mf_ref[...] = mf
    carry_sc[...] = carry_sc[...] + jnp.sum(hot_all, axis=0, keepdims=True)
    cnt_ref[...] = carry_sc[...]


def _route(x, nw, mod, wr, br, lay):
    T, D = x.shape
    nlb, bpb, B = lay["nlb"], lay["bpb"], lay["B"]
    mrow = lambda i: jnp.where(i < nlb, i // bpb, B)
    row = lambda n: pl.BlockSpec((TM, n), lambda i: (i, 0))
    return pl.pallas_call(
        _route_kernel,
        grid=(T // TM,),
        in_specs=[row(D),
                  pl.BlockSpec((1, D), lambda i: (0, 0)),
                  pl.BlockSpec((1, 1, D), lambda i: (mrow(i), 0, 3)),
                  pl.BlockSpec((1, 1, D), lambda i: (mrow(i), 0, 4)),
                  pl.BlockSpec((D, LANES), lambda i: (0, 0)),
                  pl.BlockSpec((1, LANES), lambda i: (0, 0))],
        out_specs=[row(D), row(LANES), row(LANES), pl.BlockSpec((1, LANES), lambda i: (0, 0))],
        out_shape=[jax.ShapeDtypeStruct((T, D), F32), jax.ShapeDtypeStruct((T, LANES), I32),
                   jax.ShapeDtypeStruct((T, LANES), F32), jax.ShapeDtypeStruct((1, LANES), F32)],
        scratch_shapes=[pltpu.VMEM((1, LANES), F32)],
        compiler_params=_params("arbitrary"),
        name="route",
    )(x, nw, mod, mod, wr, br)


def _dispatch_kernel(dest_ref, h_ref, xin_ref, xb_ref, sem):
    del xin_ref
    tm = h_ref.shape[0]

    def row_copy(t, d):
        return pltpu.make_async_copy(h_ref.at[pl.ds(t, 1)], xb_ref.at[pl.ds(d, 1)], sem)

    def issue(t, carry):
        for kk in range(TOP_K):
            row_copy(t, dest_ref[t * TOP_K + kk]).start()
        return carry

    lax.fori_loop(0, tm, issue, 0)

    def drain(t, carry):
        for kk in range(TOP_K):
            row_copy(t, dest_ref[t * TOP_K + kk]).wait()
        return carry

    lax.fori_loop(0, tm, drain, 0)


def _dispatch(dest, h, xb0):
    T, D = h.shape
    return pl.pallas_call(
        _dispatch_kernel,
        grid=(T // TM,),
        in_specs=[pl.BlockSpec((TM * TOP_K,), lambda i: (i,), memory_space=pltpu.SMEM),
                  pl.BlockSpec((TM, D), lambda i: (i, 0)),
                  pl.BlockSpec(memory_space=pl.ANY)],
        out_specs=pl.BlockSpec(memory_space=pl.ANY),
        out_shape=jax.ShapeDtypeStruct(xb0.shape, F32),
        scratch_shapes=[pltpu.SemaphoreType.DMA],
        input_output_aliases={2: 0},
        compiler_params=_params("arbitrary"),
        name="dispatch",
    )(dest, h, xb0)


def _gmm_kernel(be_ref, nu_ref, x_ref, wgu_ref, bgu_ref, wdn_ref, bdn_ref, o_ref, wgu_sc, wdn_sc):
    i = pl.program_id(0)
    e = be_ref[i]
    prev = be_ref[jnp.maximum(i - 1, 0)]

    @pl.when((i == 0) | (e != prev))
    def _():
        wgu_sc[...] = wgu_ref[0].astype(BF16)
        wdn_sc[...] = wdn_ref[0].astype(BF16)

    @pl.when(i < nu_ref[0])
    def _():
        xb = x_ref[...].astype(BF16)
        gu = _dot(xb, wgu_sc[...]) + bgu_ref[0]
        glu = jnp.minimum(gu[:, 0:D_EXPERT], SWIGLU_LIMIT)
        lin = jnp.clip(gu[:, D_EXPERT:2 * D_EXPERT], -SWIGLU_LIMIT, SWIGLU_LIMIT)
        act = glu * _sigmoid(SWIGLU_ALPHA * glu) * (lin + 1.0)
        o_ref[...] = _dot(act.astype(BF16), wdn_sc[...]) + bdn_ref[0]

    @pl.when(i >= nu_ref[0])
    def _():
        o_ref[...] = jnp.zeros(o_ref.shape, F32)


def _gmm(block_e, n_used, xb, wgu, bgu, wdn, bdn):
    P, D = xb.shape
    E, _, F2 = wgu.shape
    return pl.pallas_call(
        _gmm_kernel,
        grid_spec=pltpu.PrefetchScalarGridSpec(
            num_scalar_prefetch=2,
            grid=(P // MOE_BLK,),
            in_specs=[pl.BlockSpec((MOE_BLK, D), lambda i, be, nu: (i, 0)),
                      pl.BlockSpec((1, D, F2), lambda i, be, nu: (be[i], 0, 0)),
                      pl.BlockSpec((1, 1, F2), lambda i, be, nu: (be[i], 0, 0)),
                      pl.BlockSpec((1, F2 // 2, D), lambda i, be, nu: (be[i], 0, 0)),
                      pl.BlockSpec((1, 1, D), lambda i, be, nu: (be[i], 0, 0))],
            out_specs=pl.BlockSpec((MOE_BLK, D), lambda i, be, nu: (i, 0)),
            scratch_shapes=[pltpu.VMEM((D, F2), BF16), pltpu.VMEM((F2 // 2, D), BF16)]),
        out_shape=jax.ShapeDtypeStruct((P, D), F32),
        compiler_params=_params("arbitrary"),
        name="gmm",
    )(block_e, n_used, xb, wgu, bgu.reshape(E, 1, F2), wdn, bdn.reshape(E, 1, D))


def _combine_kernel(dest_ref, x_ref, gate_ref, g2_ref, fw_ref, yb_ref, o_ref, buf, sem, *, final):
    tm = x_ref.shape[0]

    def row_copy(t, kk, d):
        return pltpu.make_async_copy(yb_ref.at[pl.ds(d, 1)], buf.at[kk, pl.ds(t, 1)], sem)

    def issue(t, carry):
        for kk in range(TOP_K):
            row_copy(t, kk, dest_ref[t * TOP_K + kk]).start()
        return carry

    lax.fori_loop(0, tm, issue, 0)

    def drain(t, carry):
        for kk in range(TOP_K):
            row_copy(t, kk, dest_ref[t * TOP_K + kk]).wait()
        return carry

    lax.fori_loop(0, tm, drain, 0)

    gate = gate_ref[...]
    acc = gate[:, 0:1] * buf[0]
    for kk in range(1, TOP_K):
        acc = acc + gate[:, kk:kk + 1] * buf[kk]
    out = x_ref[...] + g2_ref[0] * acc
    if final:
        ms = jnp.mean(out * out, axis=-1, keepdims=True)
        out = out * lax.rsqrt(ms + EPS) * fw_ref[...]
    o_ref[...] = out


def _combine(dest, x, gates, mod, fw, yb, lay, final):
    T, D = x.shape
    nlb, bpb, B = lay["nlb"], lay["bpb"], lay["B"]
    mrow = lambda i: jnp.where(i < nlb, i // bpb, B)
    return pl.pallas_call(
        functools.partial(_combine_kernel, final=final),
        grid=(T // TM,),
        in_specs=[pl.BlockSpec((TM * TOP_K,), lambda i: (i,), memory_space=pltpu.SMEM),
                  pl.BlockSpec((TM, D), lambda i: (i, 0)),
                  pl.BlockSpec((TM, LANES), lambda i: (i, 0)),
                  pl.BlockSpec((1, 1, D), lambda i: (mrow(i), 0, 5)),
                  pl.BlockSpec((1, D), lambda i: (0, 0)),
                  pl.BlockSpec(memory_space=pl.ANY)],
        out_specs=pl.BlockSpec((TM, D), lambda i: (i, 0)),
        out_shape=jax.ShapeDtypeStruct((T, D), F32),
        scratch_shapes=[pltpu.VMEM((TOP_K, TM, D), F32), pltpu.SemaphoreType.DMA],
        compiler_params=_params("arbitrary"),
        name="combine",
    )(dest, x, gates, mod, fw, yb)


def _moe(x, nw, mod, wr, br, wgu, bgu, wdn, bdn, fw, lay, final):
    T, D = x.shape
    h, mi, mf, cnt = _route(x, nw, mod, wr, br, lay)
    counts = cnt[0, :N_EXPERTS].astype(I32)
    padded = (counts + MOE_BLK - 1) // MOE_BLK * MOE_BLK
    pad_ends = jnp.cumsum(padded)
    pad_starts = pad_ends - padded
    dest = (jnp.take(pad_starts, mi[:, 0:TOP_K]) + mi[:, TOP_K:2 * TOP_K]).reshape(-1)
    nblk = -(-(T * TOP_K) // MOE_BLK) + N_EXPERTS
    block_e = jnp.minimum(jnp.searchsorted(pad_ends, jnp.arange(nblk, dtype=I32) * MOE_BLK, side="right"),
                          N_EXPERTS - 1).astype(I32)
    n_used = (pad_ends[N_EXPERTS - 1:] // MOE_BLK).astype(I32)
    xb = _dispatch(dest, h, jnp.zeros((nblk * MOE_BLK, D), F32))
    yb = _gmm(block_e, n_used, xb, wgu, bgu, wdn, bdn)
    return _combine(dest, x, mf, mod, fw, yb, lay, final)


def _rope_tables(S):
    t = np.arange(S)
    row = (t // GRID_W).astype(np.float64)
    col = (t % GRID_W).astype(np.float64)
    lane = np.arange(LANES)
    j = lane % 16
    inv = ROPE_THETA ** (-(j.astype(np.float32)) / np.float32(16.0))
    pos = np.where((lane % 64) < 32, row[:, None], col[:, None]).astype(np.float32)
    ang = pos * inv.astype(np.float32)[None, :]
    sign = np.where((lane % 32) < 16, -1.0, 1.0).astype(np.float32)
    return ang.astype(np.float32), sign


def kernel(x, c, ctx, c_ctx, w_ada, b_ada, norm1_w, w_in, da_lambda, da_subln_w, ssd_conv_w, ssd_conv_b,
           ssd_a_log, ssd_dt_bias, ssd_d, ssd_norm_w, gla_gk_up, gla_gk_b, gla_norm_w, w_out, norm2_w,
           w_router, b_router, w_gate_up, b_gate_up, w_down, b_down, final_norm_w):
    B, S, D = x.shape
    Lc = ctx.shape[1]
    depth = w_ada.shape[0]
    assert S % TM == 0 and Lc % TM == 0 and S % GRID_W == 0
    lay = dict(B=B, S=S, Lc=Lc, nlb=(B * S) // TM, bpb=S // TM, cpb=Lc // TM)
    n_lat = B * S

    xs = jnp.concatenate([x.reshape(B * S, D), ctx.reshape(B * Lc, D)], axis=0)

    cc = jnp.zeros((8, D), F32).at[0:B].set(c).at[B].set(c_ctx)
    mod_all = _ada(cc, w_ada, b_ada)

    ang, sign = _rope_tables(S)
    ang = jnp.asarray(ang)
    cos_t = jnp.concatenate([jnp.cos(ang), jnp.ones((TM, LANES), F32)], axis=0)
    sin_t = jnp.concatenate([jnp.sin(ang) * jnp.asarray(sign)[None, :], jnp.zeros((TM, LANES), F32)], axis=0)

    for l in range(depth):
        last = l == depth - 1
        lam_init = 0.8 - 0.6 * math.exp(-0.3 * l)
        mod = mod_all[l, 0:B + 1].reshape(B + 1, 1, 6 * D)

        wi = w_in[l]
        misc_w = jnp.zeros((D, LANES), F32)
        misc_w = misc_w.at[:, MISC_DT:MISC_DT + 8].set(wi[:, 2304:2312])
        misc_w = misc_w.at[:, MISC_CODE:MISC_CODE + 2 * GLA_RANK].set(wi[:, 3080:3112])
        w_re = jnp.concatenate([wi[:, 0:2304], wi[:, 2312:3080], misc_w], axis=1).astype(BF16)
        a128 = jnp.zeros((1, LANES), F32).at[0, 0:8].set(-jnp.exp(ssd_a_log[l].astype(F32)).reshape(-1))
        bias128 = jnp.zeros((1, LANES), F32).at[0, 0:8].set(ssd_dt_bias[l].astype(F32).reshape(-1))
        wgk = jnp.zeros((2, LANES, GLA_KEY_WIDTH), F32)
        for d in range(2):
            wgk = wgk.at[d, MISC_CODE + d * GLA_RANK:MISC_CODE + (d + 1) * GLA_RANK, :].set(gla_gk_up[l, d])
        bgk = gla_gk_b[l].reshape(2, 1, GLA_KEY_WIDTH)
        dsk = jnp.repeat(ssd_d[l], SSD_HEAD_DIM).reshape(1, SSD_WIDTH)
        snw = ssd_norm_w[l].reshape(1, SSD_WIDTH)
        gnw = jnp.tile(gla_norm_w[l], GLA_HEADS).reshape(1, GLA_WIDTH)
        wr = jnp.zeros((D, LANES), F32).at[:, 0:N_EXPERTS].set(w_router[l])
        br = jnp.zeros((1, LANES), F32).at[0, 0:N_EXPERTS].set(b_router[l])

        q, k, v, z, xbc, gla_in, misc = _inproj(xs, norm1_w[l].reshape(1, D), mod, cos_t, sin_t, w_re, lay)

        a = _attention(q, k, v, da_lambda[l], da_subln_w[l].reshape(1, DA_V), lay, lam_init,
                       ctx_queries=not last)
        xbc_act = _conv(xbc, ssd_conv_w[l], ssd_conv_b[l].reshape(1, SSD_CONV_CH), lay)
        yf, yb = _ssd(xbc_act, misc, a128, bias128, lay)
        of, ob = _gla(gla_in, misc, wgk, bgk, lay)

        nrows = n_lat if last else xs.shape[0]
        xs = _outproj(xs, a, yf, yb, xbc_act, z, of, ob, gla_in, dsk, snw, gnw,
                      w_out[l].astype(BF16), mod, lay, nrows)
        xs = _moe(xs, norm2_w[l].reshape(1, D), mod, wr, br, w_gate_up[l], b_gate_up[l], w_down[l], b_down[l],
                  final_norm_w.reshape(1, D), lay, last)

    return xs.reshape(B, S, D)
```

```python
import functools
import math

import numpy as np
import jax
import jax.numpy as jnp
from jax import lax
from jax.experimental import pallas as pl
from jax.experimental.pallas import tpu as pltpu

F32 = jnp.float32
BF16 = jnp.bfloat16
I32 = jnp.int32

GRID_W = 64
EPS = 1e-6
DA_HEADS = 4
DA_QK = 64
DA_V = 128
DA_WIDTH = 512
ROPE_THETA = 10000.0
SSD_HEADS = 4
SSD_HEAD_DIM = 64
SSD_WIDTH = 256
SSD_STATE = 64
SSD_CONV_CH = 512
GLA_HEADS = 4
GLA_DK = 32
GLA_DV = 64
GLA_KEY_WIDTH = 128
GLA_WIDTH = 256
GLA_RANK = 16
GLA_NORMALIZER = 16.0
N_EXPERTS = 32
TOP_K = 4
D_EXPERT = 1024
SWIGLU_LIMIT = 7.0
SWIGLU_ALPHA = 1.702

LANES = 128
TM = 256
SSD_Q = 128
GLA_Q = 64
GLA_SUB = 4
ATT_TQ = 256
ATT_KT = (2816, 768, 512, 256)
LOG2E = 1.4426950408889634
MOE_BLK = 256
VMEM_LIMIT = 56 * 1024 * 1024

C_Q, C_K, C_V, C_Z, C_XBC, C_GLA, C_MISC, C_END = 0, 512, 1024, 1536, 1792, 2304, 3072, 3200
MISC_DT = 0
MISC_CODE = 8


def _sigmoid(x):
    return 1.0 / (1.0 + jnp.exp(-x))


def _softplus(x):
    return jnp.maximum(x, 0.0) + jnp.log(1.0 + jnp.exp(-jnp.abs(x)))


def _split2(a):
    hi = a.astype(BF16)
    lo = (a - hi.astype(F32)).astype(BF16)
    return hi, lo


def _split3(a):
    a1 = a.astype(BF16)
    r1 = a - a1.astype(F32)
    a2 = r1.astype(BF16)
    a3 = (r1 - a2.astype(F32)).astype(BF16)
    return a1, a2, a3


def _dot(a, b):
    return jnp.dot(a, b, preferred_element_type=F32)


def _dot_hi(a, b):
    a1, a2 = _split2(a)
    b1, b2 = _split2(b)
    return _dot(a1, b1) + (_dot(a1, b2) + _dot(a2, b1))


def _dot_exact_lhs(m, a):
    a1, a2, a3 = _split3(a)
    mb = m.astype(BF16)
    return _dot(mb, a1) + (_dot(mb, a2) + _dot(mb, a3))


def _dot_exact_rhs(a, m):
    a1, a2, a3 = _split3(a)
    mb = m.astype(BF16)
    return _dot(a1, mb) + (_dot(a2, mb) + _dot(a3, mb))


def _params(*sem):
    return pltpu.CompilerParams(dimension_semantics=sem, vmem_limit_bytes=VMEM_LIMIT)


def _ada_kernel(c_ref, w_ref, b_ref, o_ref):
    c = c_ref[...]
    s = c * _sigmoid(c)
    o_ref[0] = _dot_hi(s, w_ref[0]) + b_ref[0]


def _ada(cc, w_ada, b_ada):
    L, D, N = w_ada.shape
    tn = 512
    return pl.pallas_call(
        _ada_kernel,
        grid=(L, N // tn),
        in_specs=[pl.BlockSpec((8, D), lambda l, j: (0, 0)),
                  pl.BlockSpec((1, D, tn), lambda l, j: (l, 0, j)),
                  pl.BlockSpec((1, 1, tn), lambda l, j: (l, 0, j))],
        out_specs=pl.BlockSpec((1, 8, tn), lambda l, j: (l, 0, j)),
        out_shape=jax.ShapeDtypeStruct((L, 8, N), F32),
        compiler_params=_params("parallel", "parallel"),
        name="ada",
    )(cc, w_ada, b_ada.reshape(L, 1, N))


def _inproj_kernel(x_ref, nw_ref, sh_ref, sc_ref, cos_ref, sin_ref, w_ref,
                   q_ref, k_ref, v_ref, z_ref, xbc_ref, gla_ref, misc_ref):
    xf = x_ref[...]
    ms = jnp.mean(xf * xf, axis=-1, keepdims=True)
    h = (xf * lax.rsqrt(ms + EPS) * nw_ref[...]) * (1.0 + sc_ref[0]) + sh_ref[0]
    hb = h.astype(BF16)
    cos = cos_ref[...]
    sin = sin_ref[...]
    lane = lax.broadcasted_iota(I32, cos.shape, 1)
    first = (lane % 32) < 16

    def rope(p):
        outs = []
        for hd in range(DA_HEADS):
            ph = p[:, hd * LANES:(hd + 1) * LANES]
            sw = jnp.where(first, pltpu.roll(ph, LANES - 16, 1), pltpu.roll(ph, 16, 1))
            outs.append(ph * cos + sw * sin)
        return jnp.concatenate(outs, axis=1)

    q = rope(_dot(hb, w_ref[:, C_Q:C_K])) * (DA_QK ** -0.5 * LOG2E)
    q_ref[...] = q.astype(BF16)
    k_ref[...] = rope(_dot(hb, w_ref[:, C_K:C_V])).astype(BF16)
    v_ref[...] = _dot(hb, w_ref[:, C_V:C_Z]).astype(BF16)
    z_ref[...] = _dot(hb, w_ref[:, C_Z:C_XBC])
    xbc_ref[...] = _dot(hb, w_ref[:, C_XBC:C_GLA])
    gla_ref[...] = _dot(hb, w_ref[:, C_GLA:C_MISC])
    misc_ref[...] = _dot(hb, w_ref[:, C_MISC:C_END])


def _inproj(x, nw, mod, cos_t, sin_t, w, lay):
    T, D = x.shape
    nlb, bpb, B = lay["nlb"], lay["bpb"], lay["B"]
    mrow = lambda i: jnp.where(i < nlb, i // bpb, B)
    prow = lambda i: jnp.where(i < nlb, i % bpb, bpb)
    row = lambda n: pl.BlockSpec((TM, n), lambda i: (i, 0))
    spb = bpb + lay["cpb"]
    kvrow = lambda i: (jnp.where(i < nlb, (i // bpb) * spb + lay["cpb"] + i % bpb,
                                 ((i - nlb) // lay["cpb"]) * spb + (i - nlb) % lay["cpb"]), 0)
    kv = pl.BlockSpec((TM, DA_WIDTH), kvrow)
    outs = [(512, BF16), (512, BF16), (512, BF16), (256, F32), (512, F32), (768, F32), (128, F32)]
    out_specs = [row(n) for n, _ in outs]
    out_specs[1] = kv
    out_specs[2] = kv
    return pl.pallas_call(
        _inproj_kernel,
        grid=(T // TM,),
        in_specs=[row(D),
                  pl.BlockSpec((1, D), lambda i: (0, 0)),
                  pl.BlockSpec((1, 1, D), lambda i: (mrow(i), 0, 0)),
                  pl.BlockSpec((1, 1, D), lambda i: (mrow(i), 0, 1)),
                  pl.BlockSpec((TM, LANES), lambda i: (prow(i), 0)),
                  pl.BlockSpec((TM, LANES), lambda i: (prow(i), 0)),
                  pl.BlockSpec((D, C_END), lambda i: (0, 0))],
        out_specs=out_specs,
        out_shape=[jax.ShapeDtypeStruct((T, n), dt) for n, dt in outs],
        compiler_params=_params("parallel"),
        name="inproj",
    )(x, nw, mod, mod, cos_t, sin_t, w)


def _lane_fold(x, op):
    f = x[:, 0:LANES]
    for i in range(1, x.shape[1] // LANES):
        f = op(f, x[:, i * LANES:(i + 1) * LANES])
    return f


def _attn_two_pass(q2_sc, k_ref, v_ref, s_sc, p_sc, m_sc, l_sc, acc_sc, nt, kt):
    m_sc[...] = jnp.full(m_sc.shape, -jnp.inf, F32)
    l_sc[...] = jnp.zeros(l_sc.shape, F32)
    acc_sc[...] = jnp.zeros(acc_sc.shape, F32)

    def scores(j, carry):
        kj = k_ref[pl.ds(pl.multiple_of(j * kt, kt), kt), :]
        s = lax.dot_general(q2_sc[...], kj, (((1,), (1,)), ((), ())), preferred_element_type=F32)
        s_sc[j, :, 0:kt] = s
        m_sc[...] = jnp.maximum(m_sc[...], _lane_fold(s, jnp.maximum))
        return carry

    lax.fori_loop(0, nt, scores, 0)
    m = jnp.max(m_sc[...], axis=1, keepdims=True)

    sub = next(w for w in (512, 256) if kt % w == 0)

    def weigh(j, carry):
        for c0 in range(0, kt, sub):
            p = jnp.exp2(s_sc[j, :, c0:c0 + sub] - m)
            l_sc[...] += _lane_fold(p, jnp.add)
            p_sc[:, c0:c0 + sub] = p.astype(BF16)
        vj = v_ref[pl.ds(pl.multiple_of(j * kt, kt), kt), :]
        acc_sc[...] += _dot(p_sc[:, 0:kt], vj)
        return carry

    lax.fori_loop(0, nt, weigh, 0)


def _attn_kernel(q_ref, k_ref, v_ref, lam_ref, w_ref, o_ref, q2_sc, s_sc, p_sc, m_sc, l_sc, acc_sc,
                 *, nq, nqc, nt, kt, ntc, ktc, lam_init):
    qi = pl.program_id(2)
    tq = q_ref.shape[0]
    q = q_ref[...]
    lane = lax.broadcasted_iota(I32, q.shape, 1)
    zero = jnp.zeros_like(q)
    q2_sc[0:tq, :] = jnp.where(lane < DA_QK, q, zero)
    q2_sc[tq:2 * tq, :] = jnp.where(lane >= DA_QK, q, zero)

    if nqc == 0:
        _attn_two_pass(q2_sc, k_ref, v_ref, s_sc, p_sc, m_sc, l_sc, acc_sc, nt, kt)
    else:
        @pl.when(qi < nq)
        def _():
            _attn_two_pass(q2_sc, k_ref, v_ref, s_sc, p_sc, m_sc, l_sc, acc_sc, nt, kt)

        @pl.when(qi >= nq)
        def _():
            _attn_two_pass(q2_sc, k_ref, v_ref, s_sc, p_sc, m_sc, l_sc, acc_sc, ntc, ktc)

    lp = lam_ref[...]
    lam = (jnp.exp(jnp.sum(lp[0:1] * lp[1:2], axis=1, keepdims=True))
           - jnp.exp(jnp.sum(lp[2:3] * lp[3:4], axis=1, keepdims=True)) + lam_init)
    acc = acc_sc[...]
    l = jnp.sum(l_sc[...], axis=1, keepdims=True)
    o = acc[0:tq] / l[0:tq] - lam * (acc[tq:2 * tq] / l[tq:2 * tq])
    ms = jnp.mean(o * o, axis=-1, keepdims=True)
    o_ref[...] = (o * lax.rsqrt(ms + EPS) * w_ref[...]) * (1.0 - lam_init)


def _attention(q, k, v, lam_p, subln_w, lay, lam_init, *, ctx_queries):
    B, S, Lc = lay["B"], lay["S"], lay["Lc"]
    nkeys = Lc + S
    kt = next(t for t in ATT_KT if nkeys % t == 0)
    ktc = next(t for t in ATT_KT if Lc % t == 0)
    lat_qblk = (B * S) // ATT_TQ
    nq = S // ATT_TQ
    nqc = Lc // ATT_TQ if ctx_queries else 0
    qmap = lambda b, h, qi: (jnp.where(qi < nq, b * nq + qi, lat_qblk + b * nqc + qi - nq), h)
    kmap = lambda b, h, qi: (b, h)
    rows = B * S + (B * Lc if ctx_queries else 0)
    return pl.pallas_call(
        functools.partial(_attn_kernel, nq=nq, nqc=nqc, nt=nkeys // kt, kt=kt, ntc=Lc // ktc, ktc=ktc,
                          lam_init=lam_init),
        grid=(B, DA_HEADS, nq + nqc),
        in_specs=[pl.BlockSpec((ATT_TQ, LANES), qmap),
                  pl.BlockSpec((nkeys, LANES), kmap),
                  pl.BlockSpec((nkeys, LANES), kmap),
                  pl.BlockSpec((4, DA_QK), lambda b, h, qi: (0, 0)),
                  pl.BlockSpec((1, DA_V), lambda b, h, qi: (0, 0))],
        out_specs=pl.BlockSpec((ATT_TQ, LANES), qmap),
        out_shape=jax.ShapeDtypeStruct((rows, DA_WIDTH), F32),
        scratch_shapes=[pltpu.VMEM((2 * ATT_TQ, LANES), BF16),
                        pltpu.VMEM((nkeys // kt, 2 * ATT_TQ, kt), F32),
                        pltpu.VMEM((2 * ATT_TQ, kt), BF16),
                        pltpu.VMEM((2 * ATT_TQ, LANES), F32),
                        pltpu.VMEM((2 * ATT_TQ, LANES), F32),
                        pltpu.VMEM((2 * ATT_TQ, LANES), F32)],
        compiler_params=_params("parallel", "parallel", "arbitrary"),
        name="attn",
    )(q, k, v, lam_p, subln_w)


def _conv_kernel(flag_ref, x_ref, p_ref, n_ref, w_ref, b_ref, o_ref):
    i = pl.program_id(0)
    x = x_ref[...]
    keep_prev = (flag_ref[i, 0] == 0).astype(F32)
    keep_next = (flag_ref[i, 1] == 0).astype(F32)
    prev_row = p_ref[7:8, :] * keep_prev
    next_row = n_ref[0:1, :] * keep_next
    ridx = lax.broadcasted_iota(I32, x.shape, 0)
    xm1 = jnp.where(ridx == 0, prev_row, pltpu.roll(x, 1, 0))
    xp1 = jnp.where(ridx == x.shape[0] - 1, next_row, pltpu.roll(x, x.shape[0] - 1, 0))
    w = w_ref[...]
    y = w[0:1] * xm1 + w[1:2] * x + w[2:3] * xp1 + b_ref[...]
    o_ref[...] = y * _sigmoid(y)


def _conv(xbc, cw, cb, lay):
    T, C = xbc.shape
    nb = T // TM
    r8 = TM // 8
    flags = np.zeros((nb, 2), np.int32)
    for i in range(nb):
        if i < lay["nlb"]:
            flags[i, 0] = (i % lay["bpb"]) == 0
            flags[i, 1] = (i % lay["bpb"]) == lay["bpb"] - 1
        else:
            j = (i - lay["nlb"]) % lay["cpb"]
            flags[i, 0] = j == 0
            flags[i, 1] = j == lay["cpb"] - 1
    return pl.pallas_call(
        _conv_kernel,
        grid_spec=pltpu.PrefetchScalarGridSpec(
            num_scalar_prefetch=1,
            grid=(nb,),
            in_specs=[pl.BlockSpec((TM, C), lambda i, f: (i, 0)),
                      pl.BlockSpec((8, C), lambda i, f: (jnp.maximum(i * r8 - 1, 0), 0)),
                      pl.BlockSpec((8, C), lambda i, f: (jnp.minimum((i + 1) * r8, T // 8 - 1), 0)),
                      pl.BlockSpec((3, C), lambda i, f: (0, 0)),
                      pl.BlockSpec((1, C), lambda i, f: (0, 0))],
            out_specs=pl.BlockSpec((TM, C), lambda i, f: (i, 0))),
        out_shape=jax.ShapeDtypeStruct((T, C), F32),
        compiler_params=_params("parallel"),
        name="ssd_conv",
    )(jnp.asarray(flags), xbc, xbc, xbc, cw, cb)


def _ssd_dir(xbc, misc, a128, bias128, hs_ref, dirn, rev):
    Q = xbc.shape[0]
    dt = _softplus(misc + bias128)
    a = dt * a128
    ii = lax.broadcasted_iota(I32, (Q, Q), 0)
    jj = lax.broadcasted_iota(I32, (Q, Q), 1)
    M = (jj >= ii) if rev else (jj <= ii)
    acs = _dot_exact_lhs(M.astype(F32), a)
    acs_t = acs.T
    dt_t = dt.T
    xs = xbc[:, 0:SSD_WIDTH]
    Bt = xbc[:, SSD_WIDTH:SSD_WIDTH + 128].T
    Cm = xbc[:, SSD_WIDTH + 128:SSD_WIDTH + 256]
    last = 0 if rev else Q - 1
    ys = []
    for g in range(2):
        Cg = Cm[:, g * SSD_STATE:(g + 1) * SSD_STATE]
        Btg = Bt[g * SSD_STATE:(g + 1) * SSD_STATE, :]
        CB = _dot(Cg, Btg)
        for hh in range(2):
            h = 2 * g + hh
            li = dirn * SSD_HEADS + h
            acs_c = acs[:, li:li + 1]
            acs_r = acs_t[li:li + 1, :]
            dt_r = dt_t[li:li + 1, :]
            decay = jnp.exp(jnp.where(M, acs_c - acs_r, -jnp.inf))
            sc = CB * decay * dt_r
            x_h = xs[:, h * SSD_HEAD_DIM:(h + 1) * SSD_HEAD_DIM]
            hs = hs_ref[h]
            y = _dot(sc, x_h) + _dot(Cg * jnp.exp(acs_c), hs)
            tot = acs_r[:, last:last + 1]
            w_end = jnp.exp(tot - acs_r) * dt_r
            hs_ref[h] = jnp.exp(tot) * hs + _dot(Btg * w_end, x_h)
            ys.append(y)
    return jnp.concatenate(ys, axis=1)


def _ssd_kernel(xf_ref, mf_ref, xb_ref, mb_ref, a_ref, bias_ref, yf_ref, yb_ref, hf_sc, hb_sc):
    @pl.when(pl.program_id(1) == 0)
    def _():
        hf_sc[...] = jnp.zeros(hf_sc.shape, F32)
        hb_sc[...] = jnp.zeros(hb_sc.shape, F32)

    a128 = a_ref[...]
    bias128 = bias_ref[...]
    yf_ref[...] = _ssd_dir(xf_ref[...], mf_ref[...], a128, bias128, hf_sc, 0, False)
    yb_ref[...] = _ssd_dir(xb_ref[...], mb_ref[...], a128, bias128, hb_sc, 1, True)


def _scan_maps(lay, rows):
    ncc, ncl = lay["Lc"] // rows, lay["S"] // rows
    base = (lay["B"] * lay["S"]) // rows
    fmap = lambda b, c: (jnp.where(c < ncc, base + b * ncc + c, b * ncl + c - ncc), 0)
    bmap = lambda b, c: (jnp.where(c < ncc, base + b * ncc + (ncc - 1 - c),
                                   b * ncl + (ncl - 1 - (c - ncc))), 0)
    return ncc + ncl, fmap, bmap


def _ssd(xbc_act, misc, a128, bias128, lay):
    T = xbc_act.shape[0]
    nc, fmap, bmap = _scan_maps(lay, SSD_Q)
    vec = pl.BlockSpec((1, LANES), lambda b, c: (0, 0))
    return pl.pallas_call(
        _ssd_kernel,
        grid=(lay["B"], nc),
        in_specs=[pl.BlockSpec((SSD_Q, SSD_CONV_CH), fmap), pl.BlockSpec((SSD_Q, LANES), fmap),
                  pl.BlockSpec((SSD_Q, SSD_CONV_CH), bmap), pl.BlockSpec((SSD_Q, LANES), bmap),
                  vec, vec],
        out_specs=[pl.BlockSpec((SSD_Q, SSD_WIDTH), fmap), pl.BlockSpec((SSD_Q, SSD_WIDTH), bmap)],
        out_shape=[jax.ShapeDtypeStruct((T, SSD_WIDTH), F32)] * 2,
        scratch_shapes=[pltpu.VMEM((SSD_HEADS, SSD_STATE, SSD_HEAD_DIM), F32)] * 2,
        compiler_params=_params("parallel", "arbitrary"),
        name="ssd",
    )(xbc_act, misc, xbc_act, misc, a128, bias128)


def _gla_chunk(q, k, v, misc, wgk, bgk, s_ref, rev):
    Q = q.shape[0]
    pre = _dot_hi(misc, wgk) + bgk
    gk = (jnp.minimum(pre, 0.0) - jnp.log(1.0 + jnp.exp(-jnp.abs(pre)))) / GLA_NORMALIZER
    ii = lax.broadcasted_iota(I32, (Q, Q), 0)
    jj = lax.broadcasted_iota(I32, (Q, Q), 1)
    M = (jj >= ii) if rev else (jj <= ii)
    b = _dot_exact_lhs(M.astype(F32), gk)
    q_t = (q * (GLA_DK ** -0.5)) * jnp.exp(b)
    k_t = k * jnp.exp(-b)
    last = 0 if rev else Q - 1
    tot_row = b[last:last + 1, :]
    tot_col = jnp.sum(gk.T, axis=1, keepdims=True)
    k_end = k * jnp.exp(tot_row - b)
    lane_k = lax.broadcasted_iota(I32, (1, GLA_KEY_WIDTH), 1) // GLA_DK
    lane_v = lax.broadcasted_iota(I32, (1, GLA_WIDTH), 1) // GLA_DV
    S = s_ref[...]
    o = _dot(q_t, S)
    for h in range(GLA_HEADS):
        qh = jnp.where(lane_k == h, q_t, 0.0)
        att = lax.dot_general(qh, k_t, (((1,), (1,)), ((), ())), preferred_element_type=F32)
        att = jnp.where(M, att, 0.0)
        o = o + _dot(att, jnp.where(lane_v == h, v, 0.0))
    bd = (lax.broadcasted_iota(I32, S.shape, 0) // GLA_DK) == (lax.broadcasted_iota(I32, S.shape, 1) // GLA_DV)
    s_ref[...] = jnp.where(bd, S * jnp.exp(tot_col) + _dot(k_end.T, v), 0.0)
    return o


def _gla_kernel(gf_ref, mf_ref, gb_ref, mb_ref, wgk_ref, bgk_ref, of_ref, ob_ref, sf_sc, sb_sc):
    @pl.when(pl.program_id(1) == 0)
    def _():
        sf_sc[...] = jnp.zeros(sf_sc.shape, F32)
        sb_sc[...] = jnp.zeros(sb_sc.shape, F32)

    kw = GLA_KEY_WIDTH
    for j in range(GLA_SUB):
        r0 = j * GLA_Q
        blk = gf_ref[r0:r0 + GLA_Q, :]
        of_ref[r0:r0 + GLA_Q, :] = _gla_chunk(blk[:, 0:kw], blk[:, kw:2 * kw], blk[:, 2 * kw:2 * kw + GLA_WIDTH],
                                              mf_ref[r0:r0 + GLA_Q, :], wgk_ref[0], bgk_ref[0], sf_sc, False)
        r1 = (GLA_SUB - 1 - j) * GLA_Q
        blk = gb_ref[r1:r1 + GLA_Q, :]
        ob_ref[r1:r1 + GLA_Q, :] = _gla_chunk(blk[:, 0:kw], blk[:, kw:2 * kw], blk[:, 2 * kw:2 * kw + GLA_WIDTH],
                                              mb_ref[r1:r1 + GLA_Q, :], wgk_ref[1], bgk_ref[1], sb_sc, True)


def _gla(gla_in, misc, wgk, bgk, lay):
    T = gla_in.shape[0]
    R = GLA_Q * GLA_SUB
    nc, fmap, bmap = _scan_maps(lay, R)
    return pl.pallas_call(
        _gla_kernel,
        grid=(lay["B"], nc),
        in_specs=[pl.BlockSpec((R, 512), fmap), pl.BlockSpec((R, LANES), fmap),
                  pl.BlockSpec((R, 512), bmap), pl.BlockSpec((R, LANES), bmap),
                  pl.BlockSpec((2, LANES, GLA_KEY_WIDTH), lambda b, c: (0, 0, 0)),
                  pl.BlockSpec((2, 1, GLA_KEY_WIDTH), lambda b, c: (0, 0, 0))],
        out_specs=[pl.BlockSpec((R, GLA_WIDTH), fmap), pl.BlockSpec((R, GLA_WIDTH), bmap)],
        out_shape=[jax.ShapeDtypeStruct((T, GLA_WIDTH), F32)] * 2,
        scratch_shapes=[pltpu.VMEM((GLA_KEY_WIDTH, GLA_WIDTH), F32)] * 2,
        compiler_params=_params("parallel", "arbitrary"),
        name="gla",
    )(gla_in, misc, gla_in, misc, wgk, bgk)


def _outproj_kernel(x_ref, a_ref, yf_ref, yb_ref, xs_ref, z_ref, of_ref, ob_ref, g_ref,
                    dsk_ref, snw_ref, gnw_ref, w_ref, g1_ref, o_ref):
    y = yf_ref[...] + yb_ref[...] + dsk_ref[...] * xs_ref[...]
    z = z_ref[...]
    gs = y * (z * _sigmoid(z))
    half = SSD_WIDTH // 2
    parts = []
    for grp in range(2):
        seg = gs[:, grp * half:(grp + 1) * half]
        parts.append(seg * lax.rsqrt(jnp.mean(seg * seg, axis=-1, keepdims=True) + EPS))
    s = jnp.concatenate(parts, axis=1) * snw_ref[...]
    o = of_ref[...] + ob_ref[...]
    bd = ((lax.broadcasted_iota(I32, (GLA_WIDTH, GLA_WIDTH), 0) // GLA_DV)
          == (lax.broadcasted_iota(I32, (GLA_WIDTH, GLA_WIDTH), 1) // GLA_DV))
    ms = _dot_exact_rhs(o * o, jnp.where(bd, 1.0 / GLA_DV, 0.0))
    g = g_ref[...]
    c = (o * lax.rsqrt(ms + EPS) * gnw_ref[...]) * (g * _sigmoid(g))
    mix = jnp.concatenate([a_ref[...], s, c], axis=1).astype(BF16)
    o_ref[...] = x_ref[...] + g1_ref[0] * _dot(mix, w_ref[...])


def _outproj(x, a, yf, yb, xbc_act, z, of, ob, gla_in, dsk, snw, gnw, w_out, mod, lay, nrows):
    D = x.shape[1]
    nlb, bpb, B = lay["nlb"], lay["bpb"], lay["B"]
    mrow = lambda i: jnp.where(i < nlb, i // bpb, B)
    row = lambda n, cb=0: pl.BlockSpec((TM, n), lambda i: (i, cb))
    vec = lambda n: pl.BlockSpec((1, n), lambda i: (0, 0))
    return pl.pallas_call(
        _outproj_kernel,
        grid=(nrows // TM,),
        in_specs=[row(D), row(DA_WIDTH), row(SSD_WIDTH), row(SSD_WIDTH), row(SSD_WIDTH), row(SSD_WIDTH),
                  row(GLA_WIDTH), row(GLA_WIDTH), row(GLA_WIDTH, 2),
                  vec(SSD_WIDTH), vec(SSD_WIDTH), vec(GLA_WIDTH),
                  pl.BlockSpec((D, D), lambda i: (0, 0)),
                  pl.BlockSpec((1, 1, D), lambda i: (mrow(i), 0, 2))],
        out_specs=row(D),
        out_shape=jax.ShapeDtypeStruct((nrows, D), F32),
        compiler_params=_params("parallel"),
        name="outproj",
    )(x, a, yf, yb, xbc_act, z, of, ob, gla_in, dsk, snw, gnw, w_out, mod)


def _route_kernel(x_ref, nw_ref, sh_ref, sc_ref, wr_ref, br_ref,
                  h_ref, mi_ref, mf_ref, cnt_ref, carry_sc):
    i = pl.program_id(0)

    @pl.when(i == 0)
    def _():
        carry_sc[...] = jnp.zeros(carry_sc.shape, F32)

    xf = x_ref[...]
    ms = jnp.mean(xf * xf, axis=-1, keepdims=True)
    h = (xf * lax.rsqrt(ms + EPS) * nw_ref[...]) * (1.0 + sc_ref[0]) + sh_ref[0]
    h_ref[...] = h
    tm = h.shape[0]
    lane = lax.broadcasted_iota(I32, (tm, LANES), 1)
    logits = jnp.where(lane < N_EXPERTS, _dot_hi(h, wr_ref[...]) + br_ref[...], -jnp.inf)
    vals, idxs, hots = [], [], []
    l = logits
    for _ in range(TOP_K):
        m = jnp.max(l, axis=1, keepdims=True)
        idx = jnp.min(jnp.where(l == m, lane, LANES), axis=1, keepdims=True)
        hot = lane == idx
        vals.append(m)
        idxs.append(idx)
        hots.append(hot)
        l = jnp.where(hot, -jnp.inf, l)
    es = [jnp.exp(v - vals[0]) for v in vals]
    den = es[0] + es[1] + es[2] + es[3]
    hot_all = jnp.zeros((tm, LANES), F32)
    for hot in hots:
        hot_all = hot_all + hot.astype(F32)
    ii = lax.broadcasted_iota(I32, (tm, tm), 0)
    jj = lax.broadcasted_iota(I32, (tm, tm), 1)
    before = _dot((jj < ii).astype(BF16), hot_all.astype(BF16))
    rank_e = carry_sc[...] + before
    mi = jnp.zeros((tm, LANES), I32)
    mf = jnp.zeros((tm, LANES), F32)
    for kk in range(TOP_K):
        rank = jnp.sum(jnp.where(hots[kk], rank_e, 0.0), axis=1, keepdims=True).astype(I32)
        mi = jnp.where(lane == kk, idxs[kk], mi)
        mi = jnp.where(lane == TOP_K + kk, rank, mi)
        mf = jnp.where(lane == kk, es[kk] / den, mf)
    mi_ref[...] = mi
    mf_ref[...] = mf
    carry_sc[...] = carry_sc[...] + jnp.sum(hot_all, axis=0, keepdims=True)
    cnt_ref[...] = carry_sc[...]


def _route(x, nw, mod, wr, br, lay):
    T, D = x.shape
    nlb, bpb, B = lay["nlb"], lay["bpb"], lay["B"]
    mrow = lambda i: jnp.where(i < nlb, i // bpb, B)
    row = lambda n: pl.BlockSpec((TM, n), lambda i: (i, 0))
    return pl.pallas_call(
        _route_kernel,
        grid=(T // TM,),
        in_specs=[row(D),
                  pl.BlockSpec((1, D), lambda i: (0, 0)),
                  pl.BlockSpec((1, 1, D), lambda i: (mrow(i), 0, 3)),
                  pl.BlockSpec((1, 1, D), lambda i: (mrow(i), 0, 4)),
                  pl.BlockSpec((D, LANES), lambda i: (0, 0)),
                  pl.BlockSpec((1, LANES), lambda i: (0, 0))],
        out_specs=[row(D), row(LANES), row(LANES), pl.BlockSpec((1, LANES), lambda i: (0, 0))],
        out_shape=[jax.ShapeDtypeStruct((T, D), F32), jax.ShapeDtypeStruct((T, LANES), I32),
                   jax.ShapeDtypeStruct((T, LANES), F32), jax.ShapeDtypeStruct((1, LANES), F32)],
        scratch_shapes=[pltpu.VMEM((1, LANES), F32)],
        compiler_params=_params("arbitrary"),
        name="route",
    )(x, nw, mod, mod, wr, br)


def _dispatch_kernel(dest_ref, h_ref, xin_ref, xb_ref, sem):
    del xin_ref
    tm = h_ref.shape[0]

    def row_copy(t, d):
        return pltpu.make_async_copy(h_ref.at[pl.ds(t, 1)], xb_ref.at[pl.ds(d, 1)], sem)

    def issue(t, carry):
        for kk in range(TOP_K):
            row_copy(t, dest_ref[t * TOP_K + kk]).start()
        return carry

    lax.fori_loop(0, tm, issue, 0)

    def drain(t, carry):
        for kk in range(TOP_K):
            row_copy(t, dest_ref[t * TOP_K + kk]).wait()
        return carry

    lax.fori_loop(0, tm, drain, 0)


def _dispatch(dest, h, xb0):
    T, D = h.shape
    return pl.pallas_call(
        _dispatch_kernel,
        grid=(T // TM,),
        in_specs=[pl.BlockSpec((TM * TOP_K,), lambda i: (i,), memory_space=pltpu.SMEM),
                  pl.BlockSpec((TM, D), lambda i: (i, 0)),
                  pl.BlockSpec(memory_space=pl.ANY)],
        out_specs=pl.BlockSpec(memory_space=pl.ANY),
        out_shape=jax.ShapeDtypeStruct(xb0.shape, F32),
        scratch_shapes=[pltpu.SemaphoreType.DMA],
        input_output_aliases={2: 0},
        compiler_params=_params("arbitrary"),
        name="dispatch",
    )(dest, h, xb0)


def _gmm_kernel(be_ref, nu_ref, x_ref, wgu_ref, bgu_ref, wdn_ref, bdn_ref, o_ref, wgu_sc, wdn_sc):
    i = pl.program_id(0)
    e = be_ref[i]
    prev = be_ref[jnp.maximum(i - 1, 0)]

    @pl.when((i == 0) | (e != prev))
    def _():
        wgu_sc[...] = wgu_ref[0, 0].astype(BF16)
        wdn_sc[...] = wdn_ref[0, 0].astype(BF16)

    @pl.when(i < nu_ref[0])
    def _():
        xb = x_ref[...].astype(BF16)
        gu = _dot(xb, wgu_sc[...]) + bgu_ref[0, 0]
        glu = jnp.minimum(gu[:, 0:D_EXPERT], SWIGLU_LIMIT)
        lin = jnp.clip(gu[:, D_EXPERT:2 * D_EXPERT], -SWIGLU_LIMIT, SWIGLU_LIMIT)
        act = glu * _sigmoid(SWIGLU_ALPHA * glu) * (lin + 1.0)
        o_ref[...] = _dot(act.astype(BF16), wdn_sc[...]) + bdn_ref[0, 0]

    @pl.when(i >= nu_ref[0])
    def _():
        o_ref[...] = jnp.zeros(o_ref.shape, F32)


def _gmm(block_e, n_used, xb, wgu, bgu, wdn, bdn, l):
    P, D = xb.shape
    L, E, _, F2 = wgu.shape
    return pl.pallas_call(
        _gmm_kernel,
        grid_spec=pltpu.PrefetchScalarGridSpec(
            num_scalar_prefetch=2,
            grid=(P // MOE_BLK,),
            in_specs=[pl.BlockSpec((MOE_BLK, D), lambda i, be, nu: (i, 0)),
                      pl.BlockSpec((1, 1, D, F2), lambda i, be, nu: (l, be[i], 0, 0)),
                      pl.BlockSpec((1, 1, 1, F2), lambda i, be, nu: (l, be[i], 0, 0)),
                      pl.BlockSpec((1, 1, F2 // 2, D), lambda i, be, nu: (l, be[i], 0, 0)),
                      pl.BlockSpec((1, 1, 1, D), lambda i, be, nu: (l, be[i], 0, 0))],
            out_specs=pl.BlockSpec((MOE_BLK, D), lambda i, be, nu: (i, 0)),
            scratch_shapes=[pltpu.VMEM((D, F2), BF16), pltpu.VMEM((F2 // 2, D), BF16)]),
        out_shape=jax.ShapeDtypeStruct((P, D), F32),
        compiler_params=_params("arbitrary"),
        name="gmm",
    )(block_e, n_used, xb, wgu, bgu.reshape(L, E, 1, F2), wdn, bdn.reshape(L, E, 1, D))


def _combine_kernel(dest_ref, x_ref, gate_ref, g2_ref, fw_ref, yb_ref, o_ref, buf, sem, *, final):
    tm = x_ref.shape[0]

    def row_copy(t, kk, d):
        return pltpu.make_async_copy(yb_ref.at[pl.ds(d, 1)], buf.at[kk, pl.ds(t, 1)], sem)

    def issue(t, carry):
        for kk in range(TOP_K):
            row_copy(t, kk, dest_ref[t * TOP_K + kk]).start()
        return carry

    lax.fori_loop(0, tm, issue, 0)

    def drain(t, carry):
        for kk in range(TOP_K):
            row_copy(t, kk, dest_ref[t * TOP_K + kk]).wait()
        return carry

    lax.fori_loop(0, tm, drain, 0)

    gate = gate_ref[...]
    acc = gate[:, 0:1] * buf[0]
    for kk in range(1, TOP_K):
        acc = acc + gate[:, kk:kk + 1] * buf[kk]
    out = x_ref[...] + g2_ref[0] * acc
    if final:
        ms = jnp.mean(out * out, axis=-1, keepdims=True)
        out = out * lax.rsqrt(ms + EPS) * fw_ref[...]
    o_ref[...] = out


def _combine(dest, x, gates, mod, fw, yb, lay, final):
    T, D = x.shape
    nlb, bpb, B = lay["nlb"], lay["bpb"], lay["B"]
    mrow = lambda i: jnp.where(i < nlb, i // bpb, B)
    return pl.pallas_call(
        functools.partial(_combine_kernel, final=final),
        grid=(T // TM,),
        in_specs=[pl.BlockSpec((TM * TOP_K,), lambda i: (i,), memory_space=pltpu.SMEM),
                  pl.BlockSpec((TM, D), lambda i: (i, 0)),
                  pl.BlockSpec((TM, LANES), lambda i: (i, 0)),
                  pl.BlockSpec((1, 1, D), lambda i: (mrow(i), 0, 5)),
                  pl.BlockSpec((1, D), lambda i: (0, 0)),
                  pl.BlockSpec(memory_space=pl.ANY)],
        out_specs=pl.BlockSpec((TM, D), lambda i: (i, 0)),
        out_shape=jax.ShapeDtypeStruct((T, D), F32),
        scratch_shapes=[pltpu.VMEM((TOP_K, TM, D), F32), pltpu.SemaphoreType.DMA],
        compiler_params=_params("arbitrary"),
        name="combine",
    )(dest, x, gates, mod, fw, yb)


def _moe(x, nw, mod, wr, br, wgu, bgu, wdn, bdn, fw, lay, l, final):
    T, D = x.shape
    h, mi, mf, cnt = _route(x, nw, mod, wr, br, lay)
    counts = cnt[0, :N_EXPERTS].astype(I32)
    padded = (counts + MOE_BLK - 1) // MOE_BLK * MOE_BLK
    pad_ends = jnp.cumsum(padded)
    pad_starts = pad_ends - padded
    dest = (jnp.take(pad_starts, mi[:, 0:TOP_K]) + mi[:, TOP_K:2 * TOP_K]).reshape(-1)
    nblk = -(-(T * TOP_K) // MOE_BLK) + N_EXPERTS
    starts = jnp.arange(nblk, dtype=I32) * MOE_BLK
    block_e = jnp.minimum(jnp.sum((pad_ends[None, :] <= starts[:, None]).astype(I32), axis=1), N_EXPERTS - 1)
    n_used = (pad_ends[N_EXPERTS - 1:] // MOE_BLK).astype(I32)
    xb = _dispatch(dest, h, jnp.zeros((nblk * MOE_BLK, D), F32))
    yb = _gmm(block_e, n_used, xb, wgu, bgu, wdn, bdn, l)
    return _combine(dest, x, mf, mod, fw, yb, lay, final)


def _rope_tables(S):
    t = np.arange(S)
    row = (t // GRID_W).astype(np.float64)
    col = (t % GRID_W).astype(np.float64)
    lane = np.arange(LANES)
    j = lane % 16
    inv = ROPE_THETA ** (-(j.astype(np.float32)) / np.float32(16.0))
    pos = np.where((lane % 64) < 32, row[:, None], col[:, None]).astype(np.float32)
    ang = pos * inv.astype(np.float32)[None, :]
    sign = np.where((lane % 32) < 16, -1.0, 1.0).astype(np.float32)
    return ang.astype(np.float32), sign


def kernel(x, c, ctx, c_ctx, w_ada, b_ada, norm1_w, w_in, da_lambda, da_subln_w, ssd_conv_w, ssd_conv_b,
           ssd_a_log, ssd_dt_bias, ssd_d, ssd_norm_w, gla_gk_up, gla_gk_b, gla_norm_w, w_out, norm2_w,
           w_router, b_router, w_gate_up, b_gate_up, w_down, b_down, final_norm_w):
    B, S, D = x.shape
    Lc = ctx.shape[1]
    depth = w_ada.shape[0]
    assert S % TM == 0 and Lc % TM == 0 and S % GRID_W == 0
    lay = dict(B=B, S=S, Lc=Lc, nlb=(B * S) // TM, bpb=S // TM, cpb=Lc // TM)
    n_lat = B * S

    xs = jnp.concatenate([x.reshape(B * S, D), ctx.reshape(B * Lc, D)], axis=0)

    cc = jnp.zeros((8, D), F32).at[0:B].set(c).at[B].set(c_ctx)
    mod_all = _ada(cc, w_ada, b_ada)

    ang, sign = _rope_tables(S)
    ang = jnp.asarray(ang)
    cos_t = jnp.concatenate([jnp.cos(ang), jnp.ones((TM, LANES), F32)], axis=0)
    sin_t = jnp.concatenate([jnp.sin(ang) * jnp.asarray(sign)[None, :], jnp.zeros((TM, LANES), F32)], axis=0)

    for l in range(depth):
        last = l == depth - 1
        lam_init = 0.8 - 0.6 * math.exp(-0.3 * l)
        mod = mod_all[l, 0:B + 1].reshape(B + 1, 1, 6 * D)

        wi = w_in[l]
        misc_w = jnp.zeros((D, LANES), F32)
        misc_w = misc_w.at[:, MISC_DT:MISC_DT + 8].set(wi[:, 2304:2312])
        misc_w = misc_w.at[:, MISC_CODE:MISC_CODE + 2 * GLA_RANK].set(wi[:, 3080:3112])
        w_re = jnp.concatenate([wi[:, 0:2304], wi[:, 2312:3080], misc_w], axis=1).astype(BF16)
        a128 = jnp.zeros((1, LANES), F32).at[0, 0:8].set(-jnp.exp(ssd_a_log[l].astype(F32)).reshape(-1))
        bias128 = jnp.zeros((1, LANES), F32).at[0, 0:8].set(ssd_dt_bias[l].astype(F32).reshape(-1))
        wgk = jnp.zeros((2, LANES, GLA_KEY_WIDTH), F32)
        for d in range(2):
            wgk = wgk.at[d, MISC_CODE + d * GLA_RANK:MISC_CODE + (d + 1) * GLA_RANK, :].set(gla_gk_up[l, d])
        bgk = gla_gk_b[l].reshape(2, 1, GLA_KEY_WIDTH)
        dsk = jnp.repeat(ssd_d[l], SSD_HEAD_DIM).reshape(1, SSD_WIDTH)
        snw = ssd_norm_w[l].reshape(1, SSD_WIDTH)
        gnw = jnp.tile(gla_norm_w[l], GLA_HEADS).reshape(1, GLA_WIDTH)
        wr = jnp.zeros((D, LANES), F32).at[:, 0:N_EXPERTS].set(w_router[l])
        br = jnp.zeros((1, LANES), F32).at[0, 0:N_EXPERTS].set(b_router[l])

        q, k, v, z, xbc, gla_in, misc = _inproj(xs, norm1_w[l].reshape(1, D), mod, cos_t, sin_t, w_re, lay)

        a = _attention(q, k, v, da_lambda[l], da_subln_w[l].reshape(1, DA_V), lay, lam_init,
                       ctx_queries=not last)
        xbc_act = _conv(xbc, ssd_conv_w[l], ssd_conv_b[l].reshape(1, SSD_CONV_CH), lay)
        yf, yb = _ssd(xbc_act, misc, a128, bias128, lay)
        of, ob = _gla(gla_in, misc, wgk, bgk, lay)

        nrows = n_lat if last else xs.shape[0]
        xs = _outproj(xs, a, yf, yb, xbc_act, z, of, ob, gla_in, dsk, snw, gnw,
                      w_out[l].astype(BF16), mod, lay, nrows)
        xs = _moe(xs, norm2_w[l].reshape(1, D), mod, wr, br, w_gate_up, b_gate_up, w_down, b_down,
                  final_norm_w.reshape(1, D), lay, l, last)

    return xs.reshape(B, S, D)
```

```python
import functools
import math

import numpy as np
import jax
import jax.numpy as jnp
from jax import lax
from jax.experimental import pallas as pl
from jax.experimental.pallas import tpu as pltpu

F32 = jnp.float32
BF16 = jnp.bfloat16
I32 = jnp.int32

GRID_W = 64
EPS = 1e-6
DA_HEADS = 4
DA_QK = 64
DA_V = 128
DA_WIDTH = 512
ROPE_THETA = 10000.0
SSD_HEADS = 4
SSD_HEAD_DIM = 64
SSD_WIDTH = 256
SSD_STATE = 64
SSD_CONV_CH = 512
GLA_HEADS = 4
GLA_DK = 32
GLA_DV = 64
GLA_KEY_WIDTH = 128
GLA_WIDTH = 256
GLA_RANK = 16
GLA_NORMALIZER = 16.0
N_EXPERTS = 32
TOP_K = 4
D_EXPERT = 1024
SWIGLU_LIMIT = 7.0
SWIGLU_ALPHA = 1.702

LANES = 128
TM = 256
SSD_Q = 128
GLA_Q = 64
GLA_SUB = 4
ATT_TQ = 256
ATT_KT = (2816, 768, 512, 256)
LOG2E = 1.4426950408889634
MOE_BLK = 256
TMR = 512
SEG_ALIGN = 8
STG_ROWS = TMR * TOP_K + N_EXPERTS * SEG_ALIGN
VMEM_LIMIT = 56 * 1024 * 1024

C_Q, C_K, C_V, C_Z, C_XBC, C_GLA, C_MISC, C_END = 0, 512, 1024, 1536, 1792, 2304, 3072, 3200
MISC_DT = 0
MISC_CODE = 8


def _sigmoid(x):
    return 1.0 / (1.0 + jnp.exp(-x))


def _softplus(x):
    return jnp.maximum(x, 0.0) + jnp.log(1.0 + jnp.exp(-jnp.abs(x)))


def _split2(a):
    hi = a.astype(BF16)
    lo = (a - hi.astype(F32)).astype(BF16)
    return hi, lo


def _split3(a):
    a1 = a.astype(BF16)
    r1 = a - a1.astype(F32)
    a2 = r1.astype(BF16)
    a3 = (r1 - a2.astype(F32)).astype(BF16)
    return a1, a2, a3


def _dot(a, b):
    return jnp.dot(a, b, preferred_element_type=F32)


def _dot_hi(a, b):
    a1, a2 = _split2(a)
    b1, b2 = _split2(b)
    return _dot(a1, b1) + (_dot(a1, b2) + _dot(a2, b1))


def _dot_exact_lhs(m, a):
    a1, a2, a3 = _split3(a)
    mb = m.astype(BF16)
    return _dot(mb, a1) + (_dot(mb, a2) + _dot(mb, a3))


def _dot_exact_rhs(a, m):
    a1, a2, a3 = _split3(a)
    mb = m.astype(BF16)
    return _dot(a1, mb) + (_dot(a2, mb) + _dot(a3, mb))


def _params(*sem):
    return pltpu.CompilerParams(dimension_semantics=sem, vmem_limit_bytes=VMEM_LIMIT)


def _ada_kernel(c_ref, w_ref, b_ref, o_ref):
    c = c_ref[...]
    s = c * _sigmoid(c)
    o_ref[0] = _dot_hi(s, w_ref[0]) + b_ref[0]


def _ada(cc, w_ada, b_ada):
    L, D, N = w_ada.shape
    tn = 512
    return pl.pallas_call(
        _ada_kernel,
        grid=(L, N // tn),
        in_specs=[pl.BlockSpec((8, D), lambda l, j: (0, 0)),
                  pl.BlockSpec((1, D, tn), lambda l, j: (l, 0, j)),
                  pl.BlockSpec((1, 1, tn), lambda l, j: (l, 0, j))],
        out_specs=pl.BlockSpec((1, 8, tn), lambda l, j: (l, 0, j)),
        out_shape=jax.ShapeDtypeStruct((L, 8, N), F32),
        compiler_params=_params("parallel", "parallel"),
        name="ada",
    )(cc, w_ada, b_ada.reshape(L, 1, N))


def _inproj_kernel(x_ref, nw_ref, sh_ref, sc_ref, cos_ref, sin_ref, w_ref,
                   q_ref, k_ref, v_ref, z_ref, xbc_ref, gla_ref, misc_ref):
    xf = x_ref[...]
    ms = jnp.mean(xf * xf, axis=-1, keepdims=True)
    h = (xf * lax.rsqrt(ms + EPS) * nw_ref[...]) * (1.0 + sc_ref[0]) + sh_ref[0]
    hb = h.astype(BF16)
    cos = cos_ref[...]
    sin = sin_ref[...]
    lane = lax.broadcasted_iota(I32, cos.shape, 1)
    first = (lane % 32) < 16

    def rope(p):
        outs = []
        for hd in range(DA_HEADS):
            ph = p[:, hd * LANES:(hd + 1) * LANES]
            sw = jnp.where(first, pltpu.roll(ph, LANES - 16, 1), pltpu.roll(ph, 16, 1))
            outs.append(ph * cos + sw * sin)
        return jnp.concatenate(outs, axis=1)

    q = rope(_dot(hb, w_ref[:, C_Q:C_K])) * (DA_QK ** -0.5 * LOG2E)
    q_ref[...] = q.astype(BF16)
    k_ref[...] = rope(_dot(hb, w_ref[:, C_K:C_V])).astype(BF16)
    v_ref[...] = _dot(hb, w_ref[:, C_V:C_Z]).astype(BF16)
    z_ref[...] = _dot(hb, w_ref[:, C_Z:C_XBC])
    xbc_ref[...] = _dot(hb, w_ref[:, C_XBC:C_GLA])
    gla_ref[...] = _dot(hb, w_ref[:, C_GLA:C_MISC])
    misc_ref[...] = _dot(hb, w_ref[:, C_MISC:C_END])


def _inproj(x, nw, mod, cos_t, sin_t, w, lay):
    T, D = x.shape
    nlb, bpb, B = lay["nlb"], lay["bpb"], lay["B"]
    mrow = lambda i: jnp.where(i < nlb, i // bpb, B)
    prow = lambda i: jnp.where(i < nlb, i % bpb, bpb)
    row = lambda n: pl.BlockSpec((TM, n), lambda i: (i, 0))
    spb = bpb + lay["cpb"]
    kvrow = lambda i: (jnp.where(i < nlb, (i // bpb) * spb + lay["cpb"] + i % bpb,
                                 ((i - nlb) // lay["cpb"]) * spb + (i - nlb) % lay["cpb"]), 0)
    kv = pl.BlockSpec((TM, DA_WIDTH), kvrow)
    outs = [(512, BF16), (512, BF16), (512, BF16), (256, F32), (512, F32), (768, F32), (128, F32)]
    out_specs = [row(n) for n, _ in outs]
    out_specs[1] = kv
    out_specs[2] = kv
    return pl.pallas_call(
        _inproj_kernel,
        grid=(T // TM,),
        in_specs=[row(D),
                  pl.BlockSpec((1, D), lambda i: (0, 0)),
                  pl.BlockSpec((1, 1, D), lambda i: (mrow(i), 0, 0)),
                  pl.BlockSpec((1, 1, D), lambda i: (mrow(i), 0, 1)),
                  pl.BlockSpec((TM, LANES), lambda i: (prow(i), 0)),
                  pl.BlockSpec((TM, LANES), lambda i: (prow(i), 0)),
                  pl.BlockSpec((D, C_END), lambda i: (0, 0))],
        out_specs=out_specs,
        out_shape=[jax.ShapeDtypeStruct((T, n), dt) for n, dt in outs],
        compiler_params=_params("parallel"),
        name="inproj",
    )(x, nw, mod, mod, cos_t, sin_t, w)


def _lane_fold(x, op):
    f = x[:, 0:LANES]
    for i in range(1, x.shape[1] // LANES):
        f = op(f, x[:, i * LANES:(i + 1) * LANES])
    return f


def _attn_two_pass(q2_sc, k_ref, v_ref, s_sc, p_sc, m_sc, l_sc, acc_sc, nt, kt):
    m_sc[...] = jnp.full(m_sc.shape, -jnp.inf, F32)
    l_sc[...] = jnp.zeros(l_sc.shape, F32)
    acc_sc[...] = jnp.zeros(acc_sc.shape, F32)

    def scores(j, carry):
        kj = k_ref[pl.ds(pl.multiple_of(j * kt, kt), kt), :]
        s = lax.dot_general(q2_sc[...], kj, (((1,), (1,)), ((), ())), preferred_element_type=F32)
        s_sc[j, :, 0:kt] = s
        m_sc[...] = jnp.maximum(m_sc[...], _lane_fold(s, jnp.maximum))
        return carry

    lax.fori_loop(0, nt, scores, 0)
    m = jnp.max(m_sc[...], axis=1, keepdims=True)

    sub = next(w for w in (512, 256) if kt % w == 0)

    def weigh(j, carry):
        for c0 in range(0, kt, sub):
            p = jnp.exp2(s_sc[j, :, c0:c0 + sub] - m)
            l_sc[...] += _lane_fold(p, jnp.add)
            p_sc[:, c0:c0 + sub] = p.astype(BF16)
        vj = v_ref[pl.ds(pl.multiple_of(j * kt, kt), kt), :]
        acc_sc[...] += _dot(p_sc[:, 0:kt], vj)
        return carry

    lax.fori_loop(0, nt, weigh, 0)


def _attn_kernel(q_ref, k_ref, v_ref, lam_ref, w_ref, o_ref, q2_sc, s_sc, p_sc, m_sc, l_sc, acc_sc,
                 *, nq, nqc, nt, kt, ntc, ktc, lam_init):
    qi = pl.program_id(2)
    tq = q_ref.shape[0]
    q = q_ref[...]
    lane = lax.broadcasted_iota(I32, q.shape, 1)
    zero = jnp.zeros_like(q)
    q2_sc[0:tq, :] = jnp.where(lane < DA_QK, q, zero)
    q2_sc[tq:2 * tq, :] = jnp.where(lane >= DA_QK, q, zero)

    if nqc == 0:
        _attn_two_pass(q2_sc, k_ref, v_ref, s_sc, p_sc, m_sc, l_sc, acc_sc, nt, kt)
    else:
        @pl.when(qi < nq)
        def _():
            _attn_two_pass(q2_sc, k_ref, v_ref, s_sc, p_sc, m_sc, l_sc, acc_sc, nt, kt)

        @pl.when(qi >= nq)
        def _():
            _attn_two_pass(q2_sc, k_ref, v_ref, s_sc, p_sc, m_sc, l_sc, acc_sc, ntc, ktc)

    lp = lam_ref[...]
    lam = (jnp.exp(jnp.sum(lp[0:1] * lp[1:2], axis=1, keepdims=True))
           - jnp.exp(jnp.sum(lp[2:3] * lp[3:4], axis=1, keepdims=True)) + lam_init)
    acc = acc_sc[...]
    l = jnp.sum(l_sc[...], axis=1, keepdims=True)
    o = acc[0:tq] / l[0:tq] - lam * (acc[tq:2 * tq] / l[tq:2 * tq])
    ms = jnp.mean(o * o, axis=-1, keepdims=True)
    o_ref[...] = (o * lax.rsqrt(ms + EPS) * w_ref[...]) * (1.0 - lam_init)


def _attention(q, k, v, lam_p, subln_w, lay, lam_init, *, ctx_queries):
    B, S, Lc = lay["B"], lay["S"], lay["Lc"]
    nkeys = Lc + S
    kt = next(t for t in ATT_KT if nkeys % t == 0)
    ktc = next(t for t in ATT_KT if Lc % t == 0)
    lat_qblk = (B * S) // ATT_TQ
    nq = S // ATT_TQ
    nqc = Lc // ATT_TQ if ctx_queries else 0
    qmap = lambda b, h, qi: (jnp.where(qi < nq, b * nq + qi, lat_qblk + b * nqc + qi - nq), h)
    kmap = lambda b, h, qi: (b, h)
    rows = B * S + (B * Lc if ctx_queries else 0)
    return pl.pallas_call(
        functools.partial(_attn_kernel, nq=nq, nqc=nqc, nt=nkeys // kt, kt=kt, ntc=Lc // ktc, ktc=ktc,
                          lam_init=lam_init),
        grid=(B, DA_HEADS, nq + nqc),
        in_specs=[pl.BlockSpec((ATT_TQ, LANES), qmap),
                  pl.BlockSpec((nkeys, LANES), kmap),
                  pl.BlockSpec((nkeys, LANES), kmap),
                  pl.BlockSpec((4, DA_QK), lambda b, h, qi: (0, 0)),
                  pl.BlockSpec((1, DA_V), lambda b, h, qi: (0, 0))],
        out_specs=pl.BlockSpec((ATT_TQ, LANES), qmap),
        out_shape=jax.ShapeDtypeStruct((rows, DA_WIDTH), F32),
        scratch_shapes=[pltpu.VMEM((2 * ATT_TQ, LANES), BF16),
                        pltpu.VMEM((nkeys // kt, 2 * ATT_TQ, kt), F32),
                        pltpu.VMEM((2 * ATT_TQ, kt), BF16),
                        pltpu.VMEM((2 * ATT_TQ, LANES), F32),
                        pltpu.VMEM((2 * ATT_TQ, LANES), F32),
                        pltpu.VMEM((2 * ATT_TQ, LANES), F32)],
        compiler_params=_params("parallel", "parallel", "arbitrary"),
        name="attn",
    )(q, k, v, lam_p, subln_w)


def _conv_kernel(flag_ref, x_ref, p_ref, n_ref, w_ref, b_ref, o_ref):
    i = pl.program_id(0)
    x = x_ref[...]
    keep_prev = (flag_ref[i, 0] == 0).astype(F32)
    keep_next = (flag_ref[i, 1] == 0).astype(F32)
    prev_row = p_ref[7:8, :] * keep_prev
    next_row = n_ref[0:1, :] * keep_next
    ridx = lax.broadcasted_iota(I32, x.shape, 0)
    xm1 = jnp.where(ridx == 0, prev_row, pltpu.roll(x, 1, 0))
    xp1 = jnp.where(ridx == x.shape[0] - 1, next_row, pltpu.roll(x, x.shape[0] - 1, 0))
    w = w_ref[...]
    y = w[0:1] * xm1 + w[1:2] * x + w[2:3] * xp1 + b_ref[...]
    o_ref[...] = y * _sigmoid(y)


def _conv(xbc, cw, cb, lay):
    T, C = xbc.shape
    nb = T // TM
    r8 = TM // 8
    flags = np.zeros((nb, 2), np.int32)
    for i in range(nb):
        if i < lay["nlb"]:
            flags[i, 0] = (i % lay["bpb"]) == 0
            flags[i, 1] = (i % lay["bpb"]) == lay["bpb"] - 1
        else:
            j = (i - lay["nlb"]) % lay["cpb"]
            flags[i, 0] = j == 0
            flags[i, 1] = j == lay["cpb"] - 1
    return pl.pallas_call(
        _conv_kernel,
        grid_spec=pltpu.PrefetchScalarGridSpec(
            num_scalar_prefetch=1,
            grid=(nb,),
            in_specs=[pl.BlockSpec((TM, C), lambda i, f: (i, 0)),
                      pl.BlockSpec((8, C), lambda i, f: (jnp.maximum(i * r8 - 1, 0), 0)),
                      pl.BlockSpec((8, C), lambda i, f: (jnp.minimum((i + 1) * r8, T // 8 - 1), 0)),
                      pl.BlockSpec((3, C), lambda i, f: (0, 0)),
                      pl.BlockSpec((1, C), lambda i, f: (0, 0))],
            out_specs=pl.BlockSpec((TM, C), lambda i, f: (i, 0))),
        out_shape=jax.ShapeDtypeStruct((T, C), F32),
        compiler_params=_params("parallel"),
        name="ssd_conv",
    )(jnp.asarray(flags), xbc, xbc, xbc, cw, cb)


def _ssd_dir(xbc, misc, a128, bias128, hs_ref, dirn, rev):
    Q = xbc.shape[0]
    dt = _softplus(misc + bias128)
    a = dt * a128
    ii = lax.broadcasted_iota(I32, (Q, Q), 0)
    jj = lax.broadcasted_iota(I32, (Q, Q), 1)
    M = (jj >= ii) if rev else (jj <= ii)
    acs = _dot_exact_lhs(M.astype(F32), a)
    acs_t = acs.T
    dt_t = dt.T
    xs = xbc[:, 0:SSD_WIDTH]
    Bt = xbc[:, SSD_WIDTH:SSD_WIDTH + 128].T
    Cm = xbc[:, SSD_WIDTH + 128:SSD_WIDTH + 256]
    last = 0 if rev else Q - 1
    ys = []
    for g in range(2):
        Cg = Cm[:, g * SSD_STATE:(g + 1) * SSD_STATE]
        Btg = Bt[g * SSD_STATE:(g + 1) * SSD_STATE, :]
        CB = _dot(Cg, Btg)
        for hh in range(2):
            h = 2 * g + hh
            li = dirn * SSD_HEADS + h
            acs_c = acs[:, li:li + 1]
            acs_r = acs_t[li:li + 1, :]
            dt_r = dt_t[li:li + 1, :]
            decay = jnp.exp(jnp.where(M, acs_c - acs_r, -jnp.inf))
            sc = CB * decay * dt_r
            x_h = xs[:, h * SSD_HEAD_DIM:(h + 1) * SSD_HEAD_DIM]
            hs = hs_ref[h]
            y = _dot(sc, x_h) + _dot(Cg * jnp.exp(acs_c), hs)
            tot = acs_r[:, last:last + 1]
            w_end = jnp.exp(tot - acs_r) * dt_r
            hs_ref[h] = jnp.exp(tot) * hs + _dot(Btg * w_end, x_h)
            ys.append(y)
    return jnp.concatenate(ys, axis=1)


def _ssd_kernel(xf_ref, mf_ref, xb_ref, mb_ref, a_ref, bias_ref, yf_ref, yb_ref, hf_sc, hb_sc):
    @pl.when(pl.program_id(1) == 0)
    def _():
        hf_sc[...] = jnp.zeros(hf_sc.shape, F32)
        hb_sc[...] = jnp.zeros(hb_sc.shape, F32)

    a128 = a_ref[...]
    bias128 = bias_ref[...]
    yf_ref[...] = _ssd_dir(xf_ref[...], mf_ref[...], a128, bias128, hf_sc, 0, False)
    yb_ref[...] = _ssd_dir(xb_ref[...], mb_ref[...], a128, bias128, hb_sc, 1, True)


def _scan_maps(lay, rows):
    ncc, ncl = lay["Lc"] // rows, lay["S"] // rows
    base = (lay["B"] * lay["S"]) // rows
    fmap = lambda b, c: (jnp.where(c < ncc, base + b * ncc + c, b * ncl + c - ncc), 0)
    bmap = lambda b, c: (jnp.where(c < ncc, base + b * ncc + (ncc - 1 - c),
                                   b * ncl + (ncl - 1 - (c - ncc))), 0)
    return ncc + ncl, fmap, bmap


def _ssd(xbc_act, misc, a128, bias128, lay):
    T = xbc_act.shape[0]
    nc, fmap, bmap = _scan_maps(lay, SSD_Q)
    vec = pl.BlockSpec((1, LANES), lambda b, c: (0, 0))
    return pl.pallas_call(
        _ssd_kernel,
        grid=(lay["B"], nc),
        in_specs=[pl.BlockSpec((SSD_Q, SSD_CONV_CH), fmap), pl.BlockSpec((SSD_Q, LANES), fmap),
                  pl.BlockSpec((SSD_Q, SSD_CONV_CH), bmap), pl.BlockSpec((SSD_Q, LANES), bmap),
                  vec, vec],
        out_specs=[pl.BlockSpec((SSD_Q, SSD_WIDTH), fmap), pl.BlockSpec((SSD_Q, SSD_WIDTH), bmap)],
        out_shape=[jax.ShapeDtypeStruct((T, SSD_WIDTH), F32)] * 2,
        scratch_shapes=[pltpu.VMEM((SSD_HEADS, SSD_STATE, SSD_HEAD_DIM), F32)] * 2,
        compiler_params=_params("parallel", "arbitrary"),
        name="ssd",
    )(xbc_act, misc, xbc_act, misc, a128, bias128)


def _gla_block(blk, misc, wgk, bgk, s_ref, rev):
    R = blk.shape[0]
    kw = GLA_KEY_WIDTH
    q, k, v = blk[:, 0:kw], blk[:, kw:2 * kw], blk[:, 2 * kw:2 * kw + GLA_WIDTH]
    pre = _dot_hi(misc, wgk) + bgk
    gk = (jnp.minimum(pre, 0.0) - jnp.log(1.0 + jnp.exp(-jnp.abs(pre)))) / GLA_NORMALIZER
    ii = lax.broadcasted_iota(I32, (R, R), 0)
    jj = lax.broadcasted_iota(I32, (R, R), 1)
    same = (ii // GLA_Q) == (jj // GLA_Q)
    M = same & ((jj >= ii) if rev else (jj <= ii))
    b = _dot_exact_lhs(M.astype(F32), gk)
    tot = _dot_exact_lhs(same.astype(F32), gk)
    q_t = (q * (GLA_DK ** -0.5)) * jnp.exp(b)
    k_t = k * jnp.exp(-b)
    k_end_t = (k * jnp.exp(tot - b)).T
    gk_t = gk.T
    lane_k = lax.broadcasted_iota(I32, (1, kw), 1) // GLA_DK
    lane_v = lax.broadcasted_iota(I32, (1, GLA_WIDTH), 1) // GLA_DV
    o = jnp.zeros((R, GLA_WIDTH), F32)
    for h in range(GLA_HEADS):
        qh = jnp.where(lane_k == h, q_t, 0.0)
        att = lax.dot_general(qh, k_t, (((1,), (1,)), ((), ())), preferred_element_type=F32)
        o = o + _dot(jnp.where(M, att, 0.0), jnp.where(lane_v == h, v, 0.0))
    col_chunk = lax.broadcasted_iota(I32, (1, R), 1) // GLA_Q
    S = s_ref[...]
    bd = (lax.broadcasted_iota(I32, S.shape, 0) // GLA_DK) == (lax.broadcasted_iota(I32, S.shape, 1) // GLA_DV)
    nsub = R // GLA_Q
    inter = [None] * nsub
    for c in (range(nsub - 1, -1, -1) if rev else range(nsub)):
        sel = col_chunk == c
        inter[c] = _dot(q_t[c * GLA_Q:(c + 1) * GLA_Q], S)
        decay = jnp.exp(jnp.sum(jnp.where(sel, gk_t, 0.0), axis=1, keepdims=True))
        S = jnp.where(bd, S * decay + _dot(jnp.where(sel, k_end_t, 0.0), v), 0.0)
    s_ref[...] = S
    return o + jnp.concatenate(inter, axis=0)


def _gla_kernel(gf_ref, mf_ref, gb_ref, mb_ref, wgk_ref, bgk_ref, of_ref, ob_ref, sf_sc, sb_sc):
    @pl.when(pl.program_id(1) == 0)
    def _():
        sf_sc[...] = jnp.zeros(sf_sc.shape, F32)
        sb_sc[...] = jnp.zeros(sb_sc.shape, F32)

    of_ref[...] = _gla_block(gf_ref[...], mf_ref[...], wgk_ref[0], bgk_ref[0], sf_sc, False)
    ob_ref[...] = _gla_block(gb_ref[...], mb_ref[...], wgk_ref[1], bgk_ref[1], sb_sc, True)


def _gla(gla_in, misc, wgk, bgk, lay):
    T = gla_in.shape[0]
    R = GLA_Q * GLA_SUB
    nc, fmap, bmap = _scan_maps(lay, R)
    return pl.pallas_call(
        _gla_kernel,
        grid=(lay["B"], nc),
        in_specs=[pl.BlockSpec((R, 512), fmap), pl.BlockSpec((R, LANES), fmap),
                  pl.BlockSpec((R, 512), bmap), pl.BlockSpec((R, LANES), bmap),
                  pl.BlockSpec((2, LANES, GLA_KEY_WIDTH), lambda b, c: (0, 0, 0)),
                  pl.BlockSpec((2, 1, GLA_KEY_WIDTH), lambda b, c: (0, 0, 0))],
        out_specs=[pl.BlockSpec((R, GLA_WIDTH), fmap), pl.BlockSpec((R, GLA_WIDTH), bmap)],
        out_shape=[jax.ShapeDtypeStruct((T, GLA_WIDTH), F32)] * 2,
        scratch_shapes=[pltpu.VMEM((GLA_KEY_WIDTH, GLA_WIDTH), F32)] * 2,
        compiler_params=_params("parallel", "arbitrary"),
        name="gla",
    )(gla_in, misc, gla_in, misc, wgk, bgk)


def _outproj_kernel(x_ref, a_ref, yf_ref, yb_ref, xs_ref, z_ref, of_ref, ob_ref, g_ref,
                    dsk_ref, snw_ref, gnw_ref, w_ref, g1_ref, o_ref):
    y = yf_ref[...] + yb_ref[...] + dsk_ref[...] * xs_ref[...]
    z = z_ref[...]
    gs = y * (z * _sigmoid(z))
    half = SSD_WIDTH // 2
    parts = []
    for grp in range(2):
        seg = gs[:, grp * half:(grp + 1) * half]
        parts.append(seg * lax.rsqrt(jnp.mean(seg * seg, axis=-1, keepdims=True) + EPS))
    s = jnp.concatenate(parts, axis=1) * snw_ref[...]
    o = of_ref[...] + ob_ref[...]
    bd = ((lax.broadcasted_iota(I32, (GLA_WIDTH, GLA_WIDTH), 0) // GLA_DV)
          == (lax.broadcasted_iota(I32, (GLA_WIDTH, GLA_WIDTH), 1) // GLA_DV))
    ms = _dot_exact_rhs(o * o, jnp.where(bd, 1.0 / GLA_DV, 0.0))
    g = g_ref[...]
    c = (o * lax.rsqrt(ms + EPS) * gnw_ref[...]) * (g * _sigmoid(g))
    mix = jnp.concatenate([a_ref[...], s, c], axis=1).astype(BF16)
    o_ref[...] = x_ref[...] + g1_ref[0] * _dot(mix, w_ref[...])


def _outproj(x, a, yf, yb, xbc_act, z, of, ob, gla_in, dsk, snw, gnw, w_out, mod, lay, nrows):
    D = x.shape[1]
    nlb, bpb, B = lay["nlb"], lay["bpb"], lay["B"]
    mrow = lambda i: jnp.where(i < nlb, i // bpb, B)
    row = lambda n, cb=0: pl.BlockSpec((TM, n), lambda i: (i, cb))
    vec = lambda n: pl.BlockSpec((1, n), lambda i: (0, 0))
    return pl.pallas_call(
        _outproj_kernel,
        grid=(nrows // TM,),
        in_specs=[row(D), row(DA_WIDTH), row(SSD_WIDTH), row(SSD_WIDTH), row(SSD_WIDTH), row(SSD_WIDTH),
                  row(GLA_WIDTH), row(GLA_WIDTH), row(GLA_WIDTH, 2),
                  vec(SSD_WIDTH), vec(SSD_WIDTH), vec(GLA_WIDTH),
                  pl.BlockSpec((D, D), lambda i: (0, 0)),
                  pl.BlockSpec((1, 1, D), lambda i: (mrow(i), 0, 2))],
        out_specs=row(D),
        out_shape=jax.ShapeDtypeStruct((nrows, D), F32),
        compiler_params=_params("parallel"),
        name="outproj",
    )(x, a, yf, yb, xbc_act, z, of, ob, gla_in, dsk, snw, gnw, w_out, mod)


def _route_kernel(x_ref, nw_ref, sh_ref, sc_ref, wr_ref, br_ref,
                  h_ref, mi_ref, mf_ref, blk_ref, tot_ref, carry_sc):
    i = pl.program_id(0)

    @pl.when(i == 0)
    def _():
        carry_sc[...] = jnp.zeros(carry_sc.shape, F32)

    xf = x_ref[...]
    ms = jnp.mean(xf * xf, axis=-1, keepdims=True)
    h = (xf * lax.rsqrt(ms + EPS) * nw_ref[...]) * (1.0 + sc_ref[0]) + sh_ref[0]
    h_ref[...] = h
    tm = h.shape[0]
    lane = lax.broadcasted_iota(I32, (tm, LANES), 1)
    logits = jnp.where(lane < N_EXPERTS, _dot_hi(h, wr_ref[...]) + br_ref[...], -jnp.inf)
    vals, idxs, hots = [], [], []
    l = logits
    for _ in range(TOP_K):
        m = jnp.max(l, axis=1, keepdims=True)
        idx = jnp.min(jnp.where(l == m, lane, LANES), axis=1, keepdims=True)
        hot = lane == idx
        vals.append(m)
        idxs.append(idx)
        hots.append(hot)
        l = jnp.where(hot, -jnp.inf, l)
    es = [jnp.exp(v - vals[0]) for v in vals]
    den = es[0] + es[1] + es[2] + es[3]
    hot_all = jnp.zeros((tm, LANES), F32)
    for hot in hots:
        hot_all = hot_all + hot.astype(F32)
    ii = lax.broadcasted_iota(I32, (tm, tm), 0)
    jj = lax.broadcasted_iota(I32, (tm, tm), 1)
    before = _dot((jj < ii).astype(BF16), hot_all.astype(BF16))
    cnt = jnp.sum(hot_all, axis=0, keepdims=True)
    cnt8 = jnp.floor((cnt + (SEG_ALIGN - 1.0)) * (1.0 / SEG_ALIGN)) * SEG_ALIGN
    ei = lax.broadcasted_iota(I32, (LANES, LANES), 0)
    ej = lax.broadcasted_iota(I32, (LANES, LANES), 1)
    units = jnp.broadcast_to(cnt8 * (1.0 / SEG_ALIGN), (8, LANES)).astype(BF16)
    seg = _dot(units, (ei < ej).astype(BF16))[0:1] * SEG_ALIGN
    pos_e = seg + before
    mi = jnp.zeros((tm, LANES), I32)
    mf = jnp.zeros((tm, LANES), F32)
    for kk in range(TOP_K):
        spos = jnp.sum(jnp.where(hots[kk], pos_e, 0.0), axis=1, keepdims=True).astype(I32)
        mi = jnp.where(lane == kk, idxs[kk], mi)
        mi = jnp.where(lane == TOP_K + kk, spos, mi)
        mf = jnp.where(lane == kk, es[kk] / den, mf)
    mi_ref[...] = mi
    mf_ref[...] = mf
    rowi = lax.broadcasted_iota(I32, (8, LANES), 0)
    info = jnp.where(rowi == 0, cnt8, jnp.where(rowi == 1, seg, jnp.where(rowi == 2, carry_sc[...], 0.0)))
    blk_ref[0] = info.astype(I32)
    carry_sc[...] = carry_sc[...] + cnt8
    tot_ref[...] = carry_sc[...]


def _route(x, nw, mod, wr, br, lay):
    T, D = x.shape
    B, S = lay["B"], lay["S"]
    mrow = lambda i: jnp.where(i < (B * S) // TMR, i // (S // TMR), B)
    row = lambda n: pl.BlockSpec((TMR, n), lambda i: (i, 0))
    return pl.pallas_call(
        _route_kernel,
        grid=(T // TMR,),
        in_specs=[row(D),
                  pl.BlockSpec((1, D), lambda i: (0, 0)),
                  pl.BlockSpec((1, 1, D), lambda i: (mrow(i), 0, 3)),
                  pl.BlockSpec((1, 1, D), lambda i: (mrow(i), 0, 4)),
                  pl.BlockSpec((D, LANES), lambda i: (0, 0)),
                  pl.BlockSpec((1, LANES), lambda i: (0, 0))],
        out_specs=[row(D), row(LANES), row(LANES),
                   pl.BlockSpec((1, 8, LANES), lambda i: (i, 0, 0)),
                   pl.BlockSpec((1, LANES), lambda i: (0, 0))],
        out_shape=[jax.ShapeDtypeStruct((T, D), F32), jax.ShapeDtypeStruct((T, LANES), I32),
                   jax.ShapeDtypeStruct((T, LANES), F32),
                   jax.ShapeDtypeStruct((T // TMR, 8, LANES), I32),
                   jax.ShapeDtypeStruct((1, LANES), F32)],
        scratch_shapes=[pltpu.VMEM((1, LANES), F32)],
        compiler_params=_params("arbitrary"),
        name="route",
    )(x, nw, mod, mod, wr, br)


def _segment_copies(info_ref, make_copy):
    def start_expert(e, total):
        n = info_ref[e] // SEG_ALIGN
        src0 = info_ref[N_EXPERTS + e]
        dst0 = info_ref[2 * N_EXPERTS + e]

        def start_chunk(c, carry):
            make_copy(pl.multiple_of(src0 + c * SEG_ALIGN, SEG_ALIGN),
                      pl.multiple_of(dst0 + c * SEG_ALIGN, SEG_ALIGN)).start()
            return carry

        lax.fori_loop(0, n, start_chunk, 0)
        return total + n

    total = lax.fori_loop(0, N_EXPERTS, start_expert, 0)

    def wait_chunk(c, carry):
        make_copy(0, 0).wait()
        return carry

    lax.fori_loop(0, total, wait_chunk, 0)


def _dispatch_kernel(info_ref, h_ref, mi_ref, xin_ref, xb_ref, stg_sc, sem):
    del xin_ref
    tm = h_ref.shape[0]
    spos_t = mi_ref[...].astype(F32).T
    r = lax.broadcasted_iota(I32, (stg_sc.shape[0], tm), 0).astype(F32)
    pm = r == spos_t[TOP_K:TOP_K + 1]
    for kk in range(1, TOP_K):
        pm = pm | (r == spos_t[TOP_K + kk:TOP_K + kk + 1])
    stg_sc[...] = _dot(pm.astype(BF16), h_ref[...].astype(BF16))
    _segment_copies(info_ref, lambda s, d: pltpu.make_async_copy(
        stg_sc.at[pl.ds(s, SEG_ALIGN)], xb_ref.at[pl.ds(d, SEG_ALIGN)], sem))


def _dispatch(info, h, mi, xb0):
    T, D = h.shape
    return pl.pallas_call(
        _dispatch_kernel,
        grid=(T // TMR,),
        in_specs=[pl.BlockSpec((LANES,), lambda i: (i,), memory_space=pltpu.SMEM),
                  pl.BlockSpec((TMR, D), lambda i: (i, 0)),
                  pl.BlockSpec((TMR, LANES), lambda i: (i, 0)),
                  pl.BlockSpec(memory_space=pl.ANY)],
        out_specs=pl.BlockSpec(memory_space=pl.ANY),
        out_shape=jax.ShapeDtypeStruct(xb0.shape, F32),
        scratch_shapes=[pltpu.VMEM((STG_ROWS, D), F32), pltpu.SemaphoreType.DMA],
        input_output_aliases={3: 0},
        compiler_params=_params("arbitrary"),
        name="dispatch",
    )(info, h, mi, xb0)


def _gmm_kernel(be_ref, nu_ref, x_ref, wgu_ref, bgu_ref, wdn_ref, bdn_ref, o_ref, wgu_sc, wdn_sc):
    i = pl.program_id(0)
    e = be_ref[i]
    prev = be_ref[jnp.maximum(i - 1, 0)]

    @pl.when((i == 0) | (e != prev))
    def _():
        wgu_sc[...] = wgu_ref[0, 0].astype(BF16)
        wdn_sc[...] = wdn_ref[0, 0].astype(BF16)

    @pl.when(i < nu_ref[0])
    def _():
        xb = x_ref[...].astype(BF16)
        gu = _dot(xb, wgu_sc[...]) + bgu_ref[0, 0]
        glu = jnp.minimum(gu[:, 0:D_EXPERT], SWIGLU_LIMIT)
        lin = jnp.clip(gu[:, D_EXPERT:2 * D_EXPERT], -SWIGLU_LIMIT, SWIGLU_LIMIT)
        act = glu * _sigmoid(SWIGLU_ALPHA * glu) * (lin + 1.0)
        o_ref[...] = _dot(act.astype(BF16), wdn_sc[...]) + bdn_ref[0, 0]

    @pl.when(i >= nu_ref[0])
    def _():
        o_ref[...] = jnp.zeros(o_ref.shape, F32)


def _gmm(block_e, n_used, xb, wgu, bgu, wdn, bdn, l):
    P, D = xb.shape
    L, E, _, F2 = wgu.shape
    return pl.pallas_call(
        _gmm_kernel,
        grid_spec=pltpu.PrefetchScalarGridSpec(
            num_scalar_prefetch=2,
            grid=(P // MOE_BLK,),
            in_specs=[pl.BlockSpec((MOE_BLK, D), lambda i, be, nu: (i, 0)),
                      pl.BlockSpec((1, 1, D, F2), lambda i, be, nu: (l, be[i], 0, 0)),
                      pl.BlockSpec((1, 1, 1, F2), lambda i, be, nu: (l, be[i], 0, 0)),
                      pl.BlockSpec((1, 1, F2 // 2, D), lambda i, be, nu: (l, be[i], 0, 0)),
                      pl.BlockSpec((1, 1, 1, D), lambda i, be, nu: (l, be[i], 0, 0))],
            out_specs=pl.BlockSpec((MOE_BLK, D), lambda i, be, nu: (i, 0)),
            scratch_shapes=[pltpu.VMEM((D, F2), BF16), pltpu.VMEM((F2 // 2, D), BF16)]),
        out_shape=jax.ShapeDtypeStruct((P, D), F32),
        compiler_params=_params("arbitrary"),
        name="gmm",
    )(block_e, n_used, xb, wgu, bgu.reshape(L, E, 1, F2), wdn, bdn.reshape(L, E, 1, D))


def _combine_kernel(info_ref, x_ref, mi_ref, gate_ref, g2_ref, fw_ref, yb_ref, o_ref, stg_sc, sem, *, final):
    tm = x_ref.shape[0]

    @pl.when(pl.program_id(0) == 0)
    def _():
        stg_sc[...] = jnp.zeros(stg_sc.shape, F32)

    _segment_copies(info_ref, lambda s, d: pltpu.make_async_copy(
        yb_ref.at[pl.ds(d, SEG_ALIGN)], stg_sc.at[pl.ds(s, SEG_ALIGN)], sem))

    spos = mi_ref[...].astype(F32)
    gate = gate_ref[...]
    r = lax.broadcasted_iota(I32, (tm, stg_sc.shape[0]), 1).astype(F32)
    g = jnp.where(r == spos[:, TOP_K:TOP_K + 1], gate[:, 0:1], 0.0)
    for kk in range(1, TOP_K):
        g = g + jnp.where(r == spos[:, TOP_K + kk:TOP_K + kk + 1], gate[:, kk:kk + 1], 0.0)
    acc = _dot_hi(g, stg_sc[...])
    out = x_ref[...] + g2_ref[0] * acc
    if final:
        ms = jnp.mean(out * out, axis=-1, keepdims=True)
        out = out * lax.rsqrt(ms + EPS) * fw_ref[...]
    o_ref[...] = out


def _combine(info, x, mi, gates, mod, fw, yb, lay, final):
    T, D = x.shape
    B, S = lay["B"], lay["S"]
    mrow = lambda i: jnp.where(i < (B * S) // TMR, i // (S // TMR), B)
    return pl.pallas_call(
        functools.partial(_combine_kernel, final=final),
        grid=(T // TMR,),
        in_specs=[pl.BlockSpec((LANES,), lambda i: (i,), memory_space=pltpu.SMEM),
                  pl.BlockSpec((TMR, D), lambda i: (i, 0)),
                  pl.BlockSpec((TMR, LANES), lambda i: (i, 0)),
                  pl.BlockSpec((TMR, LANES), lambda i: (i, 0)),
                  pl.BlockSpec((1, 1, D), lambda i: (mrow(i), 0, 5)),
                  pl.BlockSpec((1, D), lambda i: (0, 0)),
                  pl.BlockSpec(memory_space=pl.ANY)],
        out_specs=pl.BlockSpec((TMR, D), lambda i: (i, 0)),
        out_shape=jax.ShapeDtypeStruct((T, D), F32),
        scratch_shapes=[pltpu.VMEM((STG_ROWS, D), F32), pltpu.SemaphoreType.DMA],
        compiler_params=_params("arbitrary"),
        name="combine",
    )(info, x, mi, gates, mod, fw, yb)


def _moe(x, nw, mod, wr, br, wgu, bgu, wdn, bdn, fw, lay, l, final):
    T, D = x.shape
    ntb = T // TMR
    h, mi, mf, blk, tot = _route(x, nw, mod, wr, br, lay)
    counts = tot[0, :N_EXPERTS].astype(I32)
    padded = (counts + MOE_BLK - 1) // MOE_BLK * MOE_BLK
    pad_ends = jnp.cumsum(padded)
    pad_starts = pad_ends - padded
    info = jnp.concatenate([blk[:, 0, :N_EXPERTS], blk[:, 1, :N_EXPERTS],
                            blk[:, 2, :N_EXPERTS] + pad_starts[None, :],
                            jnp.zeros((ntb, LANES - 3 * N_EXPERTS), I32)], axis=1).reshape(-1)
    nblk = -(-(T * TOP_K + ntb * N_EXPERTS * (SEG_ALIGN - 1)) // MOE_BLK) + N_EXPERTS
    starts = jnp.arange(nblk, dtype=I32) * MOE_BLK
    block_e = jnp.minimum(jnp.sum((pad_ends[None, :] <= starts[:, None]).astype(I32), axis=1), N_EXPERTS - 1)
    n_used = (pad_ends[N_EXPERTS - 1:] // MOE_BLK).astype(I32)
    xb = _dispatch(info, h, mi, jnp.zeros((nblk * MOE_BLK, D), F32))
    yb = _gmm(block_e, n_used, xb, wgu, bgu, wdn, bdn, l)
    return _combine(info, x, mi, mf, mod, fw, yb, lay, final)


def _rope_tables(S):
    t = np.arange(S)
    row = (t // GRID_W).astype(np.float64)
    col = (t % GRID_W).astype(np.float64)
    lane = np.arange(LANES)
    j = lane % 16
    inv = ROPE_THETA ** (-(j.astype(np.float32)) / np.float32(16.0))
    pos = np.where((lane % 64) < 32, row[:, None], col[:, None]).astype(np.float32)
    ang = pos * inv.astype(np.float32)[None, :]
    sign = np.where((lane % 32) < 16, -1.0, 1.0).astype(np.float32)
    return ang.astype(np.float32), sign


def kernel(x, c, ctx, c_ctx, w_ada, b_ada, norm1_w, w_in, da_lambda, da_subln_w, ssd_conv_w, ssd_conv_b,
           ssd_a_log, ssd_dt_bias, ssd_d, ssd_norm_w, gla_gk_up, gla_gk_b, gla_norm_w, w_out, norm2_w,
           w_router, b_router, w_gate_up, b_gate_up, w_down, b_down, final_norm_w):
    B, S, D = x.shape
    Lc = ctx.shape[1]
    depth = w_ada.shape[0]
    assert S % TMR == 0 and Lc % TM == 0 and (B * Lc) % TMR == 0 and S % GRID_W == 0
    lay = dict(B=B, S=S, Lc=Lc, nlb=(B * S) // TM, bpb=S // TM, cpb=Lc // TM)
    n_lat = B * S

    xs = jnp.concatenate([x.reshape(B * S, D), ctx.reshape(B * Lc, D)], axis=0)

    cc = jnp.zeros((8, D), F32).at[0:B].set(c).at[B].set(c_ctx)
    mod_all = _ada(cc, w_ada, b_ada)

    ang, sign = _rope_tables(S)
    ang = jnp.asarray(ang)
    cos_t = jnp.concatenate([jnp.cos(ang), jnp.ones((TM, LANES), F32)], axis=0)
    sin_t = jnp.concatenate([jnp.sin(ang) * jnp.asarray(sign)[None, :], jnp.zeros((TM, LANES), F32)], axis=0)

    for l in range(depth):
        last = l == depth - 1
        lam_init = 0.8 - 0.6 * math.exp(-0.3 * l)
        mod = mod_all[l, 0:B + 1].reshape(B + 1, 1, 6 * D)

        wi = w_in[l]
        misc_w = jnp.zeros((D, LANES), F32)
        misc_w = misc_w.at[:, MISC_DT:MISC_DT + 8].set(wi[:, 2304:2312])
        misc_w = misc_w.at[:, MISC_CODE:MISC_CODE + 2 * GLA_RANK].set(wi[:, 3080:3112])
        w_re = jnp.concatenate([wi[:, 0:2304], wi[:, 2312:3080], misc_w], axis=1).astype(BF16)
        a128 = jnp.zeros((1, LANES), F32).at[0, 0:8].set(-jnp.exp(ssd_a_log[l].astype(F32)).reshape(-1))
        bias128 = jnp.zeros((1, LANES), F32).at[0, 0:8].set(ssd_dt_bias[l].astype(F32).reshape(-1))
        wgk = jnp.zeros((2, LANES, GLA_KEY_WIDTH), F32)
        for d in range(2):
            wgk = wgk.at[d, MISC_CODE + d * GLA_RANK:MISC_CODE + (d + 1) * GLA_RANK, :].set(gla_gk_up[l, d])
        bgk = gla_gk_b[l].reshape(2, 1, GLA_KEY_WIDTH)
        dsk = jnp.repeat(ssd_d[l], SSD_HEAD_DIM).reshape(1, SSD_WIDTH)
        snw = ssd_norm_w[l].reshape(1, SSD_WIDTH)
        gnw = jnp.tile(gla_norm_w[l], GLA_HEADS).reshape(1, GLA_WIDTH)
        wr = jnp.zeros((D, LANES), F32).at[:, 0:N_EXPERTS].set(w_router[l])
        br = jnp.zeros((1, LANES), F32).at[0, 0:N_EXPERTS].set(b_router[l])

        q, k, v, z, xbc, gla_in, misc = _inproj(xs, norm1_w[l].reshape(1, D), mod, cos_t, sin_t, w_re, lay)

        a = _attention(q, k, v, da_lambda[l], da_subln_w[l].reshape(1, DA_V), lay, lam_init,
                       ctx_queries=not last)
        xbc_act = _conv(xbc, ssd_conv_w[l], ssd_conv_b[l].reshape(1, SSD_CONV_CH), lay)
        yf, yb = _ssd(xbc_act, misc, a128, bias128, lay)
        of, ob = _gla(gla_in, misc, wgk, bgk, lay)

        nrows = n_lat if last else xs.shape[0]
        xs = _outproj(xs, a, yf, yb, xbc_act, z, of, ob, gla_in, dsk, snw, gnw,
                      w_out[l].astype(BF16), mod, lay, nrows)
        xs = _moe(xs, norm2_w[l].reshape(1, D), mod, wr, br, w_gate_up, b_gate_up, w_down, b_down,
                  final_norm_w.reshape(1, D), lay, l, last)

    return xs.reshape(B, S, D)
```

```python
import functools
import math

import numpy as np
import jax
import jax.numpy as jnp
from jax import lax
from jax.experimental import pallas as pl
from jax.experimental.pallas import tpu as pltpu

F32 = jnp.float32
BF16 = jnp.bfloat16
I32 = jnp.int32

GRID_W = 64
EPS = 1e-6
DA_HEADS = 4
DA_QK = 64
DA_V = 128
DA_WIDTH = 512
ROPE_THETA = 10000.0
SSD_HEADS = 4
SSD_HEAD_DIM = 64
SSD_WIDTH = 256
SSD_STATE = 64
SSD_CONV_CH = 512
GLA_HEADS = 4
GLA_DK = 32
GLA_DV = 64
GLA_KEY_WIDTH = 128
GLA_WIDTH = 256
GLA_RANK = 16
GLA_NORMALIZER = 16.0
N_EXPERTS = 32
TOP_K = 4
D_EXPERT = 1024
SWIGLU_LIMIT = 7.0
SWIGLU_ALPHA = 1.702

LANES = 128
TM = 256
SSD_Q = 128
GLA_Q = 64
GLA_SUB = 4
ATT_TQ = 256
ATT_KT = (2816, 768, 512, 256)
LOG2E = 1.4426950408889634
MOE_BLK = 256
TMR = 512
SEG_ALIGN = 8
STG_ROWS = TMR * TOP_K + N_EXPERTS * SEG_ALIGN
INFO_LANES = 256
INFO_TAIL = 128
INFO_NUSED = 224
VMEM_LIMIT = 56 * 1024 * 1024

C_Q, C_K, C_V, C_Z, C_XBC, C_GLA, C_MISC, C_END = 0, 512, 1024, 1536, 1792, 2304, 3072, 3200
MISC_DT = 0
MISC_CODE = 8


def _sigmoid(x):
    return 1.0 / (1.0 + jnp.exp(-x))


def _softplus(x):
    return jnp.maximum(x, 0.0) + jnp.log(1.0 + jnp.exp(-jnp.abs(x)))


def _split2(a):
    hi = a.astype(BF16)
    lo = (a - hi.astype(F32)).astype(BF16)
    return hi, lo


def _split3(a):
    a1 = a.astype(BF16)
    r1 = a - a1.astype(F32)
    a2 = r1.astype(BF16)
    a3 = (r1 - a2.astype(F32)).astype(BF16)
    return a1, a2, a3


def _dot(a, b):
    return jnp.dot(a, b, preferred_element_type=F32)


def _dot_hi(a, b):
    a1, a2 = _split2(a)
    b1, b2 = _split2(b)
    return _dot(a1, b1) + (_dot(a1, b2) + _dot(a2, b1))


def _dot_exact_lhs(m, a):
    a1, a2, a3 = _split3(a)
    mb = m.astype(BF16)
    return _dot(mb, a1) + (_dot(mb, a2) + _dot(mb, a3))


def _dot_exact_rhs(a, m):
    a1, a2, a3 = _split3(a)
    mb = m.astype(BF16)
    return _dot(a1, mb) + (_dot(a2, mb) + _dot(a3, mb))


def _params(*sem):
    return pltpu.CompilerParams(dimension_semantics=sem, vmem_limit_bytes=VMEM_LIMIT)


def _ada_kernel(c_ref, w_ref, b_ref, o_ref):
    c = c_ref[...]
    s = c * _sigmoid(c)
    o_ref[0] = _dot_hi(s, w_ref[0]) + b_ref[0]


def _ada(cc, w_ada, b_ada):
    L, D, N = w_ada.shape
    tn = 512
    return pl.pallas_call(
        _ada_kernel,
        grid=(L, N // tn),
        in_specs=[pl.BlockSpec((8, D), lambda l, j: (0, 0)),
                  pl.BlockSpec((1, D, tn), lambda l, j: (l, 0, j)),
                  pl.BlockSpec((1, 1, tn), lambda l, j: (l, 0, j))],
        out_specs=pl.BlockSpec((1, 8, tn), lambda l, j: (l, 0, j)),
        out_shape=jax.ShapeDtypeStruct((L, 8, N), F32),
        compiler_params=_params("parallel", "parallel"),
        name="ada",
    )(cc, w_ada, b_ada.reshape(L, 1, N))


def _inproj_kernel(x_ref, nw_ref, sh_ref, sc_ref, cos_ref, sin_ref, w_ref,
                   q_ref, k_ref, v_ref, z_ref, xbc_ref, gla_ref, misc_ref):
    xf = x_ref[...]
    ms = jnp.mean(xf * xf, axis=-1, keepdims=True)
    h = (xf * lax.rsqrt(ms + EPS) * nw_ref[...]) * (1.0 + sc_ref[0]) + sh_ref[0]
    hb = h.astype(BF16)
    cos = cos_ref[...]
    sin = sin_ref[...]
    lane = lax.broadcasted_iota(I32, cos.shape, 1)
    first = (lane % 32) < 16

    def rope(p):
        outs = []
        for hd in range(DA_HEADS):
            ph = p[:, hd * LANES:(hd + 1) * LANES]
            sw = jnp.where(first, pltpu.roll(ph, LANES - 16, 1), pltpu.roll(ph, 16, 1))
            outs.append(ph * cos + sw * sin)
        return jnp.concatenate(outs, axis=1)

    q = rope(_dot(hb, w_ref[:, C_Q:C_K])) * (DA_QK ** -0.5 * LOG2E)
    q_ref[...] = q.astype(BF16)
    k_ref[...] = rope(_dot(hb, w_ref[:, C_K:C_V])).astype(BF16)
    v_ref[...] = _dot(hb, w_ref[:, C_V:C_Z]).astype(BF16)
    z_ref[...] = _dot(hb, w_ref[:, C_Z:C_XBC])
    xbc_ref[...] = _dot(hb, w_ref[:, C_XBC:C_GLA])
    gla_ref[...] = _dot(hb, w_ref[:, C_GLA:C_MISC])
    misc_ref[...] = _dot(hb, w_ref[:, C_MISC:C_END])


def _inproj(x, nw, mod, cos_t, sin_t, w, lay):
    T, D = x.shape
    nlb, bpb, B = lay["nlb"], lay["bpb"], lay["B"]
    mrow = lambda i: jnp.where(i < nlb, i // bpb, B)
    prow = lambda i: jnp.where(i < nlb, i % bpb, bpb)
    row = lambda n: pl.BlockSpec((TM, n), lambda i: (i, 0))
    spb = bpb + lay["cpb"]
    kvrow = lambda i: (jnp.where(i < nlb, (i // bpb) * spb + lay["cpb"] + i % bpb,
                                 ((i - nlb) // lay["cpb"]) * spb + (i - nlb) % lay["cpb"]), 0)
    kv = pl.BlockSpec((TM, DA_WIDTH), kvrow)
    outs = [(512, BF16), (512, BF16), (512, BF16), (256, F32), (512, F32), (768, F32), (128, F32)]
    out_specs = [row(n) for n, _ in outs]
    out_specs[1] = kv
    out_specs[2] = kv
    return pl.pallas_call(
        _inproj_kernel,
        grid=(T // TM,),
        in_specs=[row(D),
                  pl.BlockSpec((1, D), lambda i: (0, 0)),
                  pl.BlockSpec((1, 1, D), lambda i: (mrow(i), 0, 0)),
                  pl.BlockSpec((1, 1, D), lambda i: (mrow(i), 0, 1)),
                  pl.BlockSpec((TM, LANES), lambda i: (prow(i), 0)),
                  pl.BlockSpec((TM, LANES), lambda i: (prow(i), 0)),
                  pl.BlockSpec((D, C_END), lambda i: (0, 0))],
        out_specs=out_specs,
        out_shape=[jax.ShapeDtypeStruct((T, n), dt) for n, dt in outs],
        compiler_params=_params("parallel"),
        name="inproj",
    )(x, nw, mod, mod, cos_t, sin_t, w)


def _lane_fold(x, op):
    f = x[:, 0:LANES]
    for i in range(1, x.shape[1] // LANES):
        f = op(f, x[:, i * LANES:(i + 1) * LANES])
    return f


def _attn_two_pass(q2_sc, k_ref, v_ref, s_sc, p_sc, m_sc, l_sc, acc_sc, nt, kt):
    m_sc[...] = jnp.full(m_sc.shape, -jnp.inf, F32)
    l_sc[...] = jnp.zeros(l_sc.shape, F32)
    acc_sc[...] = jnp.zeros(acc_sc.shape, F32)

    def scores(j, carry):
        kj = k_ref[pl.ds(pl.multiple_of(j * kt, kt), kt), :]
        s = lax.dot_general(q2_sc[...], kj, (((1,), (1,)), ((), ())), preferred_element_type=F32)
        s_sc[j, :, 0:kt] = s
        m_sc[...] = jnp.maximum(m_sc[...], _lane_fold(s, jnp.maximum))
        return carry

    lax.fori_loop(0, nt, scores, 0)
    m = jnp.max(m_sc[...], axis=1, keepdims=True)

    sub = next(w for w in (512, 256) if kt % w == 0)

    def weigh(j, carry):
        for c0 in range(0, kt, sub):
            p = jnp.exp2(s_sc[j, :, c0:c0 + sub] - m)
            l_sc[...] += _lane_fold(p, jnp.add)
            p_sc[:, c0:c0 + sub] = p.astype(BF16)
        vj = v_ref[pl.ds(pl.multiple_of(j * kt, kt), kt), :]
        acc_sc[...] += _dot(p_sc[:, 0:kt], vj)
        return carry

    lax.fori_loop(0, nt, weigh, 0)


def _attn_kernel(q_ref, k_ref, v_ref, lam_ref, w_ref, o_ref, q2_sc, s_sc, p_sc, m_sc, l_sc, acc_sc,
                 *, nq, nqc, nt, kt, ntc, ktc, lam_init):
    qi = pl.program_id(2)
    tq = q_ref.shape[0]
    q = q_ref[...]
    lane = lax.broadcasted_iota(I32, q.shape, 1)
    zero = jnp.zeros_like(q)
    q2_sc[0:tq, :] = jnp.where(lane < DA_QK, q, zero)
    q2_sc[tq:2 * tq, :] = jnp.where(lane >= DA_QK, q, zero)

    if nqc == 0:
        _attn_two_pass(q2_sc, k_ref, v_ref, s_sc, p_sc, m_sc, l_sc, acc_sc, nt, kt)
    else:
        @pl.when(qi < nq)
        def _():
            _attn_two_pass(q2_sc, k_ref, v_ref, s_sc, p_sc, m_sc, l_sc, acc_sc, nt, kt)

        @pl.when(qi >= nq)
        def _():
            _attn_two_pass(q2_sc, k_ref, v_ref, s_sc, p_sc, m_sc, l_sc, acc_sc, ntc, ktc)

    lp = lam_ref[...]
    lam = (jnp.exp(jnp.sum(lp[0:1] * lp[1:2], axis=1, keepdims=True))
           - jnp.exp(jnp.sum(lp[2:3] * lp[3:4], axis=1, keepdims=True)) + lam_init)
    acc = acc_sc[...]
    l = jnp.sum(l_sc[...], axis=1, keepdims=True)
    o = acc[0:tq] / l[0:tq] - lam * (acc[tq:2 * tq] / l[tq:2 * tq])
    ms = jnp.mean(o * o, axis=-1, keepdims=True)
    o_ref[...] = (o * lax.rsqrt(ms + EPS) * w_ref[...]) * (1.0 - lam_init)


def _attention(q, k, v, lam_p, subln_w, lay, lam_init, *, ctx_queries):
    B, S, Lc = lay["B"], lay["S"], lay["Lc"]
    nkeys = Lc + S
    kt = next(t for t in ATT_KT if nkeys % t == 0)
    ktc = next(t for t in ATT_KT if Lc % t == 0)
    lat_qblk = (B * S) // ATT_TQ
    nq = S // ATT_TQ
    nqc = Lc // ATT_TQ if ctx_queries else 0
    qmap = lambda b, h, qi: (jnp.where(qi < nq, b * nq + qi, lat_qblk + b * nqc + qi - nq), h)
    kmap = lambda b, h, qi: (b, h)
    rows = B * S + (B * Lc if ctx_queries else 0)
    return pl.pallas_call(
        functools.partial(_attn_kernel, nq=nq, nqc=nqc, nt=nkeys // kt, kt=kt, ntc=Lc // ktc, ktc=ktc,
                          lam_init=lam_init),
        grid=(B, DA_HEADS, nq + nqc),
        in_specs=[pl.BlockSpec((ATT_TQ, LANES), qmap),
                  pl.BlockSpec((nkeys, LANES), kmap),
                  pl.BlockSpec((nkeys, LANES), kmap),
                  pl.BlockSpec((4, DA_QK), lambda b, h, qi: (0, 0)),
                  pl.BlockSpec((1, DA_V), lambda b, h, qi: (0, 0))],
        out_specs=pl.BlockSpec((ATT_TQ, LANES), qmap),
        out_shape=jax.ShapeDtypeStruct((rows, DA_WIDTH), F32),
        scratch_shapes=[pltpu.VMEM((2 * ATT_TQ, LANES), BF16),
                        pltpu.VMEM((nkeys // kt, 2 * ATT_TQ, kt), F32),
                        pltpu.VMEM((2 * ATT_TQ, kt), BF16),
                        pltpu.VMEM((2 * ATT_TQ, LANES), F32),
                        pltpu.VMEM((2 * ATT_TQ, LANES), F32),
                        pltpu.VMEM((2 * ATT_TQ, LANES), F32)],
        compiler_params=_params("parallel", "parallel", "arbitrary"),
        name="attn",
    )(q, k, v, lam_p, subln_w)


def _conv_kernel(flag_ref, x_ref, p_ref, n_ref, w_ref, b_ref, o_ref):
    i = pl.program_id(0)
    x = x_ref[...]
    keep_prev = (flag_ref[i, 0] == 0).astype(F32)
    keep_next = (flag_ref[i, 1] == 0).astype(F32)
    prev_row = p_ref[7:8, :] * keep_prev
    next_row = n_ref[0:1, :] * keep_next
    ridx = lax.broadcasted_iota(I32, x.shape, 0)
    xm1 = jnp.where(ridx == 0, prev_row, pltpu.roll(x, 1, 0))
    xp1 = jnp.where(ridx == x.shape[0] - 1, next_row, pltpu.roll(x, x.shape[0] - 1, 0))
    w = w_ref[...]
    y = w[0:1] * xm1 + w[1:2] * x + w[2:3] * xp1 + b_ref[...]
    o_ref[...] = y * _sigmoid(y)


def _conv(xbc, cw, cb, lay):
    T, C = xbc.shape
    nb = T // TM
    r8 = TM // 8
    flags = np.zeros((nb, 2), np.int32)
    for i in range(nb):
        if i < lay["nlb"]:
            flags[i, 0] = (i % lay["bpb"]) == 0
            flags[i, 1] = (i % lay["bpb"]) == lay["bpb"] - 1
        else:
            j = (i - lay["nlb"]) % lay["cpb"]
            flags[i, 0] = j == 0
            flags[i, 1] = j == lay["cpb"] - 1
    return pl.pallas_call(
        _conv_kernel,
        grid_spec=pltpu.PrefetchScalarGridSpec(
            num_scalar_prefetch=1,
            grid=(nb,),
            in_specs=[pl.BlockSpec((TM, C), lambda i, f: (i, 0)),
                      pl.BlockSpec((8, C), lambda i, f: (jnp.maximum(i * r8 - 1, 0), 0)),
                      pl.BlockSpec((8, C), lambda i, f: (jnp.minimum((i + 1) * r8, T // 8 - 1), 0)),
                      pl.BlockSpec((3, C), lambda i, f: (0, 0)),
                      pl.BlockSpec((1, C), lambda i, f: (0, 0))],
            out_specs=pl.BlockSpec((TM, C), lambda i, f: (i, 0))),
        out_shape=jax.ShapeDtypeStruct((T, C), F32),
        compiler_params=_params("parallel"),
        name="ssd_conv",
    )(jnp.asarray(flags), xbc, xbc, xbc, cw, cb)


def _ssd_dir(xbc, misc, a128, bias128, hs_ref, dirn, rev):
    Q = xbc.shape[0]
    dt = _softplus(misc + bias128)
    a = dt * a128
    ii = lax.broadcasted_iota(I32, (Q, Q), 0)
    jj = lax.broadcasted_iota(I32, (Q, Q), 1)
    M = (jj >= ii) if rev else (jj <= ii)
    acs = _dot_exact_lhs(M.astype(F32), a)
    acs_t = acs.T
    dt_t = dt.T
    xs = xbc[:, 0:SSD_WIDTH]
    Bt = xbc[:, SSD_WIDTH:SSD_WIDTH + 128].T
    Cm = xbc[:, SSD_WIDTH + 128:SSD_WIDTH + 256]
    last = 0 if rev else Q - 1
    ys = []
    for g in range(2):
        Cg = Cm[:, g * SSD_STATE:(g + 1) * SSD_STATE]
        Btg = Bt[g * SSD_STATE:(g + 1) * SSD_STATE, :]
        CB = _dot(Cg, Btg)
        for hh in range(2):
            h = 2 * g + hh
            li = dirn * SSD_HEADS + h
            acs_c = acs[:, li:li + 1]
            acs_r = acs_t[li:li + 1, :]
            dt_r = dt_t[li:li + 1, :]
            decay = jnp.exp(jnp.where(M, acs_c - acs_r, -jnp.inf))
            sc = CB * decay * dt_r
            x_h = xs[:, h * SSD_HEAD_DIM:(h + 1) * SSD_HEAD_DIM]
            hs = hs_ref[h]
            y = _dot(sc, x_h) + _dot(Cg * jnp.exp(acs_c), hs)
            tot = acs_r[:, last:last + 1]
            w_end = jnp.exp(tot - acs_r) * dt_r
            hs_ref[h] = jnp.exp(tot) * hs + _dot(Btg * w_end, x_h)
            ys.append(y)
    return jnp.concatenate(ys, axis=1)


def _ssd_kernel(xf_ref, mf_ref, xb_ref, mb_ref, a_ref, bias_ref, yf_ref, yb_ref, hf_sc, hb_sc):
    @pl.when(pl.program_id(1) == 0)
    def _():
        hf_sc[...] = jnp.zeros(hf_sc.shape, F32)
        hb_sc[...] = jnp.zeros(hb_sc.shape, F32)

    a128 = a_ref[...]
    bias128 = bias_ref[...]
    yf_ref[...] = _ssd_dir(xf_ref[...], mf_ref[...], a128, bias128, hf_sc, 0, False)
    yb_ref[...] = _ssd_dir(xb_ref[...], mb_ref[...], a128, bias128, hb_sc, 1, True)


def _scan_maps(lay, rows):
    ncc, ncl = lay["Lc"] // rows, lay["S"] // rows
    base = (lay["B"] * lay["S"]) // rows
    fmap = lambda b, c: (jnp.where(c < ncc, base + b * ncc + c, b * ncl + c - ncc), 0)
    bmap = lambda b, c: (jnp.where(c < ncc, base + b * ncc + (ncc - 1 - c),
                                   b * ncl + (ncl - 1 - (c - ncc))), 0)
    return ncc + ncl, fmap, bmap


def _ssd(xbc_act, misc, a128, bias128, lay):
    T = xbc_act.shape[0]
    nc, fmap, bmap = _scan_maps(lay, SSD_Q)
    vec = pl.BlockSpec((1, LANES), lambda b, c: (0, 0))
    return pl.pallas_call(
        _ssd_kernel,
        grid=(lay["B"], nc),
        in_specs=[pl.BlockSpec((SSD_Q, SSD_CONV_CH), fmap), pl.BlockSpec((SSD_Q, LANES), fmap),
                  pl.BlockSpec((SSD_Q, SSD_CONV_CH), bmap), pl.BlockSpec((SSD_Q, LANES), bmap),
                  vec, vec],
        out_specs=[pl.BlockSpec((SSD_Q, SSD_WIDTH), fmap), pl.BlockSpec((SSD_Q, SSD_WIDTH), bmap)],
        out_shape=[jax.ShapeDtypeStruct((T, SSD_WIDTH), F32)] * 2,
        scratch_shapes=[pltpu.VMEM((SSD_HEADS, SSD_STATE, SSD_HEAD_DIM), F32)] * 2,
        compiler_params=_params("parallel", "arbitrary"),
        name="ssd",
    )(xbc_act, misc, xbc_act, misc, a128, bias128)


def _gla_block(blk, misc, wgk, bgk, s_ref, rev):
    R = blk.shape[0]
    kw = GLA_KEY_WIDTH
    q, k, v = blk[:, 0:kw], blk[:, kw:2 * kw], blk[:, 2 * kw:2 * kw + GLA_WIDTH]
    pre = _dot_hi(misc, wgk) + bgk
    gk = (jnp.minimum(pre, 0.0) - jnp.log(1.0 + jnp.exp(-jnp.abs(pre)))) / GLA_NORMALIZER
    ii = lax.broadcasted_iota(I32, (R, R), 0)
    jj = lax.broadcasted_iota(I32, (R, R), 1)
    same = (ii // GLA_Q) == (jj // GLA_Q)
    M = same & ((jj >= ii) if rev else (jj <= ii))
    b = _dot_exact_lhs(M.astype(F32), gk)
    tot = _dot_exact_lhs(same.astype(F32), gk)
    q_t = (q * (GLA_DK ** -0.5)) * jnp.exp(b)
    k_t = k * jnp.exp(-b)
    k_end_t = (k * jnp.exp(tot - b)).T
    gk_t = gk.T
    lane_k = lax.broadcasted_iota(I32, (1, kw), 1) // GLA_DK
    lane_v = lax.broadcasted_iota(I32, (1, GLA_WIDTH), 1) // GLA_DV
    o = jnp.zeros((R, GLA_WIDTH), F32)
    for h in range(GLA_HEADS):
        qh = jnp.where(lane_k == h, q_t, 0.0)
        att = lax.dot_general(qh, k_t, (((1,), (1,)), ((), ())), preferred_element_type=F32)
        o = o + _dot(jnp.where(M, att, 0.0), jnp.where(lane_v == h, v, 0.0))
    col_chunk = lax.broadcasted_iota(I32, (1, R), 1) // GLA_Q
    S = s_ref[...]
    bd = (lax.broadcasted_iota(I32, S.shape, 0) // GLA_DK) == (lax.broadcasted_iota(I32, S.shape, 1) // GLA_DV)
    nsub = R // GLA_Q
    inter = [None] * nsub
    for c in (range(nsub - 1, -1, -1) if rev else range(nsub)):
        sel = col_chunk == c
        inter[c] = _dot(q_t[c * GLA_Q:(c + 1) * GLA_Q], S)
        decay = jnp.exp(jnp.sum(jnp.where(sel, gk_t, 0.0), axis=1, keepdims=True))
        S = jnp.where(bd, S * decay + _dot(jnp.where(sel, k_end_t, 0.0), v), 0.0)
    s_ref[...] = S
    return o + jnp.concatenate(inter, axis=0)


def _gla_kernel(gf_ref, mf_ref, gb_ref, mb_ref, wgk_ref, bgk_ref, of_ref, ob_ref, sf_sc, sb_sc):
    @pl.when(pl.program_id(1) == 0)
    def _():
        sf_sc[...] = jnp.zeros(sf_sc.shape, F32)
        sb_sc[...] = jnp.zeros(sb_sc.shape, F32)

    of_ref[...] = _gla_block(gf_ref[...], mf_ref[...], wgk_ref[0], bgk_ref[0], sf_sc, False)
    ob_ref[...] = _gla_block(gb_ref[...], mb_ref[...], wgk_ref[1], bgk_ref[1], sb_sc, True)


def _gla(gla_in, misc, wgk, bgk, lay):
    T = gla_in.shape[0]
    R = GLA_Q * GLA_SUB
    nc, fmap, bmap = _scan_maps(lay, R)
    return pl.pallas_call(
        _gla_kernel,
        grid=(lay["B"], nc),
        in_specs=[pl.BlockSpec((R, 512), fmap), pl.BlockSpec((R, LANES), fmap),
                  pl.BlockSpec((R, 512), bmap), pl.BlockSpec((R, LANES), bmap),
                  pl.BlockSpec((2, LANES, GLA_KEY_WIDTH), lambda b, c: (0, 0, 0)),
                  pl.BlockSpec((2, 1, GLA_KEY_WIDTH), lambda b, c: (0, 0, 0))],
        out_specs=[pl.BlockSpec((R, GLA_WIDTH), fmap), pl.BlockSpec((R, GLA_WIDTH), bmap)],
        out_shape=[jax.ShapeDtypeStruct((T, GLA_WIDTH), F32)] * 2,
        scratch_shapes=[pltpu.VMEM((GLA_KEY_WIDTH, GLA_WIDTH), F32)] * 2,
        compiler_params=_params("parallel", "arbitrary"),
        name="gla",
    )(gla_in, misc, gla_in, misc, wgk, bgk)


def _outproj_kernel(x_ref, a_ref, yf_ref, yb_ref, xs_ref, z_ref, of_ref, ob_ref, g_ref,
                    dsk_ref, snw_ref, gnw_ref, w_ref, g1_ref, o_ref):
    y = yf_ref[...] + yb_ref[...] + dsk_ref[...] * xs_ref[...]
    z = z_ref[...]
    gs = y * (z * _sigmoid(z))
    half = SSD_WIDTH // 2
    parts = []
    for grp in range(2):
        seg = gs[:, grp * half:(grp + 1) * half]
        parts.append(seg * lax.rsqrt(jnp.mean(seg * seg, axis=-1, keepdims=True) + EPS))
    s = jnp.concatenate(parts, axis=1) * snw_ref[...]
    o = of_ref[...] + ob_ref[...]
    bd = ((lax.broadcasted_iota(I32, (GLA_WIDTH, GLA_WIDTH), 0) // GLA_DV)
          == (lax.broadcasted_iota(I32, (GLA_WIDTH, GLA_WIDTH), 1) // GLA_DV))
    ms = _dot_exact_rhs(o * o, jnp.where(bd, 1.0 / GLA_DV, 0.0))
    g = g_ref[...]
    c = (o * lax.rsqrt(ms + EPS) * gnw_ref[...]) * (g * _sigmoid(g))
    mix = jnp.concatenate([a_ref[...], s, c], axis=1).astype(BF16)
    o_ref[...] = x_ref[...] + g1_ref[0] * _dot(mix, w_ref[...])


def _outproj(x, a, yf, yb, xbc_act, z, of, ob, gla_in, dsk, snw, gnw, w_out, mod, lay, nrows):
    D = x.shape[1]
    nlb, bpb, B = lay["nlb"], lay["bpb"], lay["B"]
    mrow = lambda i: jnp.where(i < nlb, i // bpb, B)
    row = lambda n, cb=0: pl.BlockSpec((TM, n), lambda i: (i, cb))
    vec = lambda n: pl.BlockSpec((1, n), lambda i: (0, 0))
    return pl.pallas_call(
        _outproj_kernel,
        grid=(nrows // TM,),
        in_specs=[row(D), row(DA_WIDTH), row(SSD_WIDTH), row(SSD_WIDTH), row(SSD_WIDTH), row(SSD_WIDTH),
                  row(GLA_WIDTH), row(GLA_WIDTH), row(GLA_WIDTH, 2),
                  vec(SSD_WIDTH), vec(SSD_WIDTH), vec(GLA_WIDTH),
                  pl.BlockSpec((D, D), lambda i: (0, 0)),
                  pl.BlockSpec((1, 1, D), lambda i: (mrow(i), 0, 2))],
        out_specs=row(D),
        out_shape=jax.ShapeDtypeStruct((nrows, D), F32),
        compiler_params=_params("parallel"),
        name="outproj",
    )(x, a, yf, yb, xbc_act, z, of, ob, gla_in, dsk, snw, gnw, w_out, mod)


def _route_kernel(x_ref, nw_ref, sh_ref, sc_ref, wr_ref, br_ref,
                  h_ref, mi_ref, mf_ref, blk_ref, tot_ref, carry_sc):
    i = pl.program_id(0)

    @pl.when(i == 0)
    def _():
        carry_sc[...] = jnp.zeros(carry_sc.shape, F32)

    xf = x_ref[...]
    ms = jnp.mean(xf * xf, axis=-1, keepdims=True)
    h = (xf * lax.rsqrt(ms + EPS) * nw_ref[...]) * (1.0 + sc_ref[0]) + sh_ref[0]
    h_ref[...] = h
    tm = h.shape[0]
    lane = lax.broadcasted_iota(I32, (tm, LANES), 1)
    logits = jnp.where(lane < N_EXPERTS, _dot_hi(h, wr_ref[...]) + br_ref[...], -jnp.inf)
    vals, idxs, hots = [], [], []
    l = logits
    for _ in range(TOP_K):
        m = jnp.max(l, axis=1, keepdims=True)
        idx = jnp.min(jnp.where(l == m, lane, LANES), axis=1, keepdims=True)
        hot = lane == idx
        vals.append(m)
        idxs.append(idx)
        hots.append(hot)
        l = jnp.where(hot, -jnp.inf, l)
    es = [jnp.exp(v - vals[0]) for v in vals]
    den = es[0] + es[1] + es[2] + es[3]
    hot_all = jnp.zeros((tm, LANES), F32)
    for hot in hots:
        hot_all = hot_all + hot.astype(F32)
    ii = lax.broadcasted_iota(I32, (tm, tm), 0)
    jj = lax.broadcasted_iota(I32, (tm, tm), 1)
    before = _dot((jj < ii).astype(BF16), hot_all.astype(BF16))
    cnt = jnp.sum(hot_all, axis=0, keepdims=True)
    cnt8 = jnp.floor((cnt + (SEG_ALIGN - 1.0)) * (1.0 / SEG_ALIGN)) * SEG_ALIGN
    ei = lax.broadcasted_iota(I32, (LANES, LANES), 0)
    ej = lax.broadcasted_iota(I32, (LANES, LANES), 1)
    units = jnp.broadcast_to(cnt8 * (1.0 / SEG_ALIGN), (8, LANES)).astype(BF16)
    seg = _dot(units, (ei < ej).astype(BF16))[0:1] * SEG_ALIGN
    pos_e = seg + before
    mi = jnp.zeros((tm, LANES), I32)
    mf = jnp.zeros((tm, LANES), F32)
    for kk in range(TOP_K):
        spos = jnp.sum(jnp.where(hots[kk], pos_e, 0.0), axis=1, keepdims=True).astype(I32)
        mi = jnp.where(lane == kk, idxs[kk], mi)
        mi = jnp.where(lane == TOP_K + kk, spos, mi)
        mf = jnp.where(lane == kk, es[kk] / den, mf)
    mi_ref[...] = mi
    mf_ref[...] = mf
    rowi = lax.broadcasted_iota(I32, (8, LANES), 0)
    info = jnp.where(rowi == 0, cnt8, jnp.where(rowi == 1, seg, jnp.where(rowi == 2, carry_sc[...], 0.0)))
    blk_ref[0] = info.astype(I32)
    carry_sc[...] = carry_sc[...] + cnt8
    tot_ref[...] = carry_sc[...]


def _route(x, nw, mod, wr, br, lay):
    T, D = x.shape
    B, S = lay["B"], lay["S"]
    mrow = lambda i: jnp.where(i < (B * S) // TMR, i // (S // TMR), B)
    row = lambda n: pl.BlockSpec((TMR, n), lambda i: (i, 0))
    return pl.pallas_call(
        _route_kernel,
        grid=(T // TMR,),
        in_specs=[row(D),
                  pl.BlockSpec((1, D), lambda i: (0, 0)),
                  pl.BlockSpec((1, 1, D), lambda i: (mrow(i), 0, 3)),
                  pl.BlockSpec((1, 1, D), lambda i: (mrow(i), 0, 4)),
                  pl.BlockSpec((D, LANES), lambda i: (0, 0)),
                  pl.BlockSpec((1, LANES), lambda i: (0, 0))],
        out_specs=[row(D), row(LANES), row(LANES),
                   pl.BlockSpec((1, 8, LANES), lambda i: (i, 0, 0)),
                   pl.BlockSpec((1, LANES), lambda i: (0, 0))],
        out_shape=[jax.ShapeDtypeStruct((T, D), F32), jax.ShapeDtypeStruct((T, LANES), I32),
                   jax.ShapeDtypeStruct((T, LANES), F32),
                   jax.ShapeDtypeStruct((T // TMR, 8, LANES), I32),
                   jax.ShapeDtypeStruct((1, LANES), F32)],
        scratch_shapes=[pltpu.VMEM((1, LANES), F32)],
        compiler_params=_params("arbitrary"),
        name="route",
    )(x, nw, mod, mod, wr, br)


def _segment_copies(info_ref, base, make_copy):
    def start_expert(e, total):
        n = info_ref[base + e] // SEG_ALIGN
        src0 = info_ref[base + N_EXPERTS + e]
        dst0 = info_ref[base + 2 * N_EXPERTS + e]

        def start_chunk(c, carry):
            make_copy(pl.multiple_of(src0 + c * SEG_ALIGN, SEG_ALIGN),
                      pl.multiple_of(dst0 + c * SEG_ALIGN, SEG_ALIGN)).start()
            return carry

        lax.fori_loop(0, n, start_chunk, 0)
        return total + n

    total = lax.fori_loop(0, N_EXPERTS, start_expert, 0)

    def wait_chunk(c, carry):
        make_copy(0, 0).wait()
        return carry

    lax.fori_loop(0, total, wait_chunk, 0)


def _dispatch_kernel(info_ref, h_ref, mi_ref, xb_ref, stg_sc, sem):
    tm = h_ref.shape[0]

    @pl.when(pl.program_id(0) == 0)
    def _():
        stg_sc[0:MOE_BLK, :] = jnp.zeros((MOE_BLK, stg_sc.shape[1]), F32)
        _segment_copies(info_ref, INFO_TAIL, lambda s, d: pltpu.make_async_copy(
            stg_sc.at[pl.ds(s, SEG_ALIGN)], xb_ref.at[pl.ds(d, SEG_ALIGN)], sem))

        def spare_copy(j):
            return pltpu.make_async_copy(stg_sc.at[pl.ds(0, MOE_BLK)],
                                         xb_ref.at[pl.ds(pl.multiple_of(j * MOE_BLK, MOE_BLK), MOE_BLK)], sem)

        def start_spare(j, carry):
            spare_copy(j).start()
            return carry

        def wait_spare(j, carry):
            spare_copy(j).wait()
            return carry

        first_spare = info_ref[INFO_NUSED]
        lax.fori_loop(first_spare, xb_ref.shape[0] // MOE_BLK, start_spare, 0)
        lax.fori_loop(first_spare, xb_ref.shape[0] // MOE_BLK, wait_spare, 0)

    spos_t = mi_ref[...].astype(F32).T
    r = lax.broadcasted_iota(I32, (stg_sc.shape[0], tm), 0).astype(F32)
    pm = r == spos_t[TOP_K:TOP_K + 1]
    for kk in range(1, TOP_K):
        pm = pm | (r == spos_t[TOP_K + kk:TOP_K + kk + 1])
    stg_sc[...] = _dot(pm.astype(BF16), h_ref[...].astype(BF16))
    _segment_copies(info_ref, 0, lambda s, d: pltpu.make_async_copy(
        stg_sc.at[pl.ds(s, SEG_ALIGN)], xb_ref.at[pl.ds(d, SEG_ALIGN)], sem))


def _dispatch(info, h, mi, rows):
    T, D = h.shape
    return pl.pallas_call(
        _dispatch_kernel,
        grid=(T // TMR,),
        in_specs=[pl.BlockSpec((INFO_LANES,), lambda i: (i,), memory_space=pltpu.SMEM),
                  pl.BlockSpec((TMR, D), lambda i: (i, 0)),
                  pl.BlockSpec((TMR, LANES), lambda i: (i, 0))],
        out_specs=pl.BlockSpec(memory_space=pl.ANY),
        out_shape=jax.ShapeDtypeStruct((rows, D), F32),
        scratch_shapes=[pltpu.VMEM((STG_ROWS, D), F32), pltpu.SemaphoreType.DMA],
        compiler_params=_params("arbitrary"),
        name="dispatch",
    )(info, h, mi)


def _gmm_kernel(be_ref, nu_ref, nxt_ref, slot_ref, x_ref, wgu_hbm, bgu_ref, wdn_hbm, bdn_ref, o_ref,
                gu_buf, dn_buf, wgu_sc, wdn_sc, sems, *, layer):
    i = pl.program_id(0)
    e = be_ref[i]
    prev = be_ref[jnp.maximum(i - 1, 0)]

    def fetch(expert, slot):
        return (pltpu.make_async_copy(wgu_hbm.at[layer, expert], gu_buf.at[slot], sems.at[0, slot]),
                pltpu.make_async_copy(wdn_hbm.at[layer, expert], dn_buf.at[slot], sems.at[1, slot]))

    @pl.when(i == 0)
    def _():
        for cp in fetch(e, slot_ref[0]):
            cp.start()

    @pl.when(((i == 0) | (e != prev)) & (i < nu_ref[0]))
    def _():
        slot = slot_ref[i]
        for cp in fetch(e, slot):
            cp.wait()
        wgu_sc[...] = gu_buf[slot].astype(BF16)
        wdn_sc[...] = dn_buf[slot].astype(BF16)

        @pl.when(nxt_ref[i] >= 0)
        def _():
            for cp in fetch(nxt_ref[i], 1 - slot):
                cp.start()

    @pl.when(i < nu_ref[0])
    def _():
        xb = x_ref[...].astype(BF16)
        gu = _dot(xb, wgu_sc[...]) + bgu_ref[0, 0]
        glu = jnp.minimum(gu[:, 0:D_EXPERT], SWIGLU_LIMIT)
        lin = jnp.clip(gu[:, D_EXPERT:2 * D_EXPERT], -SWIGLU_LIMIT, SWIGLU_LIMIT)
        act = glu * _sigmoid(SWIGLU_ALPHA * glu) * (lin + 1.0)
        o_ref[...] = _dot(act.astype(BF16), wdn_sc[...]) + bdn_ref[0, 0]

    @pl.when(i >= nu_ref[0])
    def _():
        o_ref[...] = jnp.zeros(o_ref.shape, F32)


def _gmm(block_e, n_used, nxt, slot, xb, wgu, bgu, wdn, bdn, l):
    P, D = xb.shape
    L, E, _, F2 = wgu.shape
    xmap = lambda i, be, nu, nx, sl: (jnp.minimum(i, nu[0] - 1), 0)
    bmap = lambda i, be, nu, nx, sl: (l, be[i], 0, 0)
    return pl.pallas_call(
        functools.partial(_gmm_kernel, layer=l),
        grid_spec=pltpu.PrefetchScalarGridSpec(
            num_scalar_prefetch=4,
            grid=(P // MOE_BLK,),
            in_specs=[pl.BlockSpec((MOE_BLK, D), xmap),
                      pl.BlockSpec(memory_space=pl.ANY),
                      pl.BlockSpec((1, 1, 1, F2), bmap),
                      pl.BlockSpec(memory_space=pl.ANY),
                      pl.BlockSpec((1, 1, 1, D), bmap)],
            out_specs=pl.BlockSpec((MOE_BLK, D), lambda i, be, nu, nx, sl: (i, 0)),
            scratch_shapes=[pltpu.VMEM((2, D, F2), F32), pltpu.VMEM((2, F2 // 2, D), F32),
                            pltpu.VMEM((D, F2), BF16), pltpu.VMEM((F2 // 2, D), BF16),
                            pltpu.SemaphoreType.DMA((2, 2))]),
        out_shape=jax.ShapeDtypeStruct((P, D), F32),
        compiler_params=_params("arbitrary"),
        name="gmm",
    )(block_e, n_used, nxt, slot, xb, wgu, bgu.reshape(L, E, 1, F2), wdn, bdn.reshape(L, E, 1, D))


def _combine_kernel(info_ref, x_ref, mi_ref, gate_ref, g2_ref, fw_ref, yb_ref, o_ref, stg_sc, sem, *, final):
    tm = x_ref.shape[0]

    @pl.when(pl.program_id(0) == 0)
    def _():
        stg_sc[...] = jnp.zeros(stg_sc.shape, F32)

    _segment_copies(info_ref, 0, lambda s, d: pltpu.make_async_copy(
        yb_ref.at[pl.ds(d, SEG_ALIGN)], stg_sc.at[pl.ds(s, SEG_ALIGN)], sem))

    spos = mi_ref[...].astype(F32)
    gate = gate_ref[...]
    r = lax.broadcasted_iota(I32, (tm, stg_sc.shape[0]), 1).astype(F32)
    g = jnp.where(r == spos[:, TOP_K:TOP_K + 1], gate[:, 0:1], 0.0)
    for kk in range(1, TOP_K):
        g = g + jnp.where(r == spos[:, TOP_K + kk:TOP_K + kk + 1], gate[:, kk:kk + 1], 0.0)
    g_hi, g_lo = _split2(g)
    y = stg_sc[...].astype(BF16)
    out = x_ref[...] + g2_ref[0] * (_dot(g_hi, y) + _dot(g_lo, y))
    if final:
        ms = jnp.mean(out * out, axis=-1, keepdims=True)
        out = out * lax.rsqrt(ms + EPS) * fw_ref[...]
    o_ref[...] = out


def _combine(info, x, mi, gates, mod, fw, yb, lay, final):
    T, D = x.shape
    B, S = lay["B"], lay["S"]
    mrow = lambda i: jnp.where(i < (B * S) // TMR, i // (S // TMR), B)
    return pl.pallas_call(
        functools.partial(_combine_kernel, final=final),
        grid=(T // TMR,),
        in_specs=[pl.BlockSpec((INFO_LANES,), lambda i: (i,), memory_space=pltpu.SMEM),
                  pl.BlockSpec((TMR, D), lambda i: (i, 0)),
                  pl.BlockSpec((TMR, LANES), lambda i: (i, 0)),
                  pl.BlockSpec((TMR, LANES), lambda i: (i, 0)),
                  pl.BlockSpec((1, 1, D), lambda i: (mrow(i), 0, 5)),
                  pl.BlockSpec((1, D), lambda i: (0, 0)),
                  pl.BlockSpec(memory_space=pl.ANY)],
        out_specs=pl.BlockSpec((TMR, D), lambda i: (i, 0)),
        out_shape=jax.ShapeDtypeStruct((T, D), F32),
        scratch_shapes=[pltpu.VMEM((STG_ROWS, D), F32), pltpu.SemaphoreType.DMA],
        compiler_params=_params("arbitrary"),
        name="combine",
    )(info, x, mi, gates, mod, fw, yb)


def _moe(x, nw, mod, wr, br, wgu, bgu, wdn, bdn, fw, lay, l, final):
    T, D = x.shape
    ntb = T // TMR
    h, mi, mf, blk, tot = _route(x, nw, mod, wr, br, lay)
    counts = tot[0, :N_EXPERTS].astype(I32)
    padded = (counts + MOE_BLK - 1) // MOE_BLK * MOE_BLK
    pad_ends = jnp.cumsum(padded)
    pad_starts = pad_ends - padded
    nblk = -(-(T * TOP_K + ntb * N_EXPERTS * (SEG_ALIGN - 1)) // MOE_BLK) + N_EXPERTS
    starts = jnp.arange(nblk, dtype=I32) * MOE_BLK
    block_e = jnp.minimum(jnp.sum((pad_ends[None, :] <= starts[:, None]).astype(I32), axis=1), N_EXPERTS - 1)
    n_used = (pad_ends[N_EXPERTS - 1:] // MOE_BLK).astype(I32)
    rep = lambda v: jnp.broadcast_to(v[None, :], (ntb, v.shape[0]))
    zeros_e = jnp.zeros((N_EXPERTS,), I32)
    info = jnp.concatenate([
        blk[:, 0, :N_EXPERTS], blk[:, 1, :N_EXPERTS], blk[:, 2, :N_EXPERTS] + pad_starts[None, :], rep(zeros_e),
        rep(padded - counts), rep(zeros_e), rep(pad_starts + counts),
        rep(jnp.concatenate([n_used, jnp.zeros((INFO_LANES - INFO_NUSED - 1,), I32)]))], axis=1).reshape(-1)
    eid = jnp.arange(N_EXPERTS, dtype=I32)
    has = padded > 0
    later = jnp.where(has[None, :] & (eid[None, :] > eid[:, None]), eid[None, :], N_EXPERTS)
    nxt_e = jnp.min(later, axis=1)
    nxt_e = jnp.where(nxt_e < N_EXPERTS, nxt_e, -1)
    slot_e = (jnp.cumsum(has.astype(I32)) - 1) % 2
    xb = _dispatch(info, h, mi, nblk * MOE_BLK)
    yb = _gmm(block_e, n_used, jnp.take(nxt_e, block_e), jnp.take(slot_e, block_e), xb, wgu, bgu, wdn, bdn, l)
    return _combine(info, x, mi, mf, mod, fw, yb, lay, final)


def _rope_tables(S):
    t = np.arange(S)
    row = (t // GRID_W).astype(np.float64)
    col = (t % GRID_W).astype(np.float64)
    lane = np.arange(LANES)
    j = lane % 16
    inv = ROPE_THETA ** (-(j.astype(np.float32)) / np.float32(16.0))
    pos = np.where((lane % 64) < 32, row[:, None], col[:, None]).astype(np.float32)
    ang = pos * inv.astype(np.float32)[None, :]
    sign = np.where((lane % 32) < 16, -1.0, 1.0).astype(np.float32)
    return ang.astype(np.float32), sign


def kernel(x, c, ctx, c_ctx, w_ada, b_ada, norm1_w, w_in, da_lambda, da_subln_w, ssd_conv_w, ssd_conv_b,
           ssd_a_log, ssd_dt_bias, ssd_d, ssd_norm_w, gla_gk_up, gla_gk_b, gla_norm_w, w_out, norm2_w,
           w_router, b_router, w_gate_up, b_gate_up, w_down, b_down, final_norm_w):
    B, S, D = x.shape
    Lc = ctx.shape[1]
    depth = w_ada.shape[0]
    assert S % TMR == 0 and Lc % TM == 0 and (B * Lc) % TMR == 0 and S % GRID_W == 0
    lay = dict(B=B, S=S, Lc=Lc, nlb=(B * S) // TM, bpb=S // TM, cpb=Lc // TM)
    n_lat = B * S

    xs = jnp.concatenate([x.reshape(B * S, D), ctx.reshape(B * Lc, D)], axis=0)

    cc = jnp.zeros((8, D), F32).at[0:B].set(c).at[B].set(c_ctx)
    mod_all = _ada(cc, w_ada, b_ada)

    ang, sign = _rope_tables(S)
    ang = jnp.asarray(ang)
    cos_t = jnp.concatenate([jnp.cos(ang), jnp.ones((TM, LANES), F32)], axis=0)
    sin_t = jnp.concatenate([jnp.sin(ang) * jnp.asarray(sign)[None, :], jnp.zeros((TM, LANES), F32)], axis=0)

    for l in range(depth):
        last = l == depth - 1
        lam_init = 0.8 - 0.6 * math.exp(-0.3 * l)
        mod = mod_all[l, 0:B + 1].reshape(B + 1, 1, 6 * D)

        wi = w_in[l]
        misc_w = jnp.zeros((D, LANES), F32)
        misc_w = misc_w.at[:, MISC_DT:MISC_DT + 8].set(wi[:, 2304:2312])
        misc_w = misc_w.at[:, MISC_CODE:MISC_CODE + 2 * GLA_RANK].set(wi[:, 3080:3112])
        w_re = jnp.concatenate([wi[:, 0:2304], wi[:, 2312:3080], misc_w], axis=1).astype(BF16)
        a128 = jnp.zeros((1, LANES), F32).at[0, 0:8].set(-jnp.exp(ssd_a_log[l].astype(F32)).reshape(-1))
        bias128 = jnp.zeros((1, LANES), F32).at[0, 0:8].set(ssd_dt_bias[l].astype(F32).reshape(-1))
        wgk = jnp.zeros((2, LANES, GLA_KEY_WIDTH), F32)
        for d in range(2):
            wgk = wgk.at[d, MISC_CODE + d * GLA_RANK:MISC_CODE + (d + 1) * GLA_RANK, :].set(gla_gk_up[l, d])
        bgk = gla_gk_b[l].reshape(2, 1, GLA_KEY_WIDTH)
        dsk = jnp.repeat(ssd_d[l], SSD_HEAD_DIM).reshape(1, SSD_WIDTH)
        snw = ssd_norm_w[l].reshape(1, SSD_WIDTH)
        gnw = jnp.tile(gla_norm_w[l], GLA_HEADS).reshape(1, GLA_WIDTH)
        wr = jnp.zeros((D, LANES), F32).at[:, 0:N_EXPERTS].set(w_router[l])
        br = jnp.zeros((1, LANES), F32).at[0, 0:N_EXPERTS].set(b_router[l])

        q, k, v, z, xbc, gla_in, misc = _inproj(xs, norm1_w[l].reshape(1, D), mod, cos_t, sin_t, w_re, lay)

        a = _attention(q, k, v, da_lambda[l], da_subln_w[l].reshape(1, DA_V), lay, lam_init,
                       ctx_queries=not last)
        xbc_act = _conv(xbc, ssd_conv_w[l], ssd_conv_b[l].reshape(1, SSD_CONV_CH), lay)
        yf, yb = _ssd(xbc_act, misc, a128, bias128, lay)
        of, ob = _gla(gla_in, misc, wgk, bgk, lay)

        nrows = n_lat if last else xs.shape[0]
        xs = _outproj(xs, a, yf, yb, xbc_act, z, of, ob, gla_in, dsk, snw, gnw,
                      w_out[l].astype(BF16), mod, lay, nrows)
        xs = _moe(xs, norm2_w[l].reshape(1, D), mod, wr, br, w_gate_up, b_gate_up, w_down, b_down,
                  final_norm_w.reshape(1, D), lay, l, last)

    return xs.reshape(B, S, D)
```

```python
import functools
import math

import numpy as np
import jax
import jax.numpy as jnp
from jax import lax
from jax.experimental import pallas as pl
from jax.experimental.pallas import tpu as pltpu

F32 = jnp.float32
BF16 = jnp.bfloat16
I32 = jnp.int32

GRID_W = 64
EPS = 1e-6
DA_HEADS = 4
DA_QK = 64
DA_V = 128
DA_WIDTH = 512
ROPE_THETA = 10000.0
SSD_HEADS = 4
SSD_HEAD_DIM = 64
SSD_WIDTH = 256
SSD_STATE = 64
SSD_CONV_CH = 512
GLA_HEADS = 4
GLA_DK = 32
GLA_DV = 64
GLA_KEY_WIDTH = 128
GLA_WIDTH = 256
GLA_RANK = 16
GLA_NORMALIZER = 16.0
N_EXPERTS = 32
TOP_K = 4
D_EXPERT = 1024
SWIGLU_LIMIT = 7.0
SWIGLU_ALPHA = 1.702

LANES = 128
TM = 256
SSD_Q = 128
GLA_Q = 64
GLA_SUB = 4
ATT_TQ = 256
ATT_KT = (2816, 768, 512, 256)
LOG2E = 1.4426950408889634
MOE_BLK = 256
TMR = 512
SEG_ALIGN = 8
STG_ROWS = TMR * TOP_K + N_EXPERTS * SEG_ALIGN
INFO_LANES = 256
INFO_TAIL = 128
INFO_NUSED = 224
VMEM_LIMIT = 56 * 1024 * 1024

C_Q, C_K, C_V, C_Z, C_XBC, C_GLA, C_MISC, C_END = 0, 512, 1024, 1536, 1792, 2304, 3072, 3200
MISC_DT = 0
MISC_CODE = 8


def _sigmoid(x):
    return 1.0 / (1.0 + jnp.exp(-x))


def _softplus(x):
    return jnp.maximum(x, 0.0) + jnp.log(1.0 + jnp.exp(-jnp.abs(x)))


def _split2(a):
    hi = a.astype(BF16)
    lo = (a - hi.astype(F32)).astype(BF16)
    return hi, lo


def _split3(a):
    a1 = a.astype(BF16)
    r1 = a - a1.astype(F32)
    a2 = r1.astype(BF16)
    a3 = (r1 - a2.astype(F32)).astype(BF16)
    return a1, a2, a3


def _dot(a, b):
    return jnp.dot(a, b, preferred_element_type=F32)


def _dot_hi(a, b):
    a1, a2 = _split2(a)
    b1, b2 = _split2(b)
    return _dot(a1, b1) + (_dot(a1, b2) + _dot(a2, b1))


def _dot_exact_lhs(m, a):
    a1, a2, a3 = _split3(a)
    mb = m.astype(BF16)
    return _dot(mb, a1) + (_dot(mb, a2) + _dot(mb, a3))


def _dot_exact_rhs(a, m):
    a1, a2, a3 = _split3(a)
    mb = m.astype(BF16)
    return _dot(a1, mb) + (_dot(a2, mb) + _dot(a3, mb))


def _params(*sem):
    return pltpu.CompilerParams(dimension_semantics=sem, vmem_limit_bytes=VMEM_LIMIT)


def _ada_kernel(c_ref, w_ref, b_ref, o_ref):
    c = c_ref[...]
    s = c * _sigmoid(c)
    o_ref[0] = _dot_hi(s, w_ref[0]) + b_ref[0]


def _ada(cc, w_ada, b_ada):
    L, D, N = w_ada.shape
    tn = 512
    return pl.pallas_call(
        _ada_kernel,
        grid=(L, N // tn),
        in_specs=[pl.BlockSpec((8, D), lambda l, j: (0, 0)),
                  pl.BlockSpec((1, D, tn), lambda l, j: (l, 0, j)),
                  pl.BlockSpec((1, 1, tn), lambda l, j: (l, 0, j))],
        out_specs=pl.BlockSpec((1, 8, tn), lambda l, j: (l, 0, j)),
        out_shape=jax.ShapeDtypeStruct((L, 8, N), F32),
        compiler_params=_params("parallel", "parallel"),
        name="ada",
    )(cc, w_ada, b_ada.reshape(L, 1, N))


def _inproj_kernel(x_ref, nw_ref, sh_ref, sc_ref, cos_ref, sin_ref, w_ref,
                   q_ref, k_ref, v_ref, z_ref, xbc_ref, gla_ref, misc_ref):
    xf = x_ref[...]
    ms = jnp.mean(xf * xf, axis=-1, keepdims=True)
    h = (xf * lax.rsqrt(ms + EPS) * nw_ref[...]) * (1.0 + sc_ref[0]) + sh_ref[0]
    hb = h.astype(BF16)
    cos = cos_ref[...]
    sin = sin_ref[...]
    lane = lax.broadcasted_iota(I32, cos.shape, 1)
    first = (lane % 32) < 16

    def rope(p):
        outs = []
        for hd in range(DA_HEADS):
            ph = p[:, hd * LANES:(hd + 1) * LANES]
            sw = jnp.where(first, pltpu.roll(ph, LANES - 16, 1), pltpu.roll(ph, 16, 1))
            outs.append(ph * cos + sw * sin)
        return jnp.concatenate(outs, axis=1)

    q = rope(_dot(hb, w_ref[:, C_Q:C_K])) * (DA_QK ** -0.5 * LOG2E)
    q_ref[...] = q.astype(BF16)
    k_ref[...] = rope(_dot(hb, w_ref[:, C_K:C_V])).astype(BF16)
    v_ref[...] = _dot(hb, w_ref[:, C_V:C_Z]).astype(BF16)
    z_ref[...] = _dot(hb, w_ref[:, C_Z:C_XBC])
    xbc_ref[...] = _dot(hb, w_ref[:, C_XBC:C_GLA])
    gla_ref[...] = _dot(hb, w_ref[:, C_GLA:C_MISC])
    misc_ref[...] = _dot(hb, w_ref[:, C_MISC:C_END])


def _inproj(x, nw, mod, cos_t, sin_t, w, lay):
    T, D = x.shape
    nlb, bpb, B = lay["nlb"], lay["bpb"], lay["B"]
    mrow = lambda i: jnp.where(i < nlb, i // bpb, B)
    prow = lambda i: jnp.where(i < nlb, i % bpb, bpb)
    row = lambda n: pl.BlockSpec((TM, n), lambda i: (i, 0))
    spb = bpb + lay["cpb"]
    kvrow = lambda i: (jnp.where(i < nlb, (i // bpb) * spb + lay["cpb"] + i % bpb,
                                 ((i - nlb) // lay["cpb"]) * spb + (i - nlb) % lay["cpb"]), 0)
    kv = pl.BlockSpec((TM, DA_WIDTH), kvrow)
    outs = [(512, BF16), (512, BF16), (512, BF16), (256, F32), (512, F32), (768, F32), (128, F32)]
    out_specs = [row(n) for n, _ in outs]
    out_specs[1] = kv
    out_specs[2] = kv
    return pl.pallas_call(
        _inproj_kernel,
        grid=(T // TM,),
        in_specs=[row(D),
                  pl.BlockSpec((1, D), lambda i: (0, 0)),
                  pl.BlockSpec((1, 1, D), lambda i: (mrow(i), 0, 0)),
                  pl.BlockSpec((1, 1, D), lambda i: (mrow(i), 0, 1)),
                  pl.BlockSpec((TM, LANES), lambda i: (prow(i), 0)),
                  pl.BlockSpec((TM, LANES), lambda i: (prow(i), 0)),
                  pl.BlockSpec((D, C_END), lambda i: (0, 0))],
        out_specs=out_specs,
        out_shape=[jax.ShapeDtypeStruct((T, n), dt) for n, dt in outs],
        compiler_params=_params("parallel"),
        name="inproj",
    )(x, nw, mod, mod, cos_t, sin_t, w)


def _lane_fold(x, op):
    f = x[:, 0:LANES]
    for i in range(1, x.shape[1] // LANES):
        f = op(f, x[:, i * LANES:(i + 1) * LANES])
    return f


def _attn_stack_maps(q_ref, q2_sc):
    tq = q_ref.shape[0]
    q = q_ref[...]
    lane = lax.broadcasted_iota(I32, q.shape, 1)
    zero = jnp.zeros_like(q)
    q2_sc[0:tq, :] = jnp.where(lane < DA_QK, q, zero)
    q2_sc[tq:2 * tq, :] = jnp.where(lane >= DA_QK, q, zero)


def _attn_scores_tile(q2_sc, k_ref, s_sc, slot, mf_sc, j, kt):
    kj = k_ref[pl.ds(pl.multiple_of(j * kt, kt), kt), :]
    s = lax.dot_general(q2_sc[...], kj, (((1,), (1,)), ((), ())), preferred_element_type=F32)
    s_sc[slot, j, :, 0:kt] = s
    mf_sc[...] = jnp.maximum(mf_sc[...], _lane_fold(s, jnp.maximum))


def _attn_weigh_tile(v_ref, s_sc, slot, p_sc, l_sc, acc_sc, m, j, kt):
    sub = next(w for w in (512, 256) if kt % w == 0)
    for c0 in range(0, kt, sub):
        p = jnp.exp2(s_sc[slot, j, :, c0:c0 + sub] - m)
        l_sc[...] += _lane_fold(p, jnp.add)
        p_sc[:, c0:c0 + sub] = p.astype(BF16)
    vj = v_ref[pl.ds(pl.multiple_of(j * kt, kt), kt), :]
    acc_sc[...] += _dot(p_sc[:, 0:kt], vj)


def _attn_finish(lam_ref, w_ref, o_ref, l_sc, acc_sc, lam_init):
    tq = o_ref.shape[0]
    lp = lam_ref[...]
    lam = (jnp.exp(jnp.sum(lp[0:1] * lp[1:2], axis=1, keepdims=True))
           - jnp.exp(jnp.sum(lp[2:3] * lp[3:4], axis=1, keepdims=True)) + lam_init)
    acc = acc_sc[...]
    l = jnp.sum(l_sc[...], axis=1, keepdims=True)
    o = acc[0:tq] / l[0:tq] - lam * (acc[tq:2 * tq] / l[tq:2 * tq])
    ms = jnp.mean(o * o, axis=-1, keepdims=True)
    o_ref[...] = (o * lax.rsqrt(ms + EPS) * w_ref[...]) * (1.0 - lam_init)


def _attn_kernel(q_ref, k_ref, v_ref, lam_ref, w_ref, o_ref, q2_sc, s_sc, p_sc, mf_sc, l_sc, acc_sc,
                 *, nt, kt, lam_init):
    _attn_stack_maps(q_ref, q2_sc)
    mf_sc[...] = jnp.full(mf_sc.shape, -jnp.inf, F32)
    l_sc[...] = jnp.zeros(l_sc.shape, F32)
    acc_sc[...] = jnp.zeros(acc_sc.shape, F32)

    def scores(j, carry):
        _attn_scores_tile(q2_sc, k_ref, s_sc, 0, mf_sc, j, kt)
        return carry

    lax.fori_loop(0, nt, scores, 0)
    m = jnp.max(mf_sc[...], axis=1, keepdims=True)

    def weigh(j, carry):
        _attn_weigh_tile(v_ref, s_sc, 0, p_sc, l_sc, acc_sc, m, j, kt)
        return carry

    lax.fori_loop(0, nt, weigh, 0)
    _attn_finish(lam_ref, w_ref, o_ref, l_sc, acc_sc, lam_init)


def _attn_pipe_kernel(q_ref, k_ref, v_ref, lam_ref, w_ref, o_ref, q2_sc, s_sc, p_sc, mf_sc, m_sc, l_sc, acc_sc,
                      *, nq, nt, kt, lam_init):
    t = pl.program_id(2)

    @pl.when(t < nq)
    def _():
        _attn_stack_maps(q_ref, q2_sc)
        mf_sc[...] = jnp.full(mf_sc.shape, -jnp.inf, F32)

    @pl.when(t > 0)
    def _():
        l_sc[...] = jnp.zeros(l_sc.shape, F32)
        acc_sc[...] = jnp.zeros(acc_sc.shape, F32)

    def run(slot_a, slot_b):
        m_prev = None if slot_b is None else m_sc[slot_b]

        def body(j, carry):
            if slot_b is not None:
                _attn_weigh_tile(v_ref, s_sc, slot_b, p_sc, l_sc, acc_sc, m_prev, j, kt)
            if slot_a is not None:
                _attn_scores_tile(q2_sc, k_ref, s_sc, slot_a, mf_sc, j, kt)
            return carry

        lax.fori_loop(0, nt, body, 0)
        if slot_a is not None:
            m_sc[slot_a] = jnp.max(mf_sc[...], axis=1, keepdims=True)

    inner = (t > 0) & (t < nq)
    pl.when(t == 0)(lambda: run(0, None))
    pl.when(inner & (t % 2 == 1))(lambda: run(1, 0))
    pl.when(inner & (t % 2 == 0))(lambda: run(0, 1))
    pl.when(t == nq)(lambda: run(None, (nq - 1) % 2))

    @pl.when(t > 0)
    def _():
        _attn_finish(lam_ref, w_ref, o_ref, l_sc, acc_sc, lam_init)


def _attention(q, k, v, lam_p, subln_w, lay, lam_init, *, ctx_queries):
    B, S, Lc = lay["B"], lay["S"], lay["Lc"]
    nkeys = Lc + S
    assert nkeys % Lc == 0
    small = [pl.BlockSpec((4, DA_QK), lambda b, h, t: (0, 0)), pl.BlockSpec((1, DA_V), lambda b, h, t: (0, 0))]
    stacked = lambda n, dt: pltpu.VMEM((2 * ATT_TQ, n), dt)

    kt = next(t for t in ATT_KT if nkeys % t == 0)
    nq = S // ATT_TQ
    kmap = lambda b, h, t: (b, h)
    a_lat = pl.pallas_call(
        functools.partial(_attn_pipe_kernel, nq=nq, nt=nkeys // kt, kt=kt, lam_init=lam_init),
        grid=(B, DA_HEADS, nq + 1),
        in_specs=[pl.BlockSpec((ATT_TQ, LANES), lambda b, h, t: (b * nq + jnp.minimum(t, nq - 1), h)),
                  pl.BlockSpec((nkeys, LANES), kmap),
                  pl.BlockSpec((nkeys, LANES), kmap)] + small,
        out_specs=pl.BlockSpec((ATT_TQ, LANES), lambda b, h, t: (b * nq + jnp.maximum(t - 1, 0), h)),
        out_shape=jax.ShapeDtypeStruct((B * S, DA_WIDTH), F32),
        scratch_shapes=[stacked(LANES, BF16),
                        pltpu.VMEM((2, nkeys // kt, 2 * ATT_TQ, kt), F32),
                        stacked(kt, BF16), stacked(LANES, F32),
                        pltpu.VMEM((2, 2 * ATT_TQ, 1), F32),
                        stacked(LANES, F32), stacked(LANES, F32)],
        compiler_params=_params("parallel", "parallel", "arbitrary"),
        name="attn",
    )(q, k, v, lam_p, subln_w)
    if not ctx_queries:
        return a_lat

    ktc = next(t for t in ATT_KT if Lc % t == 0)
    nqc = Lc // ATT_TQ
    cmap = lambda b, h, t: (b * (nkeys // Lc), h)
    a_ctx = pl.pallas_call(
        functools.partial(_attn_kernel, nt=Lc // ktc, kt=ktc, lam_init=lam_init),
        grid=(B, DA_HEADS, nqc),
        in_specs=[pl.BlockSpec((ATT_TQ, LANES), lambda b, h, t: ((B * S) // ATT_TQ + b * nqc + t, h)),
                  pl.BlockSpec((Lc, LANES), cmap),
                  pl.BlockSpec((Lc, LANES), cmap)] + small,
        out_specs=pl.BlockSpec((ATT_TQ, LANES), lambda b, h, t: (b * nqc + t, h)),
        out_shape=jax.ShapeDtypeStruct((B * Lc, DA_WIDTH), F32),
        scratch_shapes=[stacked(LANES, BF16),
                        pltpu.VMEM((1, Lc // ktc, 2 * ATT_TQ, ktc), F32),
                        stacked(ktc, BF16), stacked(LANES, F32), stacked(LANES, F32), stacked(LANES, F32)],
        compiler_params=_params("parallel", "parallel", "arbitrary"),
        name="attn_ctx",
    )(q, k, v, lam_p, subln_w)
    return jnp.concatenate([a_lat, a_ctx], axis=0)


def _conv_kernel(flag_ref, x_ref, p_ref, n_ref, w_ref, b_ref, o_ref):
    i = pl.program_id(0)
    x = x_ref[...]
    keep_prev = (flag_ref[i, 0] == 0).astype(F32)
    keep_next = (flag_ref[i, 1] == 0).astype(F32)
    prev_row = p_ref[7:8, :] * keep_prev
    next_row = n_ref[0:1, :] * keep_next
    ridx = lax.broadcasted_iota(I32, x.shape, 0)
    xm1 = jnp.where(ridx == 0, prev_row, pltpu.roll(x, 1, 0))
    xp1 = jnp.where(ridx == x.shape[0] - 1, next_row, pltpu.roll(x, x.shape[0] - 1, 0))
    w = w_ref[...]
    y = w[0:1] * xm1 + w[1:2] * x + w[2:3] * xp1 + b_ref[...]
    o_ref[...] = y * _sigmoid(y)


def _conv(xbc, cw, cb, lay):
    T, C = xbc.shape
    nb = T // TM
    r8 = TM // 8
    flags = np.zeros((nb, 2), np.int32)
    for i in range(nb):
        if i < lay["nlb"]:
            flags[i, 0] = (i % lay["bpb"]) == 0
            flags[i, 1] = (i % lay["bpb"]) == lay["bpb"] - 1
        else:
            j = (i - lay["nlb"]) % lay["cpb"]
            flags[i, 0] = j == 0
            flags[i, 1] = j == lay["cpb"] - 1
    return pl.pallas_call(
        _conv_kernel,
        grid_spec=pltpu.PrefetchScalarGridSpec(
            num_scalar_prefetch=1,
            grid=(nb,),
            in_specs=[pl.BlockSpec((TM, C), lambda i, f: (i, 0)),
                      pl.BlockSpec((8, C), lambda i, f: (jnp.maximum(i * r8 - 1, 0), 0)),
                      pl.BlockSpec((8, C), lambda i, f: (jnp.minimum((i + 1) * r8, T // 8 - 1), 0)),
                      pl.BlockSpec((3, C), lambda i, f: (0, 0)),
                      pl.BlockSpec((1, C), lambda i, f: (0, 0))],
            out_specs=pl.BlockSpec((TM, C), lambda i, f: (i, 0))),
        out_shape=jax.ShapeDtypeStruct((T, C), F32),
        compiler_params=_params("parallel"),
        name="ssd_conv",
    )(jnp.asarray(flags), xbc, xbc, xbc, cw, cb)


def _ssd_dir(xbc, misc, a128, bias128, hs_ref, dirn, rev):
    Q = xbc.shape[0]
    dt = _softplus(misc + bias128)
    a = dt * a128
    ii = lax.broadcasted_iota(I32, (Q, Q), 0)
    jj = lax.broadcasted_iota(I32, (Q, Q), 1)
    M = (jj >= ii) if rev else (jj <= ii)
    acs = _dot_exact_lhs(M.astype(F32), a)
    acs_t = acs.T
    dt_t = dt.T
    xs = xbc[:, 0:SSD_WIDTH]
    Bt = xbc[:, SSD_WIDTH:SSD_WIDTH + 128].T
    Cm = xbc[:, SSD_WIDTH + 128:SSD_WIDTH + 256]
    last = 0 if rev else Q - 1
    ys = []
    for g in range(2):
        Cg = Cm[:, g * SSD_STATE:(g + 1) * SSD_STATE]
        Btg = Bt[g * SSD_STATE:(g + 1) * SSD_STATE, :]
        CB = _dot(Cg, Btg)
        for hh in range(2):
            h = 2 * g + hh
            li = dirn * SSD_HEADS + h
            acs_c = acs[:, li:li + 1]
            acs_r = acs_t[li:li + 1, :]
            dt_r = dt_t[li:li + 1, :]
            decay = jnp.exp(jnp.where(M, acs_c - acs_r, -jnp.inf))
            sc = CB * decay * dt_r
            x_h = xs[:, h * SSD_HEAD_DIM:(h + 1) * SSD_HEAD_DIM]
            hs = hs_ref[h]
            y = _dot(sc, x_h) + _dot(Cg * jnp.exp(acs_c), hs)
            tot = acs_r[:, last:last + 1]
            w_end = jnp.exp(tot - acs_r) * dt_r
            hs_ref[h] = jnp.exp(tot) * hs + _dot(Btg * w_end, x_h)
            ys.append(y)
    return jnp.concatenate(ys, axis=1)


def _ssd_kernel(xf_ref, mf_ref, xb_ref, mb_ref, a_ref, bias_ref, yf_ref, yb_ref, hf_sc, hb_sc):
    @pl.when(pl.program_id(1) == 0)
    def _():
        hf_sc[...] = jnp.zeros(hf_sc.shape, F32)
        hb_sc[...] = jnp.zeros(hb_sc.shape, F32)

    a128 = a_ref[...]
    bias128 = bias_ref[...]
    yf_ref[...] = _ssd_dir(xf_ref[...], mf_ref[...], a128, bias128, hf_sc, 0, False)
    yb_ref[...] = _ssd_dir(xb_ref[...], mb_ref[...], a128, bias128, hb_sc, 1, True)


def _scan_maps(lay, rows):
    ncc, ncl = lay["Lc"] // rows, lay["S"] // rows
    base = (lay["B"] * lay["S"]) // rows
    fmap = lambda b, c: (jnp.where(c < ncc, base + b * ncc + c, b * ncl + c - ncc), 0)
    bmap = lambda b, c: (jnp.where(c < ncc, base + b * ncc + (ncc - 1 - c),
                                   b * ncl + (ncl - 1 - (c - ncc))), 0)
    return ncc + ncl, fmap, bmap


def _ssd(xbc_act, misc, a128, bias128, lay):
    T = xbc_act.shape[0]
    nc, fmap, bmap = _scan_maps(lay, SSD_Q)
    vec = pl.BlockSpec((1, LANES), lambda b, c: (0, 0))
    return pl.pallas_call(
        _ssd_kernel,
        grid=(lay["B"], nc),
        in_specs=[pl.BlockSpec((SSD_Q, SSD_CONV_CH), fmap), pl.BlockSpec((SSD_Q, LANES), fmap),
                  pl.BlockSpec((SSD_Q, SSD_CONV_CH), bmap), pl.BlockSpec((SSD_Q, LANES), bmap),
                  vec, vec],
        out_specs=[pl.BlockSpec((SSD_Q, SSD_WIDTH), fmap), pl.BlockSpec((SSD_Q, SSD_WIDTH), bmap)],
        out_shape=[jax.ShapeDtypeStruct((T, SSD_WIDTH), F32)] * 2,
        scratch_shapes=[pltpu.VMEM((SSD_HEADS, SSD_STATE, SSD_HEAD_DIM), F32)] * 2,
        compiler_params=_params("parallel", "arbitrary"),
        name="ssd",
    )(xbc_act, misc, xbc_act, misc, a128, bias128)


def _gla_block(blk, misc, wgk, bgk, s_ref, rev):
    R = blk.shape[0]
    kw = GLA_KEY_WIDTH
    q, k, v = blk[:, 0:kw], blk[:, kw:2 * kw], blk[:, 2 * kw:2 * kw + GLA_WIDTH]
    pre = _dot_hi(misc, wgk) + bgk
    gk = (jnp.minimum(pre, 0.0) - jnp.log(1.0 + jnp.exp(-jnp.abs(pre)))) / GLA_NORMALIZER
    ii = lax.broadcasted_iota(I32, (R, R), 0)
    jj = lax.broadcasted_iota(I32, (R, R), 1)
    same = (ii // GLA_Q) == (jj // GLA_Q)
    M = same & ((jj >= ii) if rev else (jj <= ii))
    b = _dot_exact_lhs(M.astype(F32), gk)
    tot = _dot_exact_lhs(same.astype(F32), gk)
    q_t = (q * (GLA_DK ** -0.5)) * jnp.exp(b)
    k_t = k * jnp.exp(-b)
    k_end_t = (k * jnp.exp(tot - b)).T
    gk_t = gk.T
    lane_k = lax.broadcasted_iota(I32, (1, kw), 1) // GLA_DK
    lane_v = lax.broadcasted_iota(I32, (1, GLA_WIDTH), 1) // GLA_DV
    o = jnp.zeros((R, GLA_WIDTH), F32)
    for h in range(GLA_HEADS):
        qh = jnp.where(lane_k == h, q_t, 0.0)
        att = lax.dot_general(qh, k_t, (((1,), (1,)), ((), ())), preferred_element_type=F32)
        o = o + _dot(jnp.where(M, att, 0.0), jnp.where(lane_v == h, v, 0.0))
    col_chunk = lax.broadcasted_iota(I32, (1, R), 1) // GLA_Q
    S = s_ref[...]
    bd = (lax.broadcasted_iota(I32, S.shape, 0) // GLA_DK) == (lax.broadcasted_iota(I32, S.shape, 1) // GLA_DV)
    nsub = R // GLA_Q
    inter = [None] * nsub
    for c in (range(nsub - 1, -1, -1) if rev else range(nsub)):
        sel = col_chunk == c
        inter[c] = _dot(q_t[c * GLA_Q:(c + 1) * GLA_Q], S)
        decay = jnp.exp(jnp.sum(jnp.where(sel, gk_t, 0.0), axis=1, keepdims=True))
        S = jnp.where(bd, S * decay + _dot(jnp.where(sel, k_end_t, 0.0), v), 0.0)
    s_ref[...] = S
    return o + jnp.concatenate(inter, axis=0)


def _gla_kernel(gf_ref, mf_ref, gb_ref, mb_ref, wgk_ref, bgk_ref, of_ref, ob_ref, sf_sc, sb_sc):
    @pl.when(pl.program_id(1) == 0)
    def _():
        sf_sc[...] = jnp.zeros(sf_sc.shape, F32)
        sb_sc[...] = jnp.zeros(sb_sc.shape, F32)

    of_ref[...] = _gla_block(gf_ref[...], mf_ref[...], wgk_ref[0], bgk_ref[0], sf_sc, False)
    ob_ref[...] = _gla_block(gb_ref[...], mb_ref[...], wgk_ref[1], bgk_ref[1], sb_sc, True)


def _gla(gla_in, misc, wgk, bgk, lay):
    T = gla_in.shape[0]
    R = GLA_Q * GLA_SUB
    nc, fmap, bmap = _scan_maps(lay, R)
    return pl.pallas_call(
        _gla_kernel,
        grid=(lay["B"], nc),
        in_specs=[pl.BlockSpec((R, 512), fmap), pl.BlockSpec((R, LANES), fmap),
                  pl.BlockSpec((R, 512), bmap), pl.BlockSpec((R, LANES), bmap),
                  pl.BlockSpec((2, LANES, GLA_KEY_WIDTH), lambda b, c: (0, 0, 0)),
                  pl.BlockSpec((2, 1, GLA_KEY_WIDTH), lambda b, c: (0, 0, 0))],
        out_specs=[pl.BlockSpec((R, GLA_WIDTH), fmap), pl.BlockSpec((R, GLA_WIDTH), bmap)],
        out_shape=[jax.ShapeDtypeStruct((T, GLA_WIDTH), F32)] * 2,
        scratch_shapes=[pltpu.VMEM((GLA_KEY_WIDTH, GLA_WIDTH), F32)] * 2,
        compiler_params=_params("parallel", "arbitrary"),
        name="gla",
    )(gla_in, misc, gla_in, misc, wgk, bgk)


def _outproj_kernel(x_ref, a_ref, yf_ref, yb_ref, xs_ref, z_ref, of_ref, ob_ref, g_ref,
                    dsk_ref, snw_ref, gnw_ref, w_ref, g1_ref, o_ref):
    y = yf_ref[...] + yb_ref[...] + dsk_ref[...] * xs_ref[...]
    z = z_ref[...]
    gs = y * (z * _sigmoid(z))
    half = SSD_WIDTH // 2
    parts = []
    for grp in range(2):
        seg = gs[:, grp * half:(grp + 1) * half]
        parts.append(seg * lax.rsqrt(jnp.mean(seg * seg, axis=-1, keepdims=True) + EPS))
    s = jnp.concatenate(parts, axis=1) * snw_ref[...]
    o = of_ref[...] + ob_ref[...]
    bd = ((lax.broadcasted_iota(I32, (GLA_WIDTH, GLA_WIDTH), 0) // GLA_DV)
          == (lax.broadcasted_iota(I32, (GLA_WIDTH, GLA_WIDTH), 1) // GLA_DV))
    ms = _dot_exact_rhs(o * o, jnp.where(bd, 1.0 / GLA_DV, 0.0))
    g = g_ref[...]
    c = (o * lax.rsqrt(ms + EPS) * gnw_ref[...]) * (g * _sigmoid(g))
    mix = jnp.concatenate([a_ref[...], s, c], axis=1).astype(BF16)
    o_ref[...] = x_ref[...] + g1_ref[0] * _dot(mix, w_ref[...])


def _outproj(x, a, yf, yb, xbc_act, z, of, ob, gla_in, dsk, snw, gnw, w_out, mod, lay, nrows):
    D = x.shape[1]
    nlb, bpb, B = lay["nlb"], lay["bpb"], lay["B"]
    mrow = lambda i: jnp.where(i < nlb, i // bpb, B)
    row = lambda n, cb=0: pl.BlockSpec((TM, n), lambda i: (i, cb))
    vec = lambda n: pl.BlockSpec((1, n), lambda i: (0, 0))
    return pl.pallas_call(
        _outproj_kernel,
        grid=(nrows // TM,),
        in_specs=[row(D), row(DA_WIDTH), row(SSD_WIDTH), row(SSD_WIDTH), row(SSD_WIDTH), row(SSD_WIDTH),
                  row(GLA_WIDTH), row(GLA_WIDTH), row(GLA_WIDTH, 2),
                  vec(SSD_WIDTH), vec(SSD_WIDTH), vec(GLA_WIDTH),
                  pl.BlockSpec((D, D), lambda i: (0, 0)),
                  pl.BlockSpec((1, 1, D), lambda i: (mrow(i), 0, 2))],
        out_specs=row(D),
        out_shape=jax.ShapeDtypeStruct((nrows, D), F32),
        compiler_params=_params("parallel"),
        name="outproj",
    )(x, a, yf, yb, xbc_act, z, of, ob, gla_in, dsk, snw, gnw, w_out, mod)


def _route_kernel(x_ref, nw_ref, sh_ref, sc_ref, wr_ref, br_ref,
                  h_ref, mi_ref, mf_ref, blk_ref, tot_ref, carry_sc):
    i = pl.program_id(0)

    @pl.when(i == 0)
    def _():
        carry_sc[...] = jnp.zeros(carry_sc.shape, F32)

    xf = x_ref[...]
    ms = jnp.mean(xf * xf, axis=-1, keepdims=True)
    h = (xf * lax.rsqrt(ms + EPS) * nw_ref[...]) * (1.0 + sc_ref[0]) + sh_ref[0]
    h_ref[...] = h
    tm = h.shape[0]
    lane = lax.broadcasted_iota(I32, (tm, LANES), 1)
    logits = jnp.where(lane < N_EXPERTS, _dot_hi(h, wr_ref[...]) + br_ref[...], -jnp.inf)
    vals, idxs, hots = [], [], []
    l = logits
    for _ in range(TOP_K):
        m = jnp.max(l, axis=1, keepdims=True)
        idx = jnp.min(jnp.where(l == m, lane, LANES), axis=1, keepdims=True)
        hot = lane == idx
        vals.append(m)
        idxs.append(idx)
        hots.append(hot)
        l = jnp.where(hot, -jnp.inf, l)
    es = [jnp.exp(v - vals[0]) for v in vals]
    den = es[0] + es[1] + es[2] + es[3]
    hot_all = jnp.zeros((tm, LANES), F32)
    for hot in hots:
        hot_all = hot_all + hot.astype(F32)
    ii = lax.broadcasted_iota(I32, (tm, tm), 0)
    jj = lax.broadcasted_iota(I32, (tm, tm), 1)
    before = _dot((jj < ii).astype(BF16), hot_all.astype(BF16))
    cnt = jnp.sum(hot_all, axis=0, keepdims=True)
    cnt8 = jnp.floor((cnt + (SEG_ALIGN - 1.0)) * (1.0 / SEG_ALIGN)) * SEG_ALIGN
    ei = lax.broadcasted_iota(I32, (LANES, LANES), 0)
    ej = lax.broadcasted_iota(I32, (LANES, LANES), 1)
    units = jnp.broadcast_to(cnt8 * (1.0 / SEG_ALIGN), (8, LANES)).astype(BF16)
    seg = _dot(units, (ei < ej).astype(BF16))[0:1] * SEG_ALIGN
    pos_e = seg + before
    mi = jnp.zeros((tm, LANES), I32)
    mf = jnp.zeros((tm, LANES), F32)
    for kk in range(TOP_K):
        spos = jnp.sum(jnp.where(hots[kk], pos_e, 0.0), axis=1, keepdims=True).astype(I32)
        mi = jnp.where(lane == kk, idxs[kk], mi)
        mi = jnp.where(lane == TOP_K + kk, spos, mi)
        mf = jnp.where(lane == kk, es[kk] / den, mf)
    mi_ref[...] = mi
    mf_ref[...] = mf
    rowi = lax.broadcasted_iota(I32, (8, LANES), 0)
    info = jnp.where(rowi == 0, cnt8, jnp.where(rowi == 1, seg, jnp.where(rowi == 2, carry_sc[...], 0.0)))
    blk_ref[0] = info.astype(I32)
    carry_sc[...] = carry_sc[...] + cnt8
    tot_ref[...] = carry_sc[...]


def _route(x, nw, mod, wr, br, lay):
    T, D = x.shape
    B, S = lay["B"], lay["S"]
    mrow = lambda i: jnp.where(i < (B * S) // TMR, i // (S // TMR), B)
    row = lambda n: pl.BlockSpec((TMR, n), lambda i: (i, 0))
    return pl.pallas_call(
        _route_kernel,
        grid=(T // TMR,),
        in_specs=[row(D),
                  pl.BlockSpec((1, D), lambda i: (0, 0)),
                  pl.BlockSpec((1, 1, D), lambda i: (mrow(i), 0, 3)),
                  pl.BlockSpec((1, 1, D), lambda i: (mrow(i), 0, 4)),
                  pl.BlockSpec((D, LANES), lambda i: (0, 0)),
                  pl.BlockSpec((1, LANES), lambda i: (0, 0))],
        out_specs=[row(D), row(LANES), row(LANES),
                   pl.BlockSpec((1, 8, LANES), lambda i: (i, 0, 0)),
                   pl.BlockSpec((1, LANES), lambda i: (0, 0))],
        out_shape=[jax.ShapeDtypeStruct((T, D), F32), jax.ShapeDtypeStruct((T, LANES), I32),
                   jax.ShapeDtypeStruct((T, LANES), F32),
                   jax.ShapeDtypeStruct((T // TMR, 8, LANES), I32),
                   jax.ShapeDtypeStruct((1, LANES), F32)],
        scratch_shapes=[pltpu.VMEM((1, LANES), F32)],
        compiler_params=_params("arbitrary"),
        name="route",
    )(x, nw, mod, mod, wr, br)


def _segment_starts(info_ref, base, make_copy):
    def start_expert(e, carry):
        n = info_ref[base + e] // SEG_ALIGN
        src0 = info_ref[base + N_EXPERTS + e]
        dst0 = info_ref[base + 2 * N_EXPERTS + e]

        def start_chunk(c, carry):
            make_copy(pl.multiple_of(src0 + c * SEG_ALIGN, SEG_ALIGN),
                      pl.multiple_of(dst0 + c * SEG_ALIGN, SEG_ALIGN)).start()
            return carry

        lax.fori_loop(0, n, start_chunk, 0)
        return carry

    lax.fori_loop(0, N_EXPERTS, start_expert, 0)


def _segment_waits(info_ref, base, make_copy):
    total = lax.fori_loop(0, N_EXPERTS, lambda e, n: n + info_ref[base + e] // SEG_ALIGN, 0)

    def wait_chunk(c, carry):
        make_copy(0, 0).wait()
        return carry

    lax.fori_loop(0, total, wait_chunk, 0)


def _segment_copies(info_ref, base, make_copy):
    _segment_starts(info_ref, base, make_copy)
    _segment_waits(info_ref, base, make_copy)


def _dispatch_kernel(info_ref, prev_ref, h_ref, mi_ref, xb_ref, stg_sc, sems):
    tm = h_ref.shape[0]
    i = pl.program_id(0)
    slot = i % 2
    rows = stg_sc.shape[1]

    def seg_copy(sl):
        return lambda s, d: pltpu.make_async_copy(
            stg_sc.at[sl, pl.ds(s, SEG_ALIGN)], xb_ref.at[pl.ds(d, SEG_ALIGN)], sems.at[sl])

    @pl.when(i == 0)
    def _():
        sem = sems.at[0]
        stg_sc[0, 0:MOE_BLK, :] = jnp.zeros((MOE_BLK, stg_sc.shape[2]), F32)
        _segment_copies(info_ref, INFO_TAIL, seg_copy(0))

        def spare_copy(j):
            return pltpu.make_async_copy(stg_sc.at[0, pl.ds(0, MOE_BLK)],
                                         xb_ref.at[pl.ds(pl.multiple_of(j * MOE_BLK, MOE_BLK), MOE_BLK)], sem)

        def start_spare(j, carry):
            spare_copy(j).start()
            return carry

        def wait_spare(j, carry):
            spare_copy(j).wait()
            return carry

        first_spare = info_ref[INFO_NUSED]
        lax.fori_loop(first_spare, xb_ref.shape[0] // MOE_BLK, start_spare, 0)
        lax.fori_loop(first_spare, xb_ref.shape[0] // MOE_BLK, wait_spare, 0)

    spos_t = mi_ref[...].astype(F32).T
    r = lax.broadcasted_iota(I32, (rows, tm), 0).astype(F32)
    pm = r == spos_t[TOP_K:TOP_K + 1]
    for kk in range(1, TOP_K):
        pm = pm | (r == spos_t[TOP_K + kk:TOP_K + kk + 1])
    staged = _dot(pm.astype(BF16), h_ref[...].astype(BF16))

    def emit(sl):
        stg_sc[sl] = staged
        _segment_starts(info_ref, 0, seg_copy(sl))

        @pl.when(i > 0)
        def _():
            _segment_waits(prev_ref, 0, seg_copy(1 - sl))

        @pl.when(i == pl.num_programs(0) - 1)
        def _():
            _segment_waits(info_ref, 0, seg_copy(sl))

    pl.when(slot == 0)(lambda: emit(0))
    pl.when(slot == 1)(lambda: emit(1))


def _dispatch(info, h, mi, rows):
    T, D = h.shape
    return pl.pallas_call(
        _dispatch_kernel,
        grid=(T // TMR,),
        in_specs=[pl.BlockSpec((INFO_LANES,), lambda i: (i,), memory_space=pltpu.SMEM),
                  pl.BlockSpec((INFO_LANES,), lambda i: (jnp.maximum(i - 1, 0),), memory_space=pltpu.SMEM),
                  pl.BlockSpec((TMR, D), lambda i: (i, 0)),
                  pl.BlockSpec((TMR, LANES), lambda i: (i, 0))],
        out_specs=pl.BlockSpec(memory_space=pl.ANY),
        out_shape=jax.ShapeDtypeStruct((rows, D), F32),
        scratch_shapes=[pltpu.VMEM((2, STG_ROWS, D), F32), pltpu.SemaphoreType.DMA((2,))],
        compiler_params=_params("arbitrary"),
        name="dispatch",
    )(info, info, h, mi)


def _gmm_kernel(be_ref, nu_ref, nxt_ref, slot_ref, x_ref, wgu_hbm, bgu_ref, wdn_hbm, bdn_ref, o_ref,
                gu_buf, dn_buf, wgu_sc, wdn_sc, sems, *, layer):
    i = pl.program_id(0)
    e = be_ref[i]
    prev = be_ref[jnp.maximum(i - 1, 0)]

    def fetch(expert, slot):
        return (pltpu.make_async_copy(wgu_hbm.at[layer, expert], gu_buf.at[slot], sems.at[0, slot]),
                pltpu.make_async_copy(wdn_hbm.at[layer, expert], dn_buf.at[slot], sems.at[1, slot]))

    @pl.when(i == 0)
    def _():
        for cp in fetch(e, slot_ref[0]):
            cp.start()

    @pl.when(((i == 0) | (e != prev)) & (i < nu_ref[0]))
    def _():
        slot = slot_ref[i]
        for cp in fetch(e, slot):
            cp.wait()
        wgu_sc[...] = gu_buf[slot].astype(BF16)
        wdn_sc[...] = dn_buf[slot].astype(BF16)

        @pl.when(nxt_ref[i] >= 0)
        def _():
            for cp in fetch(nxt_ref[i], 1 - slot):
                cp.start()

    @pl.when(i < nu_ref[0])
    def _():
        xb = x_ref[...].astype(BF16)
        gu = _dot(xb, wgu_sc[...]) + bgu_ref[0, 0]
        glu = jnp.minimum(gu[:, 0:D_EXPERT], SWIGLU_LIMIT)
        lin = jnp.clip(gu[:, D_EXPERT:2 * D_EXPERT], -SWIGLU_LIMIT, SWIGLU_LIMIT)
        act = glu * _sigmoid(SWIGLU_ALPHA * glu) * (lin + 1.0)
        o_ref[...] = _dot(act.astype(BF16), wdn_sc[...]) + bdn_ref[0, 0]

    @pl.when(i >= nu_ref[0])
    def _():
        o_ref[...] = jnp.zeros(o_ref.shape, F32)


def _gmm(block_e, n_used, nxt, slot, xb, wgu, bgu, wdn, bdn, l):
    P, D = xb.shape
    L, E, _, F2 = wgu.shape
    xmap = lambda i, be, nu, nx, sl: (jnp.minimum(i, nu[0] - 1), 0)
    bmap = lambda i, be, nu, nx, sl: (l, be[i], 0, 0)
    return pl.pallas_call(
        functools.partial(_gmm_kernel, layer=l),
        grid_spec=pltpu.PrefetchScalarGridSpec(
            num_scalar_prefetch=4,
            grid=(P // MOE_BLK,),
            in_specs=[pl.BlockSpec((MOE_BLK, D), xmap),
                      pl.BlockSpec(memory_space=pl.ANY),
                      pl.BlockSpec((1, 1, 1, F2), bmap),
                      pl.BlockSpec(memory_space=pl.ANY),
                      pl.BlockSpec((1, 1, 1, D), bmap)],
            out_specs=pl.BlockSpec((MOE_BLK, D), lambda i, be, nu, nx, sl: (i, 0)),
            scratch_shapes=[pltpu.VMEM((2, D, F2), F32), pltpu.VMEM((2, F2 // 2, D), F32),
                            pltpu.VMEM((D, F2), BF16), pltpu.VMEM((F2 // 2, D), BF16),
                            pltpu.SemaphoreType.DMA((2, 2))]),
        out_shape=jax.ShapeDtypeStruct((P, D), F32),
        compiler_params=_params("arbitrary"),
        name="gmm",
    )(block_e, n_used, nxt, slot, xb, wgu, bgu.reshape(L, E, 1, F2), wdn, bdn.reshape(L, E, 1, D))


def _combine_kernel(info_ref, next_ref, x_ref, mi_ref, gate_ref, g2_ref, fw_ref, yb_ref, o_ref, stg_sc, sems,
                    *, final):
    tm = x_ref.shape[0]
    i = pl.program_id(0)
    slot = i % 2
    rows = stg_sc.shape[1]

    def seg_copy(sl):
        return lambda s, d: pltpu.make_async_copy(
            yb_ref.at[pl.ds(d, SEG_ALIGN)], stg_sc.at[sl, pl.ds(s, SEG_ALIGN)], sems.at[sl])

    @pl.when(i == 0)
    def _():
        stg_sc[...] = jnp.zeros(stg_sc.shape, F32)
        _segment_starts(info_ref, 0, seg_copy(0))

    spos = mi_ref[...].astype(F32)
    gate = gate_ref[...]
    r = lax.broadcasted_iota(I32, (tm, rows), 1).astype(F32)
    g = jnp.where(r == spos[:, TOP_K:TOP_K + 1], gate[:, 0:1], 0.0)
    for kk in range(1, TOP_K):
        g = g + jnp.where(r == spos[:, TOP_K + kk:TOP_K + kk + 1], gate[:, kk:kk + 1], 0.0)
    g_hi, g_lo = _split2(g)

    def use(sl):
        @pl.when(i < pl.num_programs(0) - 1)
        def _():
            _segment_starts(next_ref, 0, seg_copy(1 - sl))

        _segment_waits(info_ref, 0, seg_copy(sl))
        y = stg_sc[sl].astype(BF16)
        out = x_ref[...] + g2_ref[0] * (_dot(g_hi, y) + _dot(g_lo, y))
        if final:
            ms = jnp.mean(out * out, axis=-1, keepdims=True)
            out = out * lax.rsqrt(ms + EPS) * fw_ref[...]
        o_ref[...] = out

    pl.when(slot == 0)(lambda: use(0))
    pl.when(slot == 1)(lambda: use(1))


def _combine(info, x, mi, gates, mod, fw, yb, lay, final):
    T, D = x.shape
    B, S = lay["B"], lay["S"]
    mrow = lambda i: jnp.where(i < (B * S) // TMR, i // (S // TMR), B)
    return pl.pallas_call(
        functools.partial(_combine_kernel, final=final),
        grid=(T // TMR,),
        in_specs=[pl.BlockSpec((INFO_LANES,), lambda i: (i,), memory_space=pltpu.SMEM),
                  pl.BlockSpec((INFO_LANES,), lambda i: (jnp.minimum(i + 1, T // TMR - 1),),
                               memory_space=pltpu.SMEM),
                  pl.BlockSpec((TMR, D), lambda i: (i, 0)),
                  pl.BlockSpec((TMR, LANES), lambda i: (i, 0)),
                  pl.BlockSpec((TMR, LANES), lambda i: (i, 0)),
                  pl.BlockSpec((1, 1, D), lambda i: (mrow(i), 0, 5)),
                  pl.BlockSpec((1, D), lambda i: (0, 0)),
                  pl.BlockSpec(memory_space=pl.ANY)],
        out_specs=pl.BlockSpec((TMR, D), lambda i: (i, 0)),
        out_shape=jax.ShapeDtypeStruct((T, D), F32),
        scratch_shapes=[pltpu.VMEM((2, STG_ROWS, D), F32), pltpu.SemaphoreType.DMA((2,))],
        compiler_params=_params("arbitrary"),
        name="combine",
    )(info, info, x, mi, gates, mod, fw, yb)


def _moe(x, nw, mod, wr, br, wgu, bgu, wdn, bdn, fw, lay, l, final):
    T, D = x.shape
    ntb = T // TMR
    h, mi, mf, blk, tot = _route(x, nw, mod, wr, br, lay)
    counts = tot[0, :N_EXPERTS].astype(I32)
    padded = (counts + MOE_BLK - 1) // MOE_BLK * MOE_BLK
    pad_ends = jnp.cumsum(padded)
    pad_starts = pad_ends - padded
    nblk = -(-(T * TOP_K + ntb * N_EXPERTS * (SEG_ALIGN - 1)) // MOE_BLK) + N_EXPERTS
    starts = jnp.arange(nblk, dtype=I32) * MOE_BLK
    block_e = jnp.minimum(jnp.sum((pad_ends[None, :] <= starts[:, None]).astype(I32), axis=1), N_EXPERTS - 1)
    n_used = (pad_ends[N_EXPERTS - 1:] // MOE_BLK).astype(I32)
    rep = lambda v: jnp.broadcast_to(v[None, :], (ntb, v.shape[0]))
    zeros_e = jnp.zeros((N_EXPERTS,), I32)
    info = jnp.concatenate([
        blk[:, 0, :N_EXPERTS], blk[:, 1, :N_EXPERTS], blk[:, 2, :N_EXPERTS] + pad_starts[None, :], rep(zeros_e),
        rep(padded - counts), rep(zeros_e), rep(pad_starts + counts),
        rep(jnp.concatenate([n_used, jnp.zeros((INFO_LANES - INFO_NUSED - 1,), I32)]))], axis=1).reshape(-1)
    eid = jnp.arange(N_EXPERTS, dtype=I32)
    has = padded > 0
    later = jnp.where(has[None, :] & (eid[None, :] > eid[:, None]), eid[None, :], N_EXPERTS)
    nxt_e = jnp.min(later, axis=1)
    nxt_e = jnp.where(nxt_e < N_EXPERTS, nxt_e, -1)
    slot_e = (jnp.cumsum(has.astype(I32)) - 1) % 2
    xb = _dispatch(info, h, mi, nblk * MOE_BLK)
    yb = _gmm(block_e, n_used, jnp.take(nxt_e, block_e), jnp.take(slot_e, block_e), xb, wgu, bgu, wdn, bdn, l)
    return _combine(info, x, mi, mf, mod, fw, yb, lay, final)


def _rope_tables(S):
    t = np.arange(S)
    row = (t // GRID_W).astype(np.float64)
    col = (t % GRID_W).astype(np.float64)
    lane = np.arange(LANES)
    j = lane % 16
    inv = ROPE_THETA ** (-(j.astype(np.float32)) / np.float32(16.0))
    pos = np.where((lane % 64) < 32, row[:, None], col[:, None]).astype(np.float32)
    ang = (pos * inv.astype(np.float32)[None, :]).astype(np.float32)
    sign = np.where((lane % 32) < 16, -1.0, 1.0).astype(np.float32)
    cos_t = np.concatenate([np.cos(ang), np.ones((TM, LANES), np.float32)], axis=0)
    sin_t = np.concatenate([np.sin(ang) * sign[None, :], np.zeros((TM, LANES), np.float32)], axis=0)
    return cos_t.astype(np.float32), sin_t.astype(np.float32)


def kernel(x, c, ctx, c_ctx, w_ada, b_ada, norm1_w, w_in, da_lambda, da_subln_w, ssd_conv_w, ssd_conv_b,
           ssd_a_log, ssd_dt_bias, ssd_d, ssd_norm_w, gla_gk_up, gla_gk_b, gla_norm_w, w_out, norm2_w,
           w_router, b_router, w_gate_up, b_gate_up, w_down, b_down, final_norm_w):
    B, S, D = x.shape
    Lc = ctx.shape[1]
    depth = w_ada.shape[0]
    assert S % TMR == 0 and Lc % TM == 0 and (B * Lc) % TMR == 0 and S % GRID_W == 0
    lay = dict(B=B, S=S, Lc=Lc, nlb=(B * S) // TM, bpb=S // TM, cpb=Lc // TM)
    n_lat = B * S

    xs = jnp.concatenate([x.reshape(B * S, D), ctx.reshape(B * Lc, D)], axis=0)

    cc = jnp.zeros((8, D), F32).at[0:B].set(c).at[B].set(c_ctx)
    mod_all = _ada(cc, w_ada, b_ada)

    cos_t, sin_t = (jnp.asarray(t) for t in _rope_tables(S))

    for l in range(depth):
        last = l == depth - 1
        lam_init = 0.8 - 0.6 * math.exp(-0.3 * l)
        mod = mod_all[l, 0:B + 1].reshape(B + 1, 1, 6 * D)

        wi = w_in[l]
        misc_w = jnp.zeros((D, LANES), F32)
        misc_w = misc_w.at[:, MISC_DT:MISC_DT + 8].set(wi[:, 2304:2312])
        misc_w = misc_w.at[:, MISC_CODE:MISC_CODE + 2 * GLA_RANK].set(wi[:, 3080:3112])
        w_re = jnp.concatenate([wi[:, 0:2304], wi[:, 2312:3080], misc_w], axis=1).astype(BF16)
        a128 = jnp.zeros((1, LANES), F32).at[0, 0:8].set(-jnp.exp(ssd_a_log[l].astype(F32)).reshape(-1))
        bias128 = jnp.zeros((1, LANES), F32).at[0, 0:8].set(ssd_dt_bias[l].astype(F32).reshape(-1))
        wgk = jnp.zeros((2, LANES, GLA_KEY_WIDTH), F32)
        for d in range(2):
            wgk = wgk.at[d, MISC_CODE + d * GLA_RANK:MISC_CODE + (d + 1) * GLA_RANK, :].set(gla_gk_up[l, d])
        bgk = gla_gk_b[l].reshape(2, 1, GLA_KEY_WIDTH)
        dsk = jnp.repeat(ssd_d[l], SSD_HEAD_DIM).reshape(1, SSD_WIDTH)
        snw = ssd_norm_w[l].reshape(1, SSD_WIDTH)
        gnw = jnp.tile(gla_norm_w[l], GLA_HEADS).reshape(1, GLA_WIDTH)
        wr = jnp.zeros((D, LANES), F32).at[:, 0:N_EXPERTS].set(w_router[l])
        br = jnp.zeros((1, LANES), F32).at[0, 0:N_EXPERTS].set(b_router[l])

        q, k, v, z, xbc, gla_in, misc = _inproj(xs, norm1_w[l].reshape(1, D), mod, cos_t, sin_t, w_re, lay)

        a = _attention(q, k, v, da_lambda[l], da_subln_w[l].reshape(1, DA_V), lay, lam_init,
                       ctx_queries=not last)
        xbc_act = _conv(xbc, ssd_conv_w[l], ssd_conv_b[l].reshape(1, SSD_CONV_CH), lay)
        yf, yb = _ssd(xbc_act, misc, a128, bias128, lay)
        of, ob = _gla(gla_in, misc, wgk, bgk, lay)

        nrows = n_lat if last else xs.shape[0]
        xs = _outproj(xs, a, yf, yb, xbc_act, z, of, ob, gla_in, dsk, snw, gnw,
                      w_out[l].astype(BF16), mod, lay, nrows)
        xs = _moe(xs, norm2_w[l].reshape(1, D), mod, wr, br, w_gate_up, b_gate_up, w_down, b_down,
                  final_norm_w.reshape(1, D), lay, l, last)

    return xs.reshape(B, S, D)
```

```python
import functools
import math

import numpy as np
import jax
import jax.numpy as jnp
from jax import lax
from jax.experimental import pallas as pl
from jax.experimental.pallas import tpu as pltpu

F32 = jnp.float32
BF16 = jnp.bfloat16
I32 = jnp.int32

GRID_W = 64
EPS = 1e-6
DA_HEADS = 4
DA_QK = 64
DA_V = 128
DA_WIDTH = 512
ROPE_THETA = 10000.0
SSD_HEADS = 4
SSD_HEAD_DIM = 64
SSD_WIDTH = 256
SSD_STATE = 64
SSD_CONV_CH = 512
GLA_HEADS = 4
GLA_DK = 32
GLA_DV = 64
GLA_KEY_WIDTH = 128
GLA_WIDTH = 256
GLA_RANK = 16
GLA_NORMALIZER = 16.0
N_EXPERTS = 32
TOP_K = 4
D_EXPERT = 1024
SWIGLU_LIMIT = 7.0
SWIGLU_ALPHA = 1.702

LANES = 128
TM = 256
SSD_Q = 128
GLA_Q = 64
GLA_SUB = 4
ATT_TQ = 256
ATT_KT = (2816, 768, 512, 256)
LOG2E = 1.4426950408889634
MOE_BLK = 256
TMR = 512
SEG_ALIGN = 8
STG_ROWS = TMR * TOP_K + N_EXPERTS * SEG_ALIGN
INFO_LANES = 256
INFO_TAIL = 128
INFO_NUSED = 224
VMEM_LIMIT = 56 * 1024 * 1024

C_Q, C_K, C_V, C_Z, C_XBC, C_GLA, C_MISC, C_END = 0, 512, 1024, 1536, 1792, 2304, 3072, 3200
MISC_DT = 0
MISC_CODE = 8


def _sigmoid(x):
    return 1.0 / (1.0 + jnp.exp(-x))


def _softplus(x):
    return jnp.maximum(x, 0.0) + jnp.log(1.0 + jnp.exp(-jnp.abs(x)))


def _split2(a):
    hi = a.astype(BF16)
    lo = (a - hi.astype(F32)).astype(BF16)
    return hi, lo


def _split3(a):
    a1 = a.astype(BF16)
    r1 = a - a1.astype(F32)
    a2 = r1.astype(BF16)
    a3 = (r1 - a2.astype(F32)).astype(BF16)
    return a1, a2, a3


def _dot(a, b):
    return jnp.dot(a, b, preferred_element_type=F32)


def _dot_hi(a, b):
    a1, a2 = _split2(a)
    b1, b2 = _split2(b)
    return _dot(a1, b1) + (_dot(a1, b2) + _dot(a2, b1))


def _dot_exact_lhs(m, a):
    a1, a2, a3 = _split3(a)
    mb = m.astype(BF16)
    return _dot(mb, a1) + (_dot(mb, a2) + _dot(mb, a3))


def _dot_exact_rhs(a, m):
    a1, a2, a3 = _split3(a)
    mb = m.astype(BF16)
    return _dot(a1, mb) + (_dot(a2, mb) + _dot(a3, mb))


def _params(*sem):
    return pltpu.CompilerParams(dimension_semantics=sem, vmem_limit_bytes=VMEM_LIMIT)


def _ada_kernel(c_ref, w_ref, b_ref, o_ref):
    c = c_ref[...]
    s = c * _sigmoid(c)
    o_ref[0] = _dot_hi(s, w_ref[0]) + b_ref[0]


def _ada(cc, w_ada, b_ada):
    L, D, N = w_ada.shape
    tn = 512
    return pl.pallas_call(
        _ada_kernel,
        grid=(L, N // tn),
        in_specs=[pl.BlockSpec((8, D), lambda l, j: (0, 0)),
                  pl.BlockSpec((1, D, tn), lambda l, j: (l, 0, j)),
                  pl.BlockSpec((1, 1, tn), lambda l, j: (l, 0, j))],
        out_specs=pl.BlockSpec((1, 8, tn), lambda l, j: (l, 0, j)),
        out_shape=jax.ShapeDtypeStruct((L, 8, N), F32),
        compiler_params=_params("parallel", "parallel"),
        name="ada",
    )(cc, w_ada, b_ada.reshape(L, 1, N))


def _inproj_kernel(x_ref, xp_ref, xn_ref, nw_ref, sh_ref, sc_ref, cos_ref, sin_ref, w_ref, cw_ref, cb_ref,
                   q_ref, k_ref, v_ref, z_ref, xbc_ref, gla_ref, misc_ref, *, nlb, bpb, cpb):
    def modulated(xf):
        ms = jnp.mean(xf * xf, axis=-1, keepdims=True)
        return ((xf * lax.rsqrt(ms + EPS) * nw_ref[...]) * (1.0 + sc_ref[0]) + sh_ref[0]).astype(BF16)

    hb = modulated(x_ref[...])
    cos = cos_ref[...]
    sin = sin_ref[...]
    lane = lax.broadcasted_iota(I32, cos.shape, 1)
    first = (lane % 32) < 16

    def rope(p):
        outs = []
        for hd in range(DA_HEADS):
            ph = p[:, hd * LANES:(hd + 1) * LANES]
            sw = jnp.where(first, pltpu.roll(ph, LANES - 16, 1), pltpu.roll(ph, 16, 1))
            outs.append(ph * cos + sw * sin)
        return jnp.concatenate(outs, axis=1)

    q = rope(_dot(hb, w_ref[:, C_Q:C_K])) * (DA_QK ** -0.5 * LOG2E)
    q_ref[...] = q.astype(BF16)
    k_ref[...] = rope(_dot(hb, w_ref[:, C_K:C_V])).astype(BF16)
    v_ref[...] = _dot(hb, w_ref[:, C_V:C_Z]).astype(BF16)
    z_ref[...] = _dot(hb, w_ref[:, C_Z:C_XBC])
    gla_ref[...] = _dot(hb, w_ref[:, C_GLA:C_MISC])
    misc_ref[...] = _dot(hb, w_ref[:, C_MISC:C_END])

    xbc = _dot(hb, w_ref[:, C_XBC:C_GLA])
    halo = _dot(modulated(jnp.concatenate([xp_ref[...], xn_ref[...]], axis=0)), w_ref[:, C_XBC:C_GLA])
    i = pl.program_id(0)
    pos = jnp.where(i < nlb, i % bpb, (i - nlb) % cpb)
    last = jnp.where(i < nlb, bpb - 1, cpb - 1)
    prev_row = halo[7:8, :] * (pos != 0).astype(F32)
    next_row = halo[8:9, :] * (pos != last).astype(F32)
    ridx = lax.broadcasted_iota(I32, xbc.shape, 0)
    xm1 = jnp.where(ridx == 0, prev_row, pltpu.roll(xbc, 1, 0))
    xp1 = jnp.where(ridx == xbc.shape[0] - 1, next_row, pltpu.roll(xbc, xbc.shape[0] - 1, 0))
    cw = cw_ref[...]
    y = cw[0:1] * xm1 + cw[1:2] * xbc + cw[2:3] * xp1 + cb_ref[...]
    xbc_ref[...] = y * _sigmoid(y)


def _inproj(x, nw, mod, cos_t, sin_t, w, cw, cb, lay):
    T, D = x.shape
    nlb, bpb, B = lay["nlb"], lay["bpb"], lay["B"]
    r8 = TM // 8
    mrow = lambda i: jnp.where(i < nlb, i // bpb, B)
    prow = lambda i: jnp.where(i < nlb, i % bpb, bpb)
    row = lambda n: pl.BlockSpec((TM, n), lambda i: (i, 0))
    spb = bpb + lay["cpb"]
    kvrow = lambda i: (jnp.where(i < nlb, (i // bpb) * spb + lay["cpb"] + i % bpb,
                                 ((i - nlb) // lay["cpb"]) * spb + (i - nlb) % lay["cpb"]), 0)
    kv = pl.BlockSpec((TM, DA_WIDTH), kvrow)
    outs = [(512, BF16), (512, BF16), (512, BF16), (256, F32), (512, F32), (768, F32), (128, F32)]
    out_specs = [row(n) for n, _ in outs]
    out_specs[1] = kv
    out_specs[2] = kv
    return pl.pallas_call(
        functools.partial(_inproj_kernel, nlb=nlb, bpb=bpb, cpb=lay["cpb"]),
        grid=(T // TM,),
        in_specs=[row(D),
                  pl.BlockSpec((8, D), lambda i: (jnp.maximum(i * r8 - 1, 0), 0)),
                  pl.BlockSpec((8, D), lambda i: (jnp.minimum((i + 1) * r8, T // 8 - 1), 0)),
                  pl.BlockSpec((1, D), lambda i: (0, 0)),
                  pl.BlockSpec((1, 1, D), lambda i: (mrow(i), 0, 0)),
                  pl.BlockSpec((1, 1, D), lambda i: (mrow(i), 0, 1)),
                  pl.BlockSpec((TM, LANES), lambda i: (prow(i), 0)),
                  pl.BlockSpec((TM, LANES), lambda i: (prow(i), 0)),
                  pl.BlockSpec((D, C_END), lambda i: (0, 0)),
                  pl.BlockSpec((3, SSD_CONV_CH), lambda i: (0, 0)),
                  pl.BlockSpec((1, SSD_CONV_CH), lambda i: (0, 0))],
        out_specs=out_specs,
        out_shape=[jax.ShapeDtypeStruct((T, n), dt) for n, dt in outs],
        compiler_params=_params("parallel"),
        name="inproj",
    )(x, x, x, nw, mod, mod, cos_t, sin_t, w, cw, cb)


def _lane_fold(x, op):
    f = x[:, 0:LANES]
    for i in range(1, x.shape[1] // LANES):
        f = op(f, x[:, i * LANES:(i + 1) * LANES])
    return f


def _attn_stack_maps(q_ref, q2_sc):
    tq = q_ref.shape[0]
    q = q_ref[...]
    lane = lax.broadcasted_iota(I32, q.shape, 1)
    zero = jnp.zeros_like(q)
    q2_sc[0:tq, :] = jnp.where(lane < DA_QK, q, zero)
    q2_sc[tq:2 * tq, :] = jnp.where(lane >= DA_QK, q, zero)


def _attn_scores_tile(q2_sc, k_ref, s_sc, slot, mf_sc, j, kt):
    kj = k_ref[pl.ds(pl.multiple_of(j * kt, kt), kt), :]
    s = lax.dot_general(q2_sc[...], kj, (((1,), (1,)), ((), ())), preferred_element_type=F32)
    s_sc[slot, j, :, 0:kt] = s
    mf_sc[...] = jnp.maximum(mf_sc[...], _lane_fold(s, jnp.maximum))


def _attn_weigh_tile(v_ref, s_sc, slot, p_sc, l_sc, acc_sc, m, j, kt):
    sub = next(w for w in (512, 256) if kt % w == 0)
    for c0 in range(0, kt, sub):
        p = jnp.exp2(s_sc[slot, j, :, c0:c0 + sub] - m)
        l_sc[...] += _lane_fold(p, jnp.add)
        p_sc[:, c0:c0 + sub] = p.astype(BF16)
    vj = v_ref[pl.ds(pl.multiple_of(j * kt, kt), kt), :]
    acc_sc[...] += _dot(p_sc[:, 0:kt], vj)


def _attn_finish(lam_ref, w_ref, o_ref, l_sc, acc_sc, lam_init):
    tq = o_ref.shape[0]
    lp = lam_ref[...]
    lam = (jnp.exp(jnp.sum(lp[0:1] * lp[1:2], axis=1, keepdims=True))
           - jnp.exp(jnp.sum(lp[2:3] * lp[3:4], axis=1, keepdims=True)) + lam_init)
    acc = acc_sc[...]
    l = jnp.sum(l_sc[...], axis=1, keepdims=True)
    o = acc[0:tq] / l[0:tq] - lam * (acc[tq:2 * tq] / l[tq:2 * tq])
    ms = jnp.mean(o * o, axis=-1, keepdims=True)
    o_ref[...] = (o * lax.rsqrt(ms + EPS) * w_ref[...]) * (1.0 - lam_init)


def _attn_kernel(q_ref, k_ref, v_ref, lam_ref, w_ref, o_ref, q2_sc, s_sc, p_sc, mf_sc, l_sc, acc_sc,
                 *, nt, kt, lam_init):
    _attn_stack_maps(q_ref, q2_sc)
    mf_sc[...] = jnp.full(mf_sc.shape, -jnp.inf, F32)
    l_sc[...] = jnp.zeros(l_sc.shape, F32)
    acc_sc[...] = jnp.zeros(acc_sc.shape, F32)

    def scores(j, carry):
        _attn_scores_tile(q2_sc, k_ref, s_sc, 0, mf_sc, j, kt)
        return carry

    lax.fori_loop(0, nt, scores, 0)
    m = jnp.max(mf_sc[...], axis=1, keepdims=True)

    def weigh(j, carry):
        _attn_weigh_tile(v_ref, s_sc, 0, p_sc, l_sc, acc_sc, m, j, kt)
        return carry

    lax.fori_loop(0, nt, weigh, 0)
    _attn_finish(lam_ref, w_ref, o_ref, l_sc, acc_sc, lam_init)


def _attn_pipe_kernel(q_ref, k_ref, v_ref, lam_ref, w_ref, o_ref, q2_sc, s_sc, p_sc, mf_sc, m_sc, l_sc, acc_sc,
                      *, nq, nt, kt, lam_init):
    t = pl.program_id(2)

    @pl.when(t < nq)
    def _():
        _attn_stack_maps(q_ref, q2_sc)
        mf_sc[...] = jnp.full(mf_sc.shape, -jnp.inf, F32)

    @pl.when(t > 0)
    def _():
        l_sc[...] = jnp.zeros(l_sc.shape, F32)
        acc_sc[...] = jnp.zeros(acc_sc.shape, F32)

    def run(slot_a, slot_b):
        m_prev = None if slot_b is None else m_sc[slot_b]

        def body(j, carry):
            if slot_b is not None:
                _attn_weigh_tile(v_ref, s_sc, slot_b, p_sc, l_sc, acc_sc, m_prev, j, kt)
            if slot_a is not None:
                _attn_scores_tile(q2_sc, k_ref, s_sc, slot_a, mf_sc, j, kt)
            return carry

        lax.fori_loop(0, nt, body, 0)
        if slot_a is not None:
            m_sc[slot_a] = jnp.max(mf_sc[...], axis=1, keepdims=True)

    inner = (t > 0) & (t < nq)
    pl.when(t == 0)(lambda: run(0, None))
    pl.when(inner & (t % 2 == 1))(lambda: run(1, 0))
    pl.when(inner & (t % 2 == 0))(lambda: run(0, 1))
    pl.when(t == nq)(lambda: run(None, (nq - 1) % 2))

    @pl.when(t > 0)
    def _():
        _attn_finish(lam_ref, w_ref, o_ref, l_sc, acc_sc, lam_init)


def _attention(q, k, v, lam_p, subln_w, lay, lam_init, *, ctx_queries):
    B, S, Lc = lay["B"], lay["S"], lay["Lc"]
    nkeys = Lc + S
    assert nkeys % Lc == 0
    small = [pl.BlockSpec((4, DA_QK), lambda b, h, t: (0, 0)), pl.BlockSpec((1, DA_V), lambda b, h, t: (0, 0))]
    stacked = lambda n, dt: pltpu.VMEM((2 * ATT_TQ, n), dt)

    kt = next(t for t in ATT_KT if nkeys % t == 0)
    nq = S // ATT_TQ
    kmap = lambda b, h, t: (b, h)
    a_lat = pl.pallas_call(
        functools.partial(_attn_pipe_kernel, nq=nq, nt=nkeys // kt, kt=kt, lam_init=lam_init),
        grid=(B, DA_HEADS, nq + 1),
        in_specs=[pl.BlockSpec((ATT_TQ, LANES), lambda b, h, t: (b * nq + jnp.minimum(t, nq - 1), h)),
                  pl.BlockSpec((nkeys, LANES), kmap),
                  pl.BlockSpec((nkeys, LANES), kmap)] + small,
        out_specs=pl.BlockSpec((ATT_TQ, LANES), lambda b, h, t: (b * nq + jnp.maximum(t - 1, 0), h)),
        out_shape=jax.ShapeDtypeStruct((B * S, DA_WIDTH), F32),
        scratch_shapes=[stacked(LANES, BF16),
                        pltpu.VMEM((2, nkeys // kt, 2 * ATT_TQ, kt), F32),
                        stacked(kt, BF16), stacked(LANES, F32),
                        pltpu.VMEM((2, 2 * ATT_TQ, 1), F32),
                        stacked(LANES, F32), stacked(LANES, F32)],
        compiler_params=_params("parallel", "parallel", "arbitrary"),
        name="attn",
    )(q, k, v, lam_p, subln_w)
    if not ctx_queries:
        return a_lat

    ktc = next(t for t in ATT_KT if Lc % t == 0)
    nqc = Lc // ATT_TQ
    cmap = lambda b, h, t: (b * (nkeys // Lc), h)
    a_ctx = pl.pallas_call(
        functools.partial(_attn_kernel, nt=Lc // ktc, kt=ktc, lam_init=lam_init),
        grid=(B, DA_HEADS, nqc),
        in_specs=[pl.BlockSpec((ATT_TQ, LANES), lambda b, h, t: ((B * S) // ATT_TQ + b * nqc + t, h)),
                  pl.BlockSpec((Lc, LANES), cmap),
                  pl.BlockSpec((Lc, LANES), cmap)] + small,
        out_specs=pl.BlockSpec((ATT_TQ, LANES), lambda b, h, t: (b * nqc + t, h)),
        out_shape=jax.ShapeDtypeStruct((B * Lc, DA_WIDTH), F32),
        scratch_shapes=[stacked(LANES, BF16),
                        pltpu.VMEM((1, Lc // ktc, 2 * ATT_TQ, ktc), F32),
                        stacked(ktc, BF16), stacked(LANES, F32), stacked(LANES, F32), stacked(LANES, F32)],
        compiler_params=_params("parallel", "parallel", "arbitrary"),
        name="attn_ctx",
    )(q, k, v, lam_p, subln_w)
    return jnp.concatenate([a_lat, a_ctx], axis=0)


def _ssd_dir(xbc, misc, a128, bias128, hs_ref, dirn, rev):
    Q = xbc.shape[0]
    dt = _softplus(misc + bias128)
    a = dt * a128
    ii = lax.broadcasted_iota(I32, (Q, Q), 0)
    jj = lax.broadcasted_iota(I32, (Q, Q), 1)
    M = (jj >= ii) if rev else (jj <= ii)
    acs = _dot_exact_lhs(M.astype(F32), a)
    acs_t = acs.T
    dt_t = dt.T
    xs = xbc[:, 0:SSD_WIDTH]
    Bt = xbc[:, SSD_WIDTH:SSD_WIDTH + 128].T
    Cm = xbc[:, SSD_WIDTH + 128:SSD_WIDTH + 256]
    last = 0 if rev else Q - 1
    ys = []
    for g in range(2):
        Cg = Cm[:, g * SSD_STATE:(g + 1) * SSD_STATE]
        Btg = Bt[g * SSD_STATE:(g + 1) * SSD_STATE, :]
        CB = _dot(Cg, Btg)
        for hh in range(2):
            h = 2 * g + hh
            li = dirn * SSD_HEADS + h
            acs_c = acs[:, li:li + 1]
            acs_r = acs_t[li:li + 1, :]
            dt_r = dt_t[li:li + 1, :]
            decay = jnp.exp(jnp.where(M, acs_c - acs_r, -jnp.inf))
            sc = CB * decay * dt_r
            x_h = xs[:, h * SSD_HEAD_DIM:(h + 1) * SSD_HEAD_DIM]
            hs = hs_ref[h]
            y = _dot(sc, x_h) + _dot(Cg * jnp.exp(acs_c), hs)
            tot = acs_r[:, last:last + 1]
            w_end = jnp.exp(tot - acs_r) * dt_r
            hs_ref[h] = jnp.exp(tot) * hs + _dot(Btg * w_end, x_h)
            ys.append(y)
    return jnp.concatenate(ys, axis=1)


def _ssd_kernel(xf_ref, mf_ref, xb_ref, mb_ref, a_ref, bias_ref, yf_ref, yb_ref, hf_sc, hb_sc):
    @pl.when(pl.program_id(1) == 0)
    def _():
        hf_sc[...] = jnp.zeros(hf_sc.shape, F32)
        hb_sc[...] = jnp.zeros(hb_sc.shape, F32)

    a128 = a_ref[...]
    bias128 = bias_ref[...]
    yf_ref[...] = _ssd_dir(xf_ref[...], mf_ref[...], a128, bias128, hf_sc, 0, False)
    yb_ref[...] = _ssd_dir(xb_ref[...], mb_ref[...], a128, bias128, hb_sc, 1, True)


def _scan_maps(lay, rows):
    ncc, ncl = lay["Lc"] // rows, lay["S"] // rows
    base = (lay["B"] * lay["S"]) // rows
    fmap = lambda b, c: (jnp.where(c < ncc, base + b * ncc + c, b * ncl + c - ncc), 0)
    bmap = lambda b, c: (jnp.where(c < ncc, base + b * ncc + (ncc - 1 - c),
                                   b * ncl + (ncl - 1 - (c - ncc))), 0)
    return ncc + ncl, fmap, bmap


def _ssd(xbc_act, misc, a128, bias128, lay):
    T = xbc_act.shape[0]
    nc, fmap, bmap = _scan_maps(lay, SSD_Q)
    vec = pl.BlockSpec((1, LANES), lambda b, c: (0, 0))
    return pl.pallas_call(
        _ssd_kernel,
        grid=(lay["B"], nc),
        in_specs=[pl.BlockSpec((SSD_Q, SSD_CONV_CH), fmap), pl.BlockSpec((SSD_Q, LANES), fmap),
                  pl.BlockSpec((SSD_Q, SSD_CONV_CH), bmap), pl.BlockSpec((SSD_Q, LANES), bmap),
                  vec, vec],
        out_specs=[pl.BlockSpec((SSD_Q, SSD_WIDTH), fmap), pl.BlockSpec((SSD_Q, SSD_WIDTH), bmap)],
        out_shape=[jax.ShapeDtypeStruct((T, SSD_WIDTH), F32)] * 2,
        scratch_shapes=[pltpu.VMEM((SSD_HEADS, SSD_STATE, SSD_HEAD_DIM), F32)] * 2,
        compiler_params=_params("parallel", "arbitrary"),
        name="ssd",
    )(xbc_act, misc, xbc_act, misc, a128, bias128)


def _gla_block(blk, misc, wgk, bgk, s_ref, rev):
    R = blk.shape[0]
    kw = GLA_KEY_WIDTH
    q, k, v = blk[:, 0:kw], blk[:, kw:2 * kw], blk[:, 2 * kw:2 * kw + GLA_WIDTH]
    pre = _dot_hi(misc, wgk) + bgk
    gk = (jnp.minimum(pre, 0.0) - jnp.log(1.0 + jnp.exp(-jnp.abs(pre)))) / GLA_NORMALIZER
    ii = lax.broadcasted_iota(I32, (R, R), 0)
    jj = lax.broadcasted_iota(I32, (R, R), 1)
    same = (ii // GLA_Q) == (jj // GLA_Q)
    M = same & ((jj >= ii) if rev else (jj <= ii))
    b = _dot_exact_lhs(M.astype(F32), gk)
    tot = _dot_exact_lhs(same.astype(F32), gk)
    q_t = (q * (GLA_DK ** -0.5)) * jnp.exp(b)
    k_t = k * jnp.exp(-b)
    k_end_t = (k * jnp.exp(tot - b)).T
    gk_t = gk.T
    lane_k = lax.broadcasted_iota(I32, (1, kw), 1) // GLA_DK
    lane_v = lax.broadcasted_iota(I32, (1, GLA_WIDTH), 1) // GLA_DV
    o = jnp.zeros((R, GLA_WIDTH), F32)
    for h in range(GLA_HEADS):
        qh = jnp.where(lane_k == h, q_t, 0.0)
        att = lax.dot_general(qh, k_t, (((1,), (1,)), ((), ())), preferred_element_type=F32)
        o = o + _dot(jnp.where(M, att, 0.0), jnp.where(lane_v == h, v, 0.0))
    col_chunk = lax.broadcasted_iota(I32, (1, R), 1) // GLA_Q
    S = s_ref[...]
    bd = (lax.broadcasted_iota(I32, S.shape, 0) // GLA_DK) == (lax.broadcasted_iota(I32, S.shape, 1) // GLA_DV)
    nsub = R // GLA_Q
    inter = [None] * nsub
    for c in (range(nsub - 1, -1, -1) if rev else range(nsub)):
        sel = col_chunk == c
        inter[c] = _dot(q_t[c * GLA_Q:(c + 1) * GLA_Q], S)
        decay = jnp.exp(jnp.sum(jnp.where(sel, gk_t, 0.0), axis=1, keepdims=True))
        S = jnp.where(bd, S * decay + _dot(jnp.where(sel, k_end_t, 0.0), v), 0.0)
    s_ref[...] = S
    return o + jnp.concatenate(inter, axis=0)


def _gla_kernel(gf_ref, mf_ref, gb_ref, mb_ref, wgk_ref, bgk_ref, of_ref, ob_ref, sf_sc, sb_sc):
    @pl.when(pl.program_id(1) == 0)
    def _():
        sf_sc[...] = jnp.zeros(sf_sc.shape, F32)
        sb_sc[...] = jnp.zeros(sb_sc.shape, F32)

    of_ref[...] = _gla_block(gf_ref[...], mf_ref[...], wgk_ref[0], bgk_ref[0], sf_sc, False)
    ob_ref[...] = _gla_block(gb_ref[...], mb_ref[...], wgk_ref[1], bgk_ref[1], sb_sc, True)


def _gla(gla_in, misc, wgk, bgk, lay):
    T = gla_in.shape[0]
    R = GLA_Q * GLA_SUB
    nc, fmap, bmap = _scan_maps(lay, R)
    return pl.pallas_call(
        _gla_kernel,
        grid=(lay["B"], nc),
        in_specs=[pl.BlockSpec((R, 512), fmap), pl.BlockSpec((R, LANES), fmap),
                  pl.BlockSpec((R, 512), bmap), pl.BlockSpec((R, LANES), bmap),
                  pl.BlockSpec((2, LANES, GLA_KEY_WIDTH), lambda b, c: (0, 0, 0)),
                  pl.BlockSpec((2, 1, GLA_KEY_WIDTH), lambda b, c: (0, 0, 0))],
        out_specs=[pl.BlockSpec((R, GLA_WIDTH), fmap), pl.BlockSpec((R, GLA_WIDTH), bmap)],
        out_shape=[jax.ShapeDtypeStruct((T, GLA_WIDTH), F32)] * 2,
        scratch_shapes=[pltpu.VMEM((GLA_KEY_WIDTH, GLA_WIDTH), F32)] * 2,
        compiler_params=_params("parallel", "arbitrary"),
        name="gla",
    )(gla_in, misc, gla_in, misc, wgk, bgk)


def _outproj_kernel(x_ref, a_ref, yf_ref, yb_ref, xs_ref, z_ref, of_ref, ob_ref, g_ref,
                    dsk_ref, snw_ref, gnw_ref, w_ref, g1_ref, o_ref):
    y = yf_ref[...] + yb_ref[...] + dsk_ref[...] * xs_ref[...]
    z = z_ref[...]
    gs = y * (z * _sigmoid(z))
    half = SSD_WIDTH // 2
    parts = []
    for grp in range(2):
        seg = gs[:, grp * half:(grp + 1) * half]
        parts.append(seg * lax.rsqrt(jnp.mean(seg * seg, axis=-1, keepdims=True) + EPS))
    s = jnp.concatenate(parts, axis=1) * snw_ref[...]
    o = of_ref[...] + ob_ref[...]
    bd = ((lax.broadcasted_iota(I32, (GLA_WIDTH, GLA_WIDTH), 0) // GLA_DV)
          == (lax.broadcasted_iota(I32, (GLA_WIDTH, GLA_WIDTH), 1) // GLA_DV))
    ms = _dot_exact_rhs(o * o, jnp.where(bd, 1.0 / GLA_DV, 0.0))
    g = g_ref[...]
    c = (o * lax.rsqrt(ms + EPS) * gnw_ref[...]) * (g * _sigmoid(g))
    mix = jnp.concatenate([a_ref[...], s, c], axis=1).astype(BF16)
    o_ref[...] = x_ref[...] + g1_ref[0] * _dot(mix, w_ref[...])


def _outproj(x, a, yf, yb, xbc_act, z, of, ob, gla_in, dsk, snw, gnw, w_out, mod, lay, nrows):
    D = x.shape[1]
    nlb, bpb, B = lay["nlb"], lay["bpb"], lay["B"]
    mrow = lambda i: jnp.where(i < nlb, i // bpb, B)
    row = lambda n, cb=0: pl.BlockSpec((TM, n), lambda i: (i, cb))
    vec = lambda n: pl.BlockSpec((1, n), lambda i: (0, 0))
    return pl.pallas_call(
        _outproj_kernel,
        grid=(nrows // TM,),
        in_specs=[row(D), row(DA_WIDTH), row(SSD_WIDTH), row(SSD_WIDTH), row(SSD_WIDTH), row(SSD_WIDTH),
                  row(GLA_WIDTH), row(GLA_WIDTH), row(GLA_WIDTH, 2),
                  vec(SSD_WIDTH), vec(SSD_WIDTH), vec(GLA_WIDTH),
                  pl.BlockSpec((D, D), lambda i: (0, 0)),
                  pl.BlockSpec((1, 1, D), lambda i: (mrow(i), 0, 2))],
        out_specs=row(D),
        out_shape=jax.ShapeDtypeStruct((nrows, D), F32),
        compiler_params=_params("parallel"),
        name="outproj",
    )(x, a, yf, yb, xbc_act, z, of, ob, gla_in, dsk, snw, gnw, w_out, mod)


def _route_kernel(x_ref, nw_ref, sh_ref, sc_ref, wr_ref, br_ref,
                  h_ref, mi_ref, mf_ref, blk_ref, tot_ref, carry_sc):
    i = pl.program_id(0)

    @pl.when(i == 0)
    def _():
        carry_sc[...] = jnp.zeros(carry_sc.shape, F32)

    xf = x_ref[...]
    ms = jnp.mean(xf * xf, axis=-1, keepdims=True)
    h = (xf * lax.rsqrt(ms + EPS) * nw_ref[...]) * (1.0 + sc_ref[0]) + sh_ref[0]
    h_ref[...] = h
    tm = h.shape[0]
    lane = lax.broadcasted_iota(I32, (tm, LANES), 1)
    logits = jnp.where(lane < N_EXPERTS, _dot_hi(h, wr_ref[...]) + br_ref[...], -jnp.inf)
    vals, idxs, hots = [], [], []
    l = logits
    for _ in range(TOP_K):
        m = jnp.max(l, axis=1, keepdims=True)
        idx = jnp.min(jnp.where(l == m, lane, LANES), axis=1, keepdims=True)
        hot = lane == idx
        vals.append(m)
        idxs.append(idx)
        hots.append(hot)
        l = jnp.where(hot, -jnp.inf, l)
    es = [jnp.exp(v - vals[0]) for v in vals]
    den = es[0] + es[1] + es[2] + es[3]
    hot_all = jnp.zeros((tm, LANES), F32)
    for hot in hots:
        hot_all = hot_all + hot.astype(F32)
    ii = lax.broadcasted_iota(I32, (tm, tm), 0)
    jj = lax.broadcasted_iota(I32, (tm, tm), 1)
    before = _dot((jj < ii).astype(BF16), hot_all.astype(BF16))
    cnt = jnp.sum(hot_all, axis=0, keepdims=True)
    cnt8 = jnp.floor((cnt + (SEG_ALIGN - 1.0)) * (1.0 / SEG_ALIGN)) * SEG_ALIGN
    ei = lax.broadcasted_iota(I32, (LANES, LANES), 0)
    ej = lax.broadcasted_iota(I32, (LANES, LANES), 1)
    units = jnp.broadcast_to(cnt8 * (1.0 / SEG_ALIGN), (8, LANES)).astype(BF16)
    seg = _dot(units, (ei < ej).astype(BF16))[0:1] * SEG_ALIGN
    pos_e = seg + before
    mi = jnp.zeros((tm, LANES), I32)
    mf = jnp.zeros((tm, LANES), F32)
    for kk in range(TOP_K):
        spos = jnp.sum(jnp.where(hots[kk], pos_e, 0.0), axis=1, keepdims=True).astype(I32)
        mi = jnp.where(lane == kk, idxs[kk], mi)
        mi = jnp.where(lane == TOP_K + kk, spos, mi)
        mf = jnp.where(lane == kk, es[kk] / den, mf)
    mi_ref[...] = mi
    mf_ref[...] = mf
    rowi = lax.broadcasted_iota(I32, (8, LANES), 0)
    info = jnp.where(rowi == 0, cnt8, jnp.where(rowi == 1, seg, jnp.where(rowi == 2, carry_sc[...], 0.0)))
    blk_ref[0] = info.astype(I32)
    carry_sc[...] = carry_sc[...] + cnt8
    tot_ref[...] = carry_sc[...]


def _route(x, nw, mod, wr, br, lay):
    T, D = x.shape
    B, S = lay["B"], lay["S"]
    mrow = lambda i: jnp.where(i < (B * S) // TMR, i // (S // TMR), B)
    row = lambda n: pl.BlockSpec((TMR, n), lambda i: (i, 0))
    return pl.pallas_call(
        _route_kernel,
        grid=(T // TMR,),
        in_specs=[row(D),
                  pl.BlockSpec((1, D), lambda i: (0, 0)),
                  pl.BlockSpec((1, 1, D), lambda i: (mrow(i), 0, 3)),
                  pl.BlockSpec((1, 1, D), lambda i: (mrow(i), 0, 4)),
                  pl.BlockSpec((D, LANES), lambda i: (0, 0)),
                  pl.BlockSpec((1, LANES), lambda i: (0, 0))],
        out_specs=[row(D), row(LANES), row(LANES),
                   pl.BlockSpec((1, 8, LANES), lambda i: (i, 0, 0)),
                   pl.BlockSpec((1, LANES), lambda i: (0, 0))],
        out_shape=[jax.ShapeDtypeStruct((T, D), F32), jax.ShapeDtypeStruct((T, LANES), I32),
                   jax.ShapeDtypeStruct((T, LANES), F32),
                   jax.ShapeDtypeStruct((T // TMR, 8, LANES), I32),
                   jax.ShapeDtypeStruct((1, LANES), F32)],
        scratch_shapes=[pltpu.VMEM((1, LANES), F32)],
        compiler_params=_params("arbitrary"),
        name="route",
    )(x, nw, mod, mod, wr, br)


def _segment_starts(info_ref, base, make_copy):
    def start_expert(e, carry):
        n = info_ref[base + e] // SEG_ALIGN
        src0 = info_ref[base + N_EXPERTS + e]
        dst0 = info_ref[base + 2 * N_EXPERTS + e]

        def start_chunk(c, carry):
            make_copy(pl.multiple_of(src0 + c * SEG_ALIGN, SEG_ALIGN),
                      pl.multiple_of(dst0 + c * SEG_ALIGN, SEG_ALIGN)).start()
            return carry

        lax.fori_loop(0, n, start_chunk, 0)
        return carry

    lax.fori_loop(0, N_EXPERTS, start_expert, 0)


def _segment_waits(info_ref, base, make_copy):
    total = lax.fori_loop(0, N_EXPERTS, lambda e, n: n + info_ref[base + e] // SEG_ALIGN, 0)

    def wait_chunk(c, carry):
        make_copy(0, 0).wait()
        return carry

    lax.fori_loop(0, total, wait_chunk, 0)


def _segment_copies(info_ref, base, make_copy):
    _segment_starts(info_ref, base, make_copy)
    _segment_waits(info_ref, base, make_copy)


def _dispatch_kernel(info_ref, h_ref, mi_ref, xb_ref, stg_sc, sem):
    tm = h_ref.shape[0]
    rows = stg_sc.shape[0]
    seg_copy = lambda s, d: pltpu.make_async_copy(
        stg_sc.at[pl.ds(s, SEG_ALIGN)], xb_ref.at[pl.ds(d, SEG_ALIGN)], sem)

    @pl.when(pl.program_id(0) == 0)
    def _():
        stg_sc[0:MOE_BLK, :] = jnp.zeros((MOE_BLK, stg_sc.shape[1]), F32)
        _segment_copies(info_ref, INFO_TAIL, seg_copy)

        def spare_copy(j):
            return pltpu.make_async_copy(stg_sc.at[pl.ds(0, MOE_BLK)],
                                         xb_ref.at[pl.ds(pl.multiple_of(j * MOE_BLK, MOE_BLK), MOE_BLK)], sem)

        def start_spare(j, carry):
            spare_copy(j).start()
            return carry

        def wait_spare(j, carry):
            spare_copy(j).wait()
            return carry

        first_spare = info_ref[INFO_NUSED]
        lax.fori_loop(first_spare, xb_ref.shape[0] // MOE_BLK, start_spare, 0)
        lax.fori_loop(first_spare, xb_ref.shape[0] // MOE_BLK, wait_spare, 0)

    spos_t = mi_ref[...].astype(F32).T
    r = lax.broadcasted_iota(I32, (rows, tm), 0).astype(F32)
    pm = r == spos_t[TOP_K:TOP_K + 1]
    for kk in range(1, TOP_K):
        pm = pm | (r == spos_t[TOP_K + kk:TOP_K + kk + 1])
    stg_sc[...] = _dot(pm.astype(BF16), h_ref[...].astype(BF16))
    _segment_copies(info_ref, 0, seg_copy)


def _dispatch(info, h, mi, rows):
    T, D = h.shape
    return pl.pallas_call(
        _dispatch_kernel,
        grid=(T // TMR,),
        in_specs=[pl.BlockSpec((INFO_LANES,), lambda i: (i,), memory_space=pltpu.SMEM),
                  pl.BlockSpec((TMR, D), lambda i: (i, 0)),
                  pl.BlockSpec((TMR, LANES), lambda i: (i, 0))],
        out_specs=pl.BlockSpec(memory_space=pl.ANY),
        out_shape=jax.ShapeDtypeStruct((rows, D), F32),
        scratch_shapes=[pltpu.VMEM((STG_ROWS, D), F32), pltpu.SemaphoreType.DMA],
        compiler_params=_params("arbitrary"),
        name="dispatch",
    )(info, h, mi)


def _gmm_kernel(be_ref, nu_ref, nxt_ref, slot_ref, x_ref, wgu_hbm, bgu_ref, wdn_hbm, bdn_ref, o_ref,
                gu_buf, dn_buf, wgu_sc, wdn_sc, sems, *, layer):
    i = pl.program_id(0)
    e = be_ref[i]
    prev = be_ref[jnp.maximum(i - 1, 0)]

    def fetch(expert, slot):
        return (pltpu.make_async_copy(wgu_hbm.at[layer, expert], gu_buf.at[slot], sems.at[0, slot]),
                pltpu.make_async_copy(wdn_hbm.at[layer, expert], dn_buf.at[slot], sems.at[1, slot]))

    @pl.when(i == 0)
    def _():
        for cp in fetch(e, slot_ref[0]):
            cp.start()

    @pl.when(((i == 0) | (e != prev)) & (i < nu_ref[0]))
    def _():
        slot = slot_ref[i]
        for cp in fetch(e, slot):
            cp.wait()
        wgu_sc[...] = gu_buf[slot].astype(BF16)
        wdn_sc[...] = dn_buf[slot].astype(BF16)

        @pl.when(nxt_ref[i] >= 0)
        def _():
            for cp in fetch(nxt_ref[i], 1 - slot):
                cp.start()

    @pl.when(i < nu_ref[0])
    def _():
        xb = x_ref[...].astype(BF16)
        gu = _dot(xb, wgu_sc[...]) + bgu_ref[0, 0]
        glu = jnp.minimum(gu[:, 0:D_EXPERT], SWIGLU_LIMIT)
        lin = jnp.clip(gu[:, D_EXPERT:2 * D_EXPERT], -SWIGLU_LIMIT, SWIGLU_LIMIT)
        act = glu * _sigmoid(SWIGLU_ALPHA * glu) * (lin + 1.0)
        o_ref[...] = _dot(act.astype(BF16), wdn_sc[...]) + bdn_ref[0, 0]

    @pl.when(i >= nu_ref[0])
    def _():
        o_ref[...] = jnp.zeros(o_ref.shape, F32)


def _gmm(block_e, n_used, nxt, slot, xb, wgu, bgu, wdn, bdn, l):
    P, D = xb.shape
    L, E, _, F2 = wgu.shape
    xmap = lambda i, be, nu, nx, sl: (jnp.minimum(i, nu[0] - 1), 0)
    bmap = lambda i, be, nu, nx, sl: (l, be[i], 0, 0)
    return pl.pallas_call(
        functools.partial(_gmm_kernel, layer=l),
        grid_spec=pltpu.PrefetchScalarGridSpec(
            num_scalar_prefetch=4,
            grid=(P // MOE_BLK,),
            in_specs=[pl.BlockSpec((MOE_BLK, D), xmap),
                      pl.BlockSpec(memory_space=pl.ANY),
                      pl.BlockSpec((1, 1, 1, F2), bmap),
                      pl.BlockSpec(memory_space=pl.ANY),
                      pl.BlockSpec((1, 1, 1, D), bmap)],
            out_specs=pl.BlockSpec((MOE_BLK, D), lambda i, be, nu, nx, sl: (i, 0)),
            scratch_shapes=[pltpu.VMEM((2, D, F2), F32), pltpu.VMEM((2, F2 // 2, D), F32),
                            pltpu.VMEM((D, F2), BF16), pltpu.VMEM((F2 // 2, D), BF16),
                            pltpu.SemaphoreType.DMA((2, 2))]),
        out_shape=jax.ShapeDtypeStruct((P, D), F32),
        compiler_params=_params("arbitrary"),
        name="gmm",
    )(block_e, n_used, nxt, slot, xb, wgu, bgu.reshape(L, E, 1, F2), wdn, bdn.reshape(L, E, 1, D))


def _combine_kernel(info_ref, x_ref, mi_ref, gate_ref, g2_ref, fw_ref, yb_ref, o_ref, stg_sc, sem, *, final):
    tm = x_ref.shape[0]
    rows = stg_sc.shape[0]

    @pl.when(pl.program_id(0) == 0)
    def _():
        stg_sc[...] = jnp.zeros(stg_sc.shape, F32)

    _segment_copies(info_ref, 0, lambda s, d: pltpu.make_async_copy(
        yb_ref.at[pl.ds(d, SEG_ALIGN)], stg_sc.at[pl.ds(s, SEG_ALIGN)], sem))

    spos = mi_ref[...].astype(F32)
    gate = gate_ref[...]
    r = lax.broadcasted_iota(I32, (tm, rows), 1).astype(F32)
    g = jnp.where(r == spos[:, TOP_K:TOP_K + 1], gate[:, 0:1], 0.0)
    for kk in range(1, TOP_K):
        g = g + jnp.where(r == spos[:, TOP_K + kk:TOP_K + kk + 1], gate[:, kk:kk + 1], 0.0)
    g_hi, g_lo = _split2(g)
    y = stg_sc[...].astype(BF16)
    out = x_ref[...] + g2_ref[0] * (_dot(g_hi, y) + _dot(g_lo, y))
    if final:
        ms = jnp.mean(out * out, axis=-1, keepdims=True)
        out = out * lax.rsqrt(ms + EPS) * fw_ref[...]
    o_ref[...] = out


def _combine(info, x, mi, gates, mod, fw, yb, lay, final):
    T, D = x.shape
    B, S = lay["B"], lay["S"]
    mrow = lambda i: jnp.where(i < (B * S) // TMR, i // (S // TMR), B)
    return pl.pallas_call(
        functools.partial(_combine_kernel, final=final),
        grid=(T // TMR,),
        in_specs=[pl.BlockSpec((INFO_LANES,), lambda i: (i,), memory_space=pltpu.SMEM),
                  pl.BlockSpec((TMR, D), lambda i: (i, 0)),
                  pl.BlockSpec((TMR, LANES), lambda i: (i, 0)),
                  pl.BlockSpec((TMR, LANES), lambda i: (i, 0)),
                  pl.BlockSpec((1, 1, D), lambda i: (mrow(i), 0, 5)),
                  pl.BlockSpec((1, D), lambda i: (0, 0)),
                  pl.BlockSpec(memory_space=pl.ANY)],
        out_specs=pl.BlockSpec((TMR, D), lambda i: (i, 0)),
        out_shape=jax.ShapeDtypeStruct((T, D), F32),
        scratch_shapes=[pltpu.VMEM((STG_ROWS, D), F32), pltpu.SemaphoreType.DMA],
        compiler_params=_params("arbitrary"),
        name="combine",
    )(info, x, mi, gates, mod, fw, yb)


def _moe(x, nw, mod, wr, br, wgu, bgu, wdn, bdn, fw, lay, l, final):
    T, D = x.shape
    ntb = T // TMR
    h, mi, mf, blk, tot = _route(x, nw, mod, wr, br, lay)
    counts = tot[0, :N_EXPERTS].astype(I32)
    padded = (counts + MOE_BLK - 1) // MOE_BLK * MOE_BLK
    pad_ends = jnp.cumsum(padded)
    pad_starts = pad_ends - padded
    nblk = -(-(T * TOP_K + ntb * N_EXPERTS * (SEG_ALIGN - 1)) // MOE_BLK) + N_EXPERTS
    starts = jnp.arange(nblk, dtype=I32) * MOE_BLK
    block_e = jnp.minimum(jnp.sum((pad_ends[None, :] <= starts[:, None]).astype(I32), axis=1), N_EXPERTS - 1)
    n_used = (pad_ends[N_EXPERTS - 1:] // MOE_BLK).astype(I32)
    rep = lambda v: jnp.broadcast_to(v[None, :], (ntb, v.shape[0]))
    zeros_e = jnp.zeros((N_EXPERTS,), I32)
    info = jnp.concatenate([
        blk[:, 0, :N_EXPERTS], blk[:, 1, :N_EXPERTS], blk[:, 2, :N_EXPERTS] + pad_starts[None, :], rep(zeros_e),
        rep(padded - counts), rep(zeros_e), rep(pad_starts + counts),
        rep(jnp.concatenate([n_used, jnp.zeros((INFO_LANES - INFO_NUSED - 1,), I32)]))], axis=1).reshape(-1)
    eid = jnp.arange(N_EXPERTS, dtype=I32)
    has = padded > 0
    later = jnp.where(has[None, :] & (eid[None, :] > eid[:, None]), eid[None, :], N_EXPERTS)
    nxt_e = jnp.min(later, axis=1)
    nxt_e = jnp.where(nxt_e < N_EXPERTS, nxt_e, -1)
    slot_e = (jnp.cumsum(has.astype(I32)) - 1) % 2
    xb = _dispatch(info, h, mi, nblk * MOE_BLK)
    yb = _gmm(block_e, n_used, jnp.take(nxt_e, block_e), jnp.take(slot_e, block_e), xb, wgu, bgu, wdn, bdn, l)
    return _combine(info, x, mi, mf, mod, fw, yb, lay, final)


def _rope_tables(S):
    t = np.arange(S)
    row = (t // GRID_W).astype(np.float64)
    col = (t % GRID_W).astype(np.float64)
    lane = np.arange(LANES)
    j = lane % 16
    inv = ROPE_THETA ** (-(j.astype(np.float32)) / np.float32(16.0))
    pos = np.where((lane % 64) < 32, row[:, None], col[:, None]).astype(np.float32)
    ang = (pos * inv.astype(np.float32)[None, :]).astype(np.float32)
    sign = np.where((lane % 32) < 16, -1.0, 1.0).astype(np.float32)
    cos_t = np.concatenate([np.cos(ang), np.ones((TM, LANES), np.float32)], axis=0)
    sin_t = np.concatenate([np.sin(ang) * sign[None, :], np.zeros((TM, LANES), np.float32)], axis=0)
    return cos_t.astype(np.float32), sin_t.astype(np.float32)


def kernel(x, c, ctx, c_ctx, w_ada, b_ada, norm1_w, w_in, da_lambda, da_subln_w, ssd_conv_w, ssd_conv_b,
           ssd_a_log, ssd_dt_bias, ssd_d, ssd_norm_w, gla_gk_up, gla_gk_b, gla_norm_w, w_out, norm2_w,
           w_router, b_router, w_gate_up, b_gate_up, w_down, b_down, final_norm_w):
    B, S, D = x.shape
    Lc = ctx.shape[1]
    depth = w_ada.shape[0]
    assert S % TMR == 0 and Lc % TM == 0 and (B * Lc) % TMR == 0 and S % GRID_W == 0
    lay = dict(B=B, S=S, Lc=Lc, nlb=(B * S) // TM, bpb=S // TM, cpb=Lc // TM)
    n_lat = B * S

    xs = jnp.concatenate([x.reshape(B * S, D), ctx.reshape(B * Lc, D)], axis=0)

    cc = jnp.zeros((8, D), F32).at[0:B].set(c).at[B].set(c_ctx)
    mod_all = _ada(cc, w_ada, b_ada)

    cos_t, sin_t = (jnp.asarray(t) for t in _rope_tables(S))

    for l in range(depth):
        last = l == depth - 1
        lam_init = 0.8 - 0.6 * math.exp(-0.3 * l)
        mod = mod_all[l, 0:B + 1].reshape(B + 1, 1, 6 * D)

        wi = w_in[l]
        misc_w = jnp.zeros((D, LANES), F32)
        misc_w = misc_w.at[:, MISC_DT:MISC_DT + 8].set(wi[:, 2304:2312])
        misc_w = misc_w.at[:, MISC_CODE:MISC_CODE + 2 * GLA_RANK].set(wi[:, 3080:3112])
        w_re = jnp.concatenate([wi[:, 0:2304], wi[:, 2312:3080], misc_w], axis=1).astype(BF16)
        a128 = jnp.zeros((1, LANES), F32).at[0, 0:8].set(-jnp.exp(ssd_a_log[l].astype(F32)).reshape(-1))
        bias128 = jnp.zeros((1, LANES), F32).at[0, 0:8].set(ssd_dt_bias[l].astype(F32).reshape(-1))
        wgk = jnp.zeros((2, LANES, GLA_KEY_WIDTH), F32)
        for d in range(2):
            wgk = wgk.at[d, MISC_CODE + d * GLA_RANK:MISC_CODE + (d + 1) * GLA_RANK, :].set(gla_gk_up[l, d])
        bgk = gla_gk_b[l].reshape(2, 1, GLA_KEY_WIDTH)
        dsk = jnp.repeat(ssd_d[l], SSD_HEAD_DIM).reshape(1, SSD_WIDTH)
        snw = ssd_norm_w[l].reshape(1, SSD_WIDTH)
        gnw = jnp.tile(gla_norm_w[l], GLA_HEADS).reshape(1, GLA_WIDTH)
        wr = jnp.zeros((D, LANES), F32).at[:, 0:N_EXPERTS].set(w_router[l])
        br = jnp.zeros((1, LANES), F32).at[0, 0:N_EXPERTS].set(b_router[l])

        q, k, v, z, xbc_act, gla_in, misc = _inproj(xs, norm1_w[l].reshape(1, D), mod, cos_t, sin_t, w_re,
                                                    ssd_conv_w[l], ssd_conv_b[l].reshape(1, SSD_CONV_CH), lay)

        a = _attention(q, k, v, da_lambda[l], da_subln_w[l].reshape(1, DA_V), lay, lam_init,
                       ctx_queries=not last)
        yf, yb = _ssd(xbc_act, misc, a128, bias128, lay)
        of, ob = _gla(gla_in, misc, wgk, bgk, lay)

        nrows = n_lat if last else xs.shape[0]
        xs = _outproj(xs, a, yf, yb, xbc_act, z, of, ob, gla_in, dsk, snw, gnw,
                      w_out[l].astype(BF16), mod, lay, nrows)
        xs = _moe(xs, norm2_w[l].reshape(1, D), mod, wr, br, w_gate_up, b_gate_up, w_down, b_down,
                  final_norm_w.reshape(1, D), lay, l, last)

    return xs.reshape(B, S, D)
```

```python
import functools
import math

import numpy as np
import jax
import jax.numpy as jnp
from jax import lax
from jax.experimental import pallas as pl
from jax.experimental.pallas import tpu as pltpu

F32 = jnp.float32
BF16 = jnp.bfloat16
I32 = jnp.int32

GRID_W = 64
EPS = 1e-6
DA_HEADS = 4
DA_QK = 64
DA_V = 128
DA_WIDTH = 512
ROPE_THETA = 10000.0
SSD_HEADS = 4
SSD_HEAD_DIM = 64
SSD_WIDTH = 256
SSD_STATE = 64
SSD_CONV_CH = 512
GLA_HEADS = 4
GLA_DK = 32
GLA_DV = 64
GLA_KEY_WIDTH = 128
GLA_WIDTH = 256
GLA_RANK = 16
GLA_NORMALIZER = 16.0
N_EXPERTS = 32
TOP_K = 4
D_EXPERT = 1024
SWIGLU_LIMIT = 7.0
SWIGLU_ALPHA = 1.702

LANES = 128
TM = 256
SSD_Q = 128
GLA_Q = 64
GLA_SUB = 4
ATT_TQ = 256
ATT_KT = (2816, 768, 512, 256)
LOG2E = 1.4426950408889634
MOE_BLK = 256
TMR = 512
SEG_ALIGN = 16
STG_ROWS = TMR * TOP_K + N_EXPERTS * SEG_ALIGN
INFO_LANES = 256
INFO_TAIL = 128
INFO_NUSED = 224
VMEM_LIMIT = 56 * 1024 * 1024

C_Q, C_K, C_V, C_Z, C_XBC, C_GLA, C_MISC, C_END = 0, 512, 1024, 1536, 1792, 2304, 3072, 3200
MISC_DT = 0
MISC_CODE = 8


def _sigmoid(x):
    return 1.0 / (1.0 + jnp.exp(-x))


def _softplus(x):
    return jnp.maximum(x, 0.0) + jnp.log(1.0 + jnp.exp(-jnp.abs(x)))


def _split2(a):
    hi = a.astype(BF16)
    lo = (a - hi.astype(F32)).astype(BF16)
    return hi, lo


def _split3(a):
    a1 = a.astype(BF16)
    r1 = a - a1.astype(F32)
    a2 = r1.astype(BF16)
    a3 = (r1 - a2.astype(F32)).astype(BF16)
    return a1, a2, a3


def _dot(a, b):
    return jnp.dot(a, b, preferred_element_type=F32)


def _dot_hi(a, b):
    a1, a2 = _split2(a)
    b1, b2 = _split2(b)
    return _dot(a1, b1) + (_dot(a1, b2) + _dot(a2, b1))


def _dot_exact_lhs(m, a):
    a1, a2, a3 = _split3(a)
    mb = m.astype(BF16)
    return _dot(mb, a1) + (_dot(mb, a2) + _dot(mb, a3))


def _dot_exact_rhs(a, m):
    a1, a2, a3 = _split3(a)
    mb = m.astype(BF16)
    return _dot(a1, mb) + (_dot(a2, mb) + _dot(a3, mb))


def _params(*sem):
    return pltpu.CompilerParams(dimension_semantics=sem, vmem_limit_bytes=VMEM_LIMIT)


def _ada_kernel(c_ref, w_ref, b_ref, o_ref):
    c = c_ref[...]
    s = c * _sigmoid(c)
    o_ref[0] = _dot_hi(s, w_ref[0]) + b_ref[0]


def _ada(cc, w_ada, b_ada):
    L, D, N = w_ada.shape
    tn = 512
    return pl.pallas_call(
        _ada_kernel,
        grid=(L, N // tn),
        in_specs=[pl.BlockSpec((8, D), lambda l, j: (0, 0)),
                  pl.BlockSpec((1, D, tn), lambda l, j: (l, 0, j)),
                  pl.BlockSpec((1, 1, tn), lambda l, j: (l, 0, j))],
        out_specs=pl.BlockSpec((1, 8, tn), lambda l, j: (l, 0, j)),
        out_shape=jax.ShapeDtypeStruct((L, 8, N), F32),
        compiler_params=_params("parallel", "parallel"),
        name="ada",
    )(cc, w_ada, b_ada.reshape(L, 1, N))


def _inproj_kernel(x_ref, xp_ref, xn_ref, nw_ref, sh_ref, sc_ref, cos_ref, sin_ref, w_ref, cw_ref, cb_ref,
                   q_ref, k_ref, v_ref, z_ref, xbc_ref, gla_ref, misc_ref, *, nlb, bpb, cpb):
    def modulated(xf):
        ms = jnp.mean(xf * xf, axis=-1, keepdims=True)
        return ((xf * lax.rsqrt(ms + EPS) * nw_ref[...]) * (1.0 + sc_ref[0]) + sh_ref[0]).astype(BF16)

    hb = modulated(x_ref[...])
    cos = cos_ref[...]
    sin = sin_ref[...]
    lane = lax.broadcasted_iota(I32, cos.shape, 1)
    first = (lane % 32) < 16

    def rope(p):
        outs = []
        for hd in range(DA_HEADS):
            ph = p[:, hd * LANES:(hd + 1) * LANES]
            sw = jnp.where(first, pltpu.roll(ph, LANES - 16, 1), pltpu.roll(ph, 16, 1))
            outs.append(ph * cos + sw * sin)
        return jnp.concatenate(outs, axis=1)

    q = rope(_dot(hb, w_ref[:, C_Q:C_K])) * (DA_QK ** -0.5 * LOG2E)
    q_ref[...] = q.astype(BF16)
    k_ref[...] = rope(_dot(hb, w_ref[:, C_K:C_V])).astype(BF16)
    v_ref[...] = _dot(hb, w_ref[:, C_V:C_Z]).astype(BF16)
    z_ref[...] = _dot(hb, w_ref[:, C_Z:C_XBC])
    gla_ref[...] = _dot(hb, w_ref[:, C_GLA:C_MISC])
    misc_ref[...] = _dot(hb, w_ref[:, C_MISC:C_END])

    xbc = _dot(hb, w_ref[:, C_XBC:C_GLA])
    halo = _dot(modulated(jnp.concatenate([xp_ref[...], xn_ref[...]], axis=0)), w_ref[:, C_XBC:C_GLA])
    i = pl.program_id(0)
    pos = jnp.where(i < nlb, i % bpb, (i - nlb) % cpb)
    last = jnp.where(i < nlb, bpb - 1, cpb - 1)
    prev_row = halo[7:8, :] * (pos != 0).astype(F32)
    next_row = halo[8:9, :] * (pos != last).astype(F32)
    ridx = lax.broadcasted_iota(I32, xbc.shape, 0)
    xm1 = jnp.where(ridx == 0, prev_row, pltpu.roll(xbc, 1, 0))
    xp1 = jnp.where(ridx == xbc.shape[0] - 1, next_row, pltpu.roll(xbc, xbc.shape[0] - 1, 0))
    cw = cw_ref[...]
    y = cw[0:1] * xm1 + cw[1:2] * xbc + cw[2:3] * xp1 + cb_ref[...]
    xbc_ref[...] = y * _sigmoid(y)


def _inproj(x, nw, mod, cos_t, sin_t, w, cw, cb, lay):
    T, D = x.shape
    nlb, bpb, B = lay["nlb"], lay["bpb"], lay["B"]
    r8 = TM // 8
    mrow = lambda i: jnp.where(i < nlb, i // bpb, B)
    prow = lambda i: jnp.where(i < nlb, i % bpb, bpb)
    row = lambda n: pl.BlockSpec((TM, n), lambda i: (i, 0))
    spb = bpb + lay["cpb"]
    kvrow = lambda i: (jnp.where(i < nlb, (i // bpb) * spb + lay["cpb"] + i % bpb,
                                 ((i - nlb) // lay["cpb"]) * spb + (i - nlb) % lay["cpb"]), 0)
    kv = pl.BlockSpec((TM, DA_WIDTH), kvrow)
    outs = [(512, BF16), (512, BF16), (512, BF16), (256, F32), (512, F32), (768, F32), (128, F32)]
    out_specs = [row(n) for n, _ in outs]
    out_specs[1] = kv
    out_specs[2] = kv
    return pl.pallas_call(
        functools.partial(_inproj_kernel, nlb=nlb, bpb=bpb, cpb=lay["cpb"]),
        grid=(T // TM,),
        in_specs=[row(D),
                  pl.BlockSpec((8, D), lambda i: (jnp.maximum(i * r8 - 1, 0), 0)),
                  pl.BlockSpec((8, D), lambda i: (jnp.minimum((i + 1) * r8, T // 8 - 1), 0)),
                  pl.BlockSpec((1, D), lambda i: (0, 0)),
                  pl.BlockSpec((1, 1, D), lambda i: (mrow(i), 0, 0)),
                  pl.BlockSpec((1, 1, D), lambda i: (mrow(i), 0, 1)),
                  pl.BlockSpec((TM, LANES), lambda i: (prow(i), 0)),
                  pl.BlockSpec((TM, LANES), lambda i: (prow(i), 0)),
                  pl.BlockSpec((D, C_END), lambda i: (0, 0)),
                  pl.BlockSpec((3, SSD_CONV_CH), lambda i: (0, 0)),
                  pl.BlockSpec((1, SSD_CONV_CH), lambda i: (0, 0))],
        out_specs=out_specs,
        out_shape=[jax.ShapeDtypeStruct((T, n), dt) for n, dt in outs],
        compiler_params=_params("parallel"),
        name="inproj",
    )(x, x, x, nw, mod, mod, cos_t, sin_t, w, cw, cb)


def _lane_fold(x, op):
    f = x[:, 0:LANES]
    for i in range(1, x.shape[1] // LANES):
        f = op(f, x[:, i * LANES:(i + 1) * LANES])
    return f


def _attn_stack_maps(q_ref, q2_sc):
    tq = q_ref.shape[0]
    q = q_ref[...]
    lane = lax.broadcasted_iota(I32, q.shape, 1)
    zero = jnp.zeros_like(q)
    q2_sc[0:tq, :] = jnp.where(lane < DA_QK, q, zero)
    q2_sc[tq:2 * tq, :] = jnp.where(lane >= DA_QK, q, zero)


def _attn_scores_tile(q2_sc, k_ref, s_sc, slot, mf_sc, j, kt):
    kj = k_ref[pl.ds(pl.multiple_of(j * kt, kt), kt), :]
    s = lax.dot_general(q2_sc[...], kj, (((1,), (1,)), ((), ())), preferred_element_type=F32)
    s_sc[slot, j, :, 0:kt] = s
    mf_sc[...] = jnp.maximum(mf_sc[...], _lane_fold(s, jnp.maximum))


def _attn_weigh_tile(v_ref, s_sc, slot, p_sc, l_sc, acc_sc, m, j, kt):
    sub = next(w for w in (512, 256) if kt % w == 0)
    for c0 in range(0, kt, sub):
        p = jnp.exp2(s_sc[slot, j, :, c0:c0 + sub] - m)
        l_sc[...] += _lane_fold(p, jnp.add)
        p_sc[:, c0:c0 + sub] = p.astype(BF16)
    vj = v_ref[pl.ds(pl.multiple_of(j * kt, kt), kt), :]
    acc_sc[...] += _dot(p_sc[:, 0:kt], vj)


def _attn_finish(lam_ref, w_ref, o_ref, l_sc, acc_sc, lam_init):
    tq = o_ref.shape[0]
    lp = lam_ref[...]
    lam = (jnp.exp(jnp.sum(lp[0:1] * lp[1:2], axis=1, keepdims=True))
           - jnp.exp(jnp.sum(lp[2:3] * lp[3:4], axis=1, keepdims=True)) + lam_init)
    acc = acc_sc[...]
    l = jnp.sum(l_sc[...], axis=1, keepdims=True)
    o = acc[0:tq] / l[0:tq] - lam * (acc[tq:2 * tq] / l[tq:2 * tq])
    ms = jnp.mean(o * o, axis=-1, keepdims=True)
    o_ref[...] = (o * lax.rsqrt(ms + EPS) * w_ref[...]) * (1.0 - lam_init)


def _attn_kernel(q_ref, k_ref, v_ref, lam_ref, w_ref, o_ref, q2_sc, s_sc, p_sc, mf_sc, l_sc, acc_sc,
                 *, nt, kt, lam_init):
    _attn_stack_maps(q_ref, q2_sc)
    mf_sc[...] = jnp.full(mf_sc.shape, -jnp.inf, F32)
    l_sc[...] = jnp.zeros(l_sc.shape, F32)
    acc_sc[...] = jnp.zeros(acc_sc.shape, F32)

    def scores(j, carry):
        _attn_scores_tile(q2_sc, k_ref, s_sc, 0, mf_sc, j, kt)
        return carry

    lax.fori_loop(0, nt, scores, 0)
    m = jnp.max(mf_sc[...], axis=1, keepdims=True)

    def weigh(j, carry):
        _attn_weigh_tile(v_ref, s_sc, 0, p_sc, l_sc, acc_sc, m, j, kt)
        return carry

    lax.fori_loop(0, nt, weigh, 0)
    _attn_finish(lam_ref, w_ref, o_ref, l_sc, acc_sc, lam_init)


def _attn_pipe_kernel(q_ref, k_ref, v_ref, lam_ref, w_ref, o_ref, q2_sc, s_sc, p_sc, mf_sc, m_sc, l_sc, acc_sc,
                      *, nq, nt, kt, lam_init):
    t = pl.program_id(2)

    @pl.when(t < nq)
    def _():
        _attn_stack_maps(q_ref, q2_sc)
        mf_sc[...] = jnp.full(mf_sc.shape, -jnp.inf, F32)

    @pl.when(t > 0)
    def _():
        l_sc[...] = jnp.zeros(l_sc.shape, F32)
        acc_sc[...] = jnp.zeros(acc_sc.shape, F32)

    def run(slot_a, slot_b):
        m_prev = None if slot_b is None else m_sc[slot_b]

        def body(j, carry):
            if slot_b is not None:
                _attn_weigh_tile(v_ref, s_sc, slot_b, p_sc, l_sc, acc_sc, m_prev, j, kt)
            if slot_a is not None:
                _attn_scores_tile(q2_sc, k_ref, s_sc, slot_a, mf_sc, j, kt)
            return carry

        lax.fori_loop(0, nt, body, 0)
        if slot_a is not None:
            m_sc[slot_a] = jnp.max(mf_sc[...], axis=1, keepdims=True)

    inner = (t > 0) & (t < nq)
    pl.when(t == 0)(lambda: run(0, None))
    pl.when(inner & (t % 2 == 1))(lambda: run(1, 0))
    pl.when(inner & (t % 2 == 0))(lambda: run(0, 1))
    pl.when(t == nq)(lambda: run(None, (nq - 1) % 2))

    @pl.when(t > 0)
    def _():
        _attn_finish(lam_ref, w_ref, o_ref, l_sc, acc_sc, lam_init)


def _attention(q, k, v, lam_p, subln_w, lay, lam_init, *, ctx_queries):
    B, S, Lc = lay["B"], lay["S"], lay["Lc"]
    nkeys = Lc + S
    assert nkeys % Lc == 0
    small = [pl.BlockSpec((4, DA_QK), lambda b, h, t: (0, 0)), pl.BlockSpec((1, DA_V), lambda b, h, t: (0, 0))]
    stacked = lambda n, dt: pltpu.VMEM((2 * ATT_TQ, n), dt)

    kt = next(t for t in ATT_KT if nkeys % t == 0)
    nq = S // ATT_TQ
    kmap = lambda b, h, t: (b, h)
    a_lat = pl.pallas_call(
        functools.partial(_attn_pipe_kernel, nq=nq, nt=nkeys // kt, kt=kt, lam_init=lam_init),
        grid=(B, DA_HEADS, nq + 1),
        in_specs=[pl.BlockSpec((ATT_TQ, LANES), lambda b, h, t: (b * nq + jnp.minimum(t, nq - 1), h)),
                  pl.BlockSpec((nkeys, LANES), kmap),
                  pl.BlockSpec((nkeys, LANES), kmap)] + small,
        out_specs=pl.BlockSpec((ATT_TQ, LANES), lambda b, h, t: (b * nq + jnp.maximum(t - 1, 0), h)),
        out_shape=jax.ShapeDtypeStruct((B * S, DA_WIDTH), F32),
        scratch_shapes=[stacked(LANES, BF16),
                        pltpu.VMEM((2, nkeys // kt, 2 * ATT_TQ, kt), F32),
                        stacked(kt, BF16), stacked(LANES, F32),
                        pltpu.VMEM((2, 2 * ATT_TQ, 1), F32),
                        stacked(LANES, F32), stacked(LANES, F32)],
        compiler_params=_params("parallel", "parallel", "arbitrary"),
        name="attn",
    )(q, k, v, lam_p, subln_w)
    if not ctx_queries:
        return a_lat

    ktc = next(t for t in ATT_KT if Lc % t == 0)
    nqc = Lc // ATT_TQ
    cmap = lambda b, h, t: (b * (nkeys // Lc), h)
    a_ctx = pl.pallas_call(
        functools.partial(_attn_kernel, nt=Lc // ktc, kt=ktc, lam_init=lam_init),
        grid=(B, DA_HEADS, nqc),
        in_specs=[pl.BlockSpec((ATT_TQ, LANES), lambda b, h, t: ((B * S) // ATT_TQ + b * nqc + t, h)),
                  pl.BlockSpec((Lc, LANES), cmap),
                  pl.BlockSpec((Lc, LANES), cmap)] + small,
        out_specs=pl.BlockSpec((ATT_TQ, LANES), lambda b, h, t: (b * nqc + t, h)),
        out_shape=jax.ShapeDtypeStruct((B * Lc, DA_WIDTH), F32),
        scratch_shapes=[stacked(LANES, BF16),
                        pltpu.VMEM((1, Lc // ktc, 2 * ATT_TQ, ktc), F32),
                        stacked(ktc, BF16), stacked(LANES, F32), stacked(LANES, F32), stacked(LANES, F32)],
        compiler_params=_params("parallel", "parallel", "arbitrary"),
        name="attn_ctx",
    )(q, k, v, lam_p, subln_w)
    return jnp.concatenate([a_lat, a_ctx], axis=0)


def _ssd_dir(xbc, misc, a128, bias128, hs_ref, dirn, rev):
    Q = xbc.shape[0]
    dt = _softplus(misc + bias128)
    a = dt * a128
    ii = lax.broadcasted_iota(I32, (Q, Q), 0)
    jj = lax.broadcasted_iota(I32, (Q, Q), 1)
    M = (jj >= ii) if rev else (jj <= ii)
    acs = _dot_exact_lhs(M.astype(F32), a)
    acs_t = acs.T
    dt_t = dt.T
    xs = xbc[:, 0:SSD_WIDTH]
    Bt = xbc[:, SSD_WIDTH:SSD_WIDTH + 128].T
    Cm = xbc[:, SSD_WIDTH + 128:SSD_WIDTH + 256]
    last = 0 if rev else Q - 1
    ys = []
    for g in range(2):
        Cg = Cm[:, g * SSD_STATE:(g + 1) * SSD_STATE]
        Btg = Bt[g * SSD_STATE:(g + 1) * SSD_STATE, :]
        CB = _dot(Cg, Btg)
        for hh in range(2):
            h = 2 * g + hh
            li = dirn * SSD_HEADS + h
            acs_c = acs[:, li:li + 1]
            acs_r = acs_t[li:li + 1, :]
            dt_r = dt_t[li:li + 1, :]
            decay = jnp.exp(jnp.where(M, acs_c - acs_r, -jnp.inf))
            sc = CB * decay * dt_r
            x_h = xs[:, h * SSD_HEAD_DIM:(h + 1) * SSD_HEAD_DIM]
            hs = hs_ref[h]
            y = _dot(sc, x_h) + _dot(Cg * jnp.exp(acs_c), hs)
            tot = acs_r[:, last:last + 1]
            w_end = jnp.exp(tot - acs_r) * dt_r
            hs_ref[h] = jnp.exp(tot) * hs + _dot(Btg * w_end, x_h)
            ys.append(y)
    return jnp.concatenate(ys, axis=1)


def _ssd_kernel(xf_ref, mf_ref, xb_ref, mb_ref, a_ref, bias_ref, yf_ref, yb_ref, hf_sc, hb_sc):
    @pl.when(pl.program_id(1) == 0)
    def _():
        hf_sc[...] = jnp.zeros(hf_sc.shape, F32)
        hb_sc[...] = jnp.zeros(hb_sc.shape, F32)

    a128 = a_ref[...]
    bias128 = bias_ref[...]
    yf_ref[...] = _ssd_dir(xf_ref[...], mf_ref[...], a128, bias128, hf_sc, 0, False)
    yb_ref[...] = _ssd_dir(xb_ref[...], mb_ref[...], a128, bias128, hb_sc, 1, True)


def _scan_maps(lay, rows):
    ncc, ncl = lay["Lc"] // rows, lay["S"] // rows
    base = (lay["B"] * lay["S"]) // rows
    fmap = lambda b, c: (jnp.where(c < ncc, base + b * ncc + c, b * ncl + c - ncc), 0)
    bmap = lambda b, c: (jnp.where(c < ncc, base + b * ncc + (ncc - 1 - c),
                                   b * ncl + (ncl - 1 - (c - ncc))), 0)
    return ncc + ncl, fmap, bmap


def _ssd(xbc_act, misc, a128, bias128, lay):
    T = xbc_act.shape[0]
    nc, fmap, bmap = _scan_maps(lay, SSD_Q)
    vec = pl.BlockSpec((1, LANES), lambda b, c: (0, 0))
    return pl.pallas_call(
        _ssd_kernel,
        grid=(lay["B"], nc),
        in_specs=[pl.BlockSpec((SSD_Q, SSD_CONV_CH), fmap), pl.BlockSpec((SSD_Q, LANES), fmap),
                  pl.BlockSpec((SSD_Q, SSD_CONV_CH), bmap), pl.BlockSpec((SSD_Q, LANES), bmap),
                  vec, vec],
        out_specs=[pl.BlockSpec((SSD_Q, SSD_WIDTH), fmap), pl.BlockSpec((SSD_Q, SSD_WIDTH), bmap)],
        out_shape=[jax.ShapeDtypeStruct((T, SSD_WIDTH), F32)] * 2,
        scratch_shapes=[pltpu.VMEM((SSD_HEADS, SSD_STATE, SSD_HEAD_DIM), F32)] * 2,
        compiler_params=_params("parallel", "arbitrary"),
        name="ssd",
    )(xbc_act, misc, xbc_act, misc, a128, bias128)


def _gla_block(blk, misc, wgk, bgk, s_ref, rev):
    R = blk.shape[0]
    kw = GLA_KEY_WIDTH
    q, k, v = blk[:, 0:kw], blk[:, kw:2 * kw], blk[:, 2 * kw:2 * kw + GLA_WIDTH]
    pre = _dot_hi(misc, wgk) + bgk
    gk = (jnp.minimum(pre, 0.0) - jnp.log(1.0 + jnp.exp(-jnp.abs(pre)))) / GLA_NORMALIZER
    ii = lax.broadcasted_iota(I32, (R, R), 0)
    jj = lax.broadcasted_iota(I32, (R, R), 1)
    same = (ii // GLA_Q) == (jj // GLA_Q)
    M = same & ((jj >= ii) if rev else (jj <= ii))
    b = _dot_exact_lhs(M.astype(F32), gk)
    tot = _dot_exact_lhs(same.astype(F32), gk)
    q_t = (q * (GLA_DK ** -0.5)) * jnp.exp(b)
    k_t = k * jnp.exp(-b)
    k_end_t = (k * jnp.exp(tot - b)).T
    gk_t = gk.T
    lane_k = lax.broadcasted_iota(I32, (1, kw), 1) // GLA_DK
    lane_v = lax.broadcasted_iota(I32, (1, GLA_WIDTH), 1) // GLA_DV
    o = jnp.zeros((R, GLA_WIDTH), F32)
    for h in range(GLA_HEADS):
        qh = jnp.where(lane_k == h, q_t, 0.0)
        att = lax.dot_general(qh, k_t, (((1,), (1,)), ((), ())), preferred_element_type=F32)
        o = o + _dot(jnp.where(M, att, 0.0), jnp.where(lane_v == h, v, 0.0))
    col_chunk = lax.broadcasted_iota(I32, (1, R), 1) // GLA_Q
    S = s_ref[...]
    bd = (lax.broadcasted_iota(I32, S.shape, 0) // GLA_DK) == (lax.broadcasted_iota(I32, S.shape, 1) // GLA_DV)
    nsub = R // GLA_Q
    inter = [None] * nsub
    for c in (range(nsub - 1, -1, -1) if rev else range(nsub)):
        sel = col_chunk == c
        inter[c] = _dot(q_t[c * GLA_Q:(c + 1) * GLA_Q], S)
        decay = jnp.exp(jnp.sum(jnp.where(sel, gk_t, 0.0), axis=1, keepdims=True))
        S = jnp.where(bd, S * decay + _dot(jnp.where(sel, k_end_t, 0.0), v), 0.0)
    s_ref[...] = S
    return o + jnp.concatenate(inter, axis=0)


def _gla_kernel(gf_ref, mf_ref, gb_ref, mb_ref, wgk_ref, bgk_ref, of_ref, ob_ref, sf_sc, sb_sc):
    @pl.when(pl.program_id(1) == 0)
    def _():
        sf_sc[...] = jnp.zeros(sf_sc.shape, F32)
        sb_sc[...] = jnp.zeros(sb_sc.shape, F32)

    of_ref[...] = _gla_block(gf_ref[...], mf_ref[...], wgk_ref[0], bgk_ref[0], sf_sc, False)
    ob_ref[...] = _gla_block(gb_ref[...], mb_ref[...], wgk_ref[1], bgk_ref[1], sb_sc, True)


def _gla(gla_in, misc, wgk, bgk, lay):
    T = gla_in.shape[0]
    R = GLA_Q * GLA_SUB
    nc, fmap, bmap = _scan_maps(lay, R)
    return pl.pallas_call(
        _gla_kernel,
        grid=(lay["B"], nc),
        in_specs=[pl.BlockSpec((R, 512), fmap), pl.BlockSpec((R, LANES), fmap),
                  pl.BlockSpec((R, 512), bmap), pl.BlockSpec((R, LANES), bmap),
                  pl.BlockSpec((2, LANES, GLA_KEY_WIDTH), lambda b, c: (0, 0, 0)),
                  pl.BlockSpec((2, 1, GLA_KEY_WIDTH), lambda b, c: (0, 0, 0))],
        out_specs=[pl.BlockSpec((R, GLA_WIDTH), fmap), pl.BlockSpec((R, GLA_WIDTH), bmap)],
        out_shape=[jax.ShapeDtypeStruct((T, GLA_WIDTH), F32)] * 2,
        scratch_shapes=[pltpu.VMEM((GLA_KEY_WIDTH, GLA_WIDTH), F32)] * 2,
        compiler_params=_params("parallel", "arbitrary"),
        name="gla",
    )(gla_in, misc, gla_in, misc, wgk, bgk)


def _outproj_kernel(x_ref, a_ref, yf_ref, yb_ref, xs_ref, z_ref, of_ref, ob_ref, g_ref,
                    dsk_ref, snw_ref, gnw_ref, w_ref, g1_ref, o_ref):
    y = yf_ref[...] + yb_ref[...] + dsk_ref[...] * xs_ref[...]
    z = z_ref[...]
    gs = y * (z * _sigmoid(z))
    half = SSD_WIDTH // 2
    parts = []
    for grp in range(2):
        seg = gs[:, grp * half:(grp + 1) * half]
        parts.append(seg * lax.rsqrt(jnp.mean(seg * seg, axis=-1, keepdims=True) + EPS))
    s = jnp.concatenate(parts, axis=1) * snw_ref[...]
    o = of_ref[...] + ob_ref[...]
    bd = ((lax.broadcasted_iota(I32, (GLA_WIDTH, GLA_WIDTH), 0) // GLA_DV)
          == (lax.broadcasted_iota(I32, (GLA_WIDTH, GLA_WIDTH), 1) // GLA_DV))
    ms = _dot_exact_rhs(o * o, jnp.where(bd, 1.0 / GLA_DV, 0.0))
    g = g_ref[...]
    c = (o * lax.rsqrt(ms + EPS) * gnw_ref[...]) * (g * _sigmoid(g))
    mix = jnp.concatenate([a_ref[...], s, c], axis=1).astype(BF16)
    o_ref[...] = x_ref[...] + g1_ref[0] * _dot(mix, w_ref[...])


def _outproj(x, a, yf, yb, xbc_act, z, of, ob, gla_in, dsk, snw, gnw, w_out, mod, lay, nrows):
    D = x.shape[1]
    nlb, bpb, B = lay["nlb"], lay["bpb"], lay["B"]
    mrow = lambda i: jnp.where(i < nlb, i // bpb, B)
    row = lambda n, cb=0: pl.BlockSpec((TM, n), lambda i: (i, cb))
    vec = lambda n: pl.BlockSpec((1, n), lambda i: (0, 0))
    return pl.pallas_call(
        _outproj_kernel,
        grid=(nrows // TM,),
        in_specs=[row(D), row(DA_WIDTH), row(SSD_WIDTH), row(SSD_WIDTH), row(SSD_WIDTH), row(SSD_WIDTH),
                  row(GLA_WIDTH), row(GLA_WIDTH), row(GLA_WIDTH, 2),
                  vec(SSD_WIDTH), vec(SSD_WIDTH), vec(GLA_WIDTH),
                  pl.BlockSpec((D, D), lambda i: (0, 0)),
                  pl.BlockSpec((1, 1, D), lambda i: (mrow(i), 0, 2))],
        out_specs=row(D),
        out_shape=jax.ShapeDtypeStruct((nrows, D), F32),
        compiler_params=_params("parallel"),
        name="outproj",
    )(x, a, yf, yb, xbc_act, z, of, ob, gla_in, dsk, snw, gnw, w_out, mod)


def _route_kernel(x_ref, nw_ref, sh_ref, sc_ref, wr_ref, br_ref,
                  h_ref, mi_ref, mf_ref, blk_ref, tot_ref, carry_sc):
    i = pl.program_id(0)

    @pl.when(i == 0)
    def _():
        carry_sc[...] = jnp.zeros(carry_sc.shape, F32)

    xf = x_ref[...]
    ms = jnp.mean(xf * xf, axis=-1, keepdims=True)
    h = (xf * lax.rsqrt(ms + EPS) * nw_ref[...]) * (1.0 + sc_ref[0]) + sh_ref[0]
    h_ref[...] = h
    tm = h.shape[0]
    lane = lax.broadcasted_iota(I32, (tm, LANES), 1)
    logits = jnp.where(lane < N_EXPERTS, _dot_hi(h, wr_ref[...]) + br_ref[...], -jnp.inf)
    vals, idxs, hots = [], [], []
    l = logits
    for _ in range(TOP_K):
        m = jnp.max(l, axis=1, keepdims=True)
        idx = jnp.min(jnp.where(l == m, lane, LANES), axis=1, keepdims=True)
        hot = lane == idx
        vals.append(m)
        idxs.append(idx)
        hots.append(hot)
        l = jnp.where(hot, -jnp.inf, l)
    es = [jnp.exp(v - vals[0]) for v in vals]
    den = es[0] + es[1] + es[2] + es[3]
    hot_all = jnp.zeros((tm, LANES), F32)
    for hot in hots:
        hot_all = hot_all + hot.astype(F32)
    ii = lax.broadcasted_iota(I32, (tm, tm), 0)
    jj = lax.broadcasted_iota(I32, (tm, tm), 1)
    before = _dot((jj < ii).astype(BF16), hot_all.astype(BF16))
    cnt = jnp.sum(hot_all, axis=0, keepdims=True)
    cntp = jnp.floor((cnt + (SEG_ALIGN - 1.0)) * (1.0 / SEG_ALIGN)) * SEG_ALIGN
    ei = lax.broadcasted_iota(I32, (LANES, LANES), 0)
    ej = lax.broadcasted_iota(I32, (LANES, LANES), 1)
    units = jnp.broadcast_to(cntp * (1.0 / SEG_ALIGN), (8, LANES)).astype(BF16)
    seg = _dot(units, (ei < ej).astype(BF16))[0:1] * SEG_ALIGN
    pos_e = seg + before
    mi = jnp.zeros((tm, LANES), I32)
    mf = jnp.zeros((tm, LANES), F32)
    for kk in range(TOP_K):
        spos = jnp.sum(jnp.where(hots[kk], pos_e, 0.0), axis=1, keepdims=True).astype(I32)
        mi = jnp.where(lane == kk, idxs[kk], mi)
        mi = jnp.where(lane == TOP_K + kk, spos, mi)
        mf = jnp.where(lane == kk, es[kk] / den, mf)
    mi_ref[...] = mi
    mf_ref[...] = mf
    rowi = lax.broadcasted_iota(I32, (8, LANES), 0)
    info = jnp.where(rowi == 0, cntp, jnp.where(rowi == 1, seg, jnp.where(rowi == 2, carry_sc[...], 0.0)))
    blk_ref[0] = info.astype(I32)
    carry_sc[...] = carry_sc[...] + cntp
    tot_ref[...] = carry_sc[...]


def _route(x, nw, mod, wr, br, lay):
    T, D = x.shape
    B, S = lay["B"], lay["S"]
    mrow = lambda i: jnp.where(i < (B * S) // TMR, i // (S // TMR), B)
    row = lambda n: pl.BlockSpec((TMR, n), lambda i: (i, 0))
    return pl.pallas_call(
        _route_kernel,
        grid=(T // TMR,),
        in_specs=[row(D),
                  pl.BlockSpec((1, D), lambda i: (0, 0)),
                  pl.BlockSpec((1, 1, D), lambda i: (mrow(i), 0, 3)),
                  pl.BlockSpec((1, 1, D), lambda i: (mrow(i), 0, 4)),
                  pl.BlockSpec((D, LANES), lambda i: (0, 0)),
                  pl.BlockSpec((1, LANES), lambda i: (0, 0))],
        out_specs=[row(D), row(LANES), row(LANES),
                   pl.BlockSpec((1, 8, LANES), lambda i: (i, 0, 0)),
                   pl.BlockSpec((1, LANES), lambda i: (0, 0))],
        out_shape=[jax.ShapeDtypeStruct((T, D), F32), jax.ShapeDtypeStruct((T, LANES), I32),
                   jax.ShapeDtypeStruct((T, LANES), F32),
                   jax.ShapeDtypeStruct((T // TMR, 8, LANES), I32),
                   jax.ShapeDtypeStruct((1, LANES), F32)],
        scratch_shapes=[pltpu.VMEM((1, LANES), F32)],
        compiler_params=_params("arbitrary"),
        name="route",
    )(x, nw, mod, mod, wr, br)


def _segment_starts(info_ref, base, make_copy):
    def start_expert(e, carry):
        n = info_ref[base + e] // SEG_ALIGN
        src0 = info_ref[base + N_EXPERTS + e]
        dst0 = info_ref[base + 2 * N_EXPERTS + e]

        def start_chunk(c, carry):
            make_copy(pl.multiple_of(src0 + c * SEG_ALIGN, SEG_ALIGN),
                      pl.multiple_of(dst0 + c * SEG_ALIGN, SEG_ALIGN)).start()
            return carry

        lax.fori_loop(0, n, start_chunk, 0)
        return carry

    lax.fori_loop(0, N_EXPERTS, start_expert, 0)


def _segment_waits(info_ref, base, make_copy):
    total = lax.fori_loop(0, N_EXPERTS, lambda e, n: n + info_ref[base + e] // SEG_ALIGN, 0)

    def wait_chunk(c, carry):
        make_copy(0, 0).wait()
        return carry

    lax.fori_loop(0, total, wait_chunk, 0)


def _segment_copies(info_ref, base, make_copy):
    _segment_starts(info_ref, base, make_copy)
    _segment_waits(info_ref, base, make_copy)


def _dispatch_kernel(info_ref, h_ref, mi_ref, xb_ref, stg_sc, sem):
    tm = h_ref.shape[0]
    rows = stg_sc.shape[0]
    seg_copy = lambda s, d: pltpu.make_async_copy(
        stg_sc.at[pl.ds(s, SEG_ALIGN)], xb_ref.at[pl.ds(d, SEG_ALIGN)], sem)

    @pl.when(pl.program_id(0) == 0)
    def _():
        stg_sc[0:MOE_BLK, :] = jnp.zeros((MOE_BLK, stg_sc.shape[1]), BF16)
        _segment_copies(info_ref, INFO_TAIL, seg_copy)

        def spare_copy(j):
            return pltpu.make_async_copy(stg_sc.at[pl.ds(0, MOE_BLK)],
                                         xb_ref.at[pl.ds(pl.multiple_of(j * MOE_BLK, MOE_BLK), MOE_BLK)], sem)

        def start_spare(j, carry):
            spare_copy(j).start()
            return carry

        def wait_spare(j, carry):
            spare_copy(j).wait()
            return carry

        first_spare = info_ref[INFO_NUSED]
        lax.fori_loop(first_spare, xb_ref.shape[0] // MOE_BLK, start_spare, 0)
        lax.fori_loop(first_spare, xb_ref.shape[0] // MOE_BLK, wait_spare, 0)

    spos_t = mi_ref[...].astype(F32).T
    r = lax.broadcasted_iota(I32, (rows, tm), 0).astype(F32)
    pm = r == spos_t[TOP_K:TOP_K + 1]
    for kk in range(1, TOP_K):
        pm = pm | (r == spos_t[TOP_K + kk:TOP_K + kk + 1])
    stg_sc[...] = _dot(pm.astype(BF16), h_ref[...].astype(BF16)).astype(BF16)
    _segment_copies(info_ref, 0, seg_copy)


def _dispatch(info, h, mi, rows):
    T, D = h.shape
    return pl.pallas_call(
        _dispatch_kernel,
        grid=(T // TMR,),
        in_specs=[pl.BlockSpec((INFO_LANES,), lambda i: (i,), memory_space=pltpu.SMEM),
                  pl.BlockSpec((TMR, D), lambda i: (i, 0)),
                  pl.BlockSpec((TMR, LANES), lambda i: (i, 0))],
        out_specs=pl.BlockSpec(memory_space=pl.ANY),
        out_shape=jax.ShapeDtypeStruct((rows, D), BF16),
        scratch_shapes=[pltpu.VMEM((STG_ROWS, D), BF16), pltpu.SemaphoreType.DMA],
        compiler_params=_params("arbitrary"),
        name="dispatch",
    )(info, h, mi)


def _gmm_kernel(be_ref, nu_ref, nxt_ref, slot_ref, x_ref, wgu_hbm, bgu_ref, wdn_hbm, bdn_ref, o_ref,
                gu_buf, dn_buf, wgu_sc, wdn_sc, sems, *, layer):
    i = pl.program_id(0)
    e = be_ref[i]
    prev = be_ref[jnp.maximum(i - 1, 0)]

    def fetch(expert, slot):
        return (pltpu.make_async_copy(wgu_hbm.at[layer, expert], gu_buf.at[slot], sems.at[0, slot]),
                pltpu.make_async_copy(wdn_hbm.at[layer, expert], dn_buf.at[slot], sems.at[1, slot]))

    @pl.when(i == 0)
    def _():
        for cp in fetch(e, slot_ref[0]):
            cp.start()

    @pl.when(((i == 0) | (e != prev)) & (i < nu_ref[0]))
    def _():
        slot = slot_ref[i]
        for cp in fetch(e, slot):
            cp.wait()
        wgu_sc[...] = gu_buf[slot].astype(BF16)
        wdn_sc[...] = dn_buf[slot].astype(BF16)

        @pl.when(nxt_ref[i] >= 0)
        def _():
            for cp in fetch(nxt_ref[i], 1 - slot):
                cp.start()

    @pl.when(i < nu_ref[0])
    def _():
        gu = _dot(x_ref[...], wgu_sc[...]) + bgu_ref[0, 0]
        glu = jnp.minimum(gu[:, 0:D_EXPERT], SWIGLU_LIMIT)
        lin = jnp.clip(gu[:, D_EXPERT:2 * D_EXPERT], -SWIGLU_LIMIT, SWIGLU_LIMIT)
        act = glu * _sigmoid(SWIGLU_ALPHA * glu) * (lin + 1.0)
        o_ref[...] = (_dot(act.astype(BF16), wdn_sc[...]) + bdn_ref[0, 0]).astype(BF16)

    @pl.when(i >= nu_ref[0])
    def _():
        o_ref[...] = jnp.zeros(o_ref.shape, BF16)


def _gmm(block_e, n_used, nxt, slot, xb, wgu, bgu, wdn, bdn, l):
    P, D = xb.shape
    L, E, _, F2 = wgu.shape
    xmap = lambda i, be, nu, nx, sl: (jnp.minimum(i, nu[0] - 1), 0)
    bmap = lambda i, be, nu, nx, sl: (l, be[i], 0, 0)
    return pl.pallas_call(
        functools.partial(_gmm_kernel, layer=l),
        grid_spec=pltpu.PrefetchScalarGridSpec(
            num_scalar_prefetch=4,
            grid=(P // MOE_BLK,),
            in_specs=[pl.BlockSpec((MOE_BLK, D), xmap),
                      pl.BlockSpec(memory_space=pl.ANY),
                      pl.BlockSpec((1, 1, 1, F2), bmap),
                      pl.BlockSpec(memory_space=pl.ANY),
                      pl.BlockSpec((1, 1, 1, D), bmap)],
            out_specs=pl.BlockSpec((MOE_BLK, D), lambda i, be, nu, nx, sl: (i, 0)),
            scratch_shapes=[pltpu.VMEM((2, D, F2), F32), pltpu.VMEM((2, F2 // 2, D), F32),
                            pltpu.VMEM((D, F2), BF16), pltpu.VMEM((F2 // 2, D), BF16),
                            pltpu.SemaphoreType.DMA((2, 2))]),
        out_shape=jax.ShapeDtypeStruct((P, D), BF16),
        compiler_params=_params("arbitrary"),
        name="gmm",
    )(block_e, n_used, nxt, slot, xb, wgu, bgu.reshape(L, E, 1, F2), wdn, bdn.reshape(L, E, 1, D))


def _combine_kernel(info_ref, x_ref, mi_ref, gate_ref, g2_ref, fw_ref, yb_ref, o_ref, stg_sc, sem, *, final):
    tm = x_ref.shape[0]
    rows = stg_sc.shape[0]

    @pl.when(pl.program_id(0) == 0)
    def _():
        stg_sc[...] = jnp.zeros(stg_sc.shape, BF16)

    _segment_copies(info_ref, 0, lambda s, d: pltpu.make_async_copy(
        yb_ref.at[pl.ds(d, SEG_ALIGN)], stg_sc.at[pl.ds(s, SEG_ALIGN)], sem))

    spos = mi_ref[...].astype(F32)
    gate = gate_ref[...]
    r = lax.broadcasted_iota(I32, (tm, rows), 1).astype(F32)
    g = jnp.where(r == spos[:, TOP_K:TOP_K + 1], gate[:, 0:1], 0.0)
    for kk in range(1, TOP_K):
        g = g + jnp.where(r == spos[:, TOP_K + kk:TOP_K + kk + 1], gate[:, kk:kk + 1], 0.0)
    g_hi, g_lo = _split2(g)
    y = stg_sc[...]
    out = x_ref[...] + g2_ref[0] * (_dot(g_hi, y) + _dot(g_lo, y))
    if final:
        ms = jnp.mean(out * out, axis=-1, keepdims=True)
        out = out * lax.rsqrt(ms + EPS) * fw_ref[...]
    o_ref[...] = out


def _combine(info, x, mi, gates, mod, fw, yb, lay, final):
    T, D = x.shape
    B, S = lay["B"], lay["S"]
    mrow = lambda i: jnp.where(i < (B * S) // TMR, i // (S // TMR), B)
    return pl.pallas_call(
        functools.partial(_combine_kernel, final=final),
        grid=(T // TMR,),
        in_specs=[pl.BlockSpec((INFO_LANES,), lambda i: (i,), memory_space=pltpu.SMEM),
                  pl.BlockSpec((TMR, D), lambda i: (i, 0)),
                  pl.BlockSpec((TMR, LANES), lambda i: (i, 0)),
                  pl.BlockSpec((TMR, LANES), lambda i: (i, 0)),
                  pl.BlockSpec((1, 1, D), lambda i: (mrow(i), 0, 5)),
                  pl.BlockSpec((1, D), lambda i: (0, 0)),
                  pl.BlockSpec(memory_space=pl.ANY)],
        out_specs=pl.BlockSpec((TMR, D), lambda i: (i, 0)),
        out_shape=jax.ShapeDtypeStruct((T, D), F32),
        scratch_shapes=[pltpu.VMEM((STG_ROWS, D), BF16), pltpu.SemaphoreType.DMA],
        compiler_params=_params("arbitrary"),
        name="combine",
    )(info, x, mi, gates, mod, fw, yb)


def _moe(x, nw, mod, wr, br, wgu, bgu, wdn, bdn, fw, lay, l, final):
    T, D = x.shape
    ntb = T // TMR
    h, mi, mf, blk, tot = _route(x, nw, mod, wr, br, lay)
    counts = tot[0, :N_EXPERTS].astype(I32)
    padded = (counts + MOE_BLK - 1) // MOE_BLK * MOE_BLK
    pad_ends = jnp.cumsum(padded)
    pad_starts = pad_ends - padded
    nblk = -(-(T * TOP_K + ntb * N_EXPERTS * (SEG_ALIGN - 1)) // MOE_BLK) + N_EXPERTS
    starts = jnp.arange(nblk, dtype=I32) * MOE_BLK
    block_e = jnp.minimum(jnp.sum((pad_ends[None, :] <= starts[:, None]).astype(I32), axis=1), N_EXPERTS - 1)
    n_used = (pad_ends[N_EXPERTS - 1:] // MOE_BLK).astype(I32)
    rep = lambda v: jnp.broadcast_to(v[None, :], (ntb, v.shape[0]))
    zeros_e = jnp.zeros((N_EXPERTS,), I32)
    info = jnp.concatenate([
        blk[:, 0, :N_EXPERTS], blk[:, 1, :N_EXPERTS], blk[:, 2, :N_EXPERTS] + pad_starts[None, :], rep(zeros_e),
        rep(padded - counts), rep(zeros_e), rep(pad_starts + counts),
        rep(jnp.concatenate([n_used, jnp.zeros((INFO_LANES - INFO_NUSED - 1,), I32)]))], axis=1).reshape(-1)
    eid = jnp.arange(N_EXPERTS, dtype=I32)
    has = padded > 0
    later = jnp.where(has[None, :] & (eid[None, :] > eid[:, None]), eid[None, :], N_EXPERTS)
    nxt_e = jnp.min(later, axis=1)
    nxt_e = jnp.where(nxt_e < N_EXPERTS, nxt_e, -1)
    slot_e = (jnp.cumsum(has.astype(I32)) - 1) % 2
    xb = _dispatch(info, h, mi, nblk * MOE_BLK)
    yb = _gmm(block_e, n_used, jnp.take(nxt_e, block_e), jnp.take(slot_e, block_e), xb, wgu, bgu, wdn, bdn, l)
    return _combine(info, x, mi, mf, mod, fw, yb, lay, final)


def _rope_tables(S):
    t = np.arange(S)
    row = (t // GRID_W).astype(np.float64)
    col = (t % GRID_W).astype(np.float64)
    lane = np.arange(LANES)
    j = lane % 16
    inv = ROPE_THETA ** (-(j.astype(np.float32)) / np.float32(16.0))
    pos = np.where((lane % 64) < 32, row[:, None], col[:, None]).astype(np.float32)
    ang = (pos * inv.astype(np.float32)[None, :]).astype(np.float32)
    sign = np.where((lane % 32) < 16, -1.0, 1.0).astype(np.float32)
    cos_t = np.concatenate([np.cos(ang), np.ones((TM, LANES), np.float32)], axis=0)
    sin_t = np.concatenate([np.sin(ang) * sign[None, :], np.zeros((TM, LANES), np.float32)], axis=0)
    return cos_t.astype(np.float32), sin_t.astype(np.float32)


def kernel(x, c, ctx, c_ctx, w_ada, b_ada, norm1_w, w_in, da_lambda, da_subln_w, ssd_conv_w, ssd_conv_b,
           ssd_a_log, ssd_dt_bias, ssd_d, ssd_norm_w, gla_gk_up, gla_gk_b, gla_norm_w, w_out, norm2_w,
           w_router, b_router, w_gate_up, b_gate_up, w_down, b_down, final_norm_w):
    B, S, D = x.shape
    Lc = ctx.shape[1]
    depth = w_ada.shape[0]
    assert S % TMR == 0 and Lc % TM == 0 and (B * Lc) % TMR == 0 and S % GRID_W == 0
    lay = dict(B=B, S=S, Lc=Lc, nlb=(B * S) // TM, bpb=S // TM, cpb=Lc // TM)
    n_lat = B * S

    xs = jnp.concatenate([x.reshape(B * S, D), ctx.reshape(B * Lc, D)], axis=0)

    cc = jnp.zeros((8, D), F32).at[0:B].set(c).at[B].set(c_ctx)
    mod_all = _ada(cc, w_ada, b_ada)

    cos_t, sin_t = (jnp.asarray(t) for t in _rope_tables(S))

    for l in range(depth):
        last = l == depth - 1
        lam_init = 0.8 - 0.6 * math.exp(-0.3 * l)
        mod = mod_all[l, 0:B + 1].reshape(B + 1, 1, 6 * D)

        wi = w_in[l]
        misc_w = jnp.zeros((D, LANES), F32)
        misc_w = misc_w.at[:, MISC_DT:MISC_DT + 8].set(wi[:, 2304:2312])
        misc_w = misc_w.at[:, MISC_CODE:MISC_CODE + 2 * GLA_RANK].set(wi[:, 3080:3112])
        w_re = jnp.concatenate([wi[:, 0:2304], wi[:, 2312:3080], misc_w], axis=1).astype(BF16)
        a128 = jnp.zeros((1, LANES), F32).at[0, 0:8].set(-jnp.exp(ssd_a_log[l].astype(F32)).reshape(-1))
        bias128 = jnp.zeros((1, LANES), F32).at[0, 0:8].set(ssd_dt_bias[l].astype(F32).reshape(-1))
        wgk = jnp.zeros((2, LANES, GLA_KEY_WIDTH), F32)
        for d in range(2):
            wgk = wgk.at[d, MISC_CODE + d * GLA_RANK:MISC_CODE + (d + 1) * GLA_RANK, :].set(gla_gk_up[l, d])
        bgk = gla_gk_b[l].reshape(2, 1, GLA_KEY_WIDTH)
        dsk = jnp.repeat(ssd_d[l], SSD_HEAD_DIM).reshape(1, SSD_WIDTH)
        snw = ssd_norm_w[l].reshape(1, SSD_WIDTH)
        gnw = jnp.tile(gla_norm_w[l], GLA_HEADS).reshape(1, GLA_WIDTH)
        wr = jnp.zeros((D, LANES), F32).at[:, 0:N_EXPERTS].set(w_router[l])
        br = jnp.zeros((1, LANES), F32).at[0, 0:N_EXPERTS].set(b_router[l])

        q, k, v, z, xbc_act, gla_in, misc = _inproj(xs, norm1_w[l].reshape(1, D), mod, cos_t, sin_t, w_re,
                                                    ssd_conv_w[l], ssd_conv_b[l].reshape(1, SSD_CONV_CH), lay)

        a = _attention(q, k, v, da_lambda[l], da_subln_w[l].reshape(1, DA_V), lay, lam_init,
                       ctx_queries=not last)
        yf, yb = _ssd(xbc_act, misc, a128, bias128, lay)
        of, ob = _gla(gla_in, misc, wgk, bgk, lay)

        nrows = n_lat if last else xs.shape[0]
        xs = _outproj(xs, a, yf, yb, xbc_act, z, of, ob, gla_in, dsk, snw, gnw,
                      w_out[l].astype(BF16), mod, lay, nrows)
        xs = _moe(xs, norm2_w[l].reshape(1, D), mod, wr, br, w_gate_up, b_gate_up, w_down, b_down,
                  final_norm_w.reshape(1, D), lay, l, last)

    return xs.reshape(B, S, D)
```

```python
import functools
import math

import numpy as np
import jax
import jax.numpy as jnp
from jax import lax
from jax.experimental import pallas as pl
from jax.experimental.pallas import tpu as pltpu

F32 = jnp.float32
BF16 = jnp.bfloat16
I32 = jnp.int32

GRID_W = 64
EPS = 1e-6
DA_HEADS = 4
DA_QK = 64
DA_V = 128
DA_WIDTH = 512
ROPE_THETA = 10000.0
SSD_HEADS = 4
SSD_HEAD_DIM = 64
SSD_WIDTH = 256
SSD_STATE = 64
SSD_CONV_CH = 512
GLA_HEADS = 4
GLA_DK = 32
GLA_DV = 64
GLA_KEY_WIDTH = 128
GLA_WIDTH = 256
GLA_RANK = 16
GLA_NORMALIZER = 16.0
N_EXPERTS = 32
TOP_K = 4
D_EXPERT = 1024
SWIGLU_LIMIT = 7.0
SWIGLU_ALPHA = 1.702

LANES = 128
TM = 256
SSD_Q = 128
GLA_Q = 64
GLA_SUB = 4
ATT_TQ = 256
ATT_KT = (2816, 768, 512, 256)
LOG2E = 1.4426950408889634
MOE_BLK = 256
TMR = 512
SEG_ALIGN = 16
STG_ROWS = TMR * TOP_K + N_EXPERTS * SEG_ALIGN
INFO_LANES = 256
INFO_TAIL = 128
INFO_NUSED = 224
VMEM_LIMIT = 56 * 1024 * 1024

C_Q, C_K, C_V, C_Z, C_XBC, C_GLA, C_MISC, C_END = 0, 512, 1024, 1536, 1792, 2304, 3072, 3200
MISC_DT = 0
MISC_CODE = 8


def _sigmoid(x):
    return 1.0 / (1.0 + jnp.exp(-x))


def _softplus(x):
    return jnp.maximum(x, 0.0) + jnp.log(1.0 + jnp.exp(-jnp.abs(x)))


def _split2(a):
    hi = a.astype(BF16)
    lo = (a - hi.astype(F32)).astype(BF16)
    return hi, lo


def _split3(a):
    a1 = a.astype(BF16)
    r1 = a - a1.astype(F32)
    a2 = r1.astype(BF16)
    a3 = (r1 - a2.astype(F32)).astype(BF16)
    return a1, a2, a3


def _dot(a, b):
    return jnp.dot(a, b, preferred_element_type=F32)


def _dot_hi(a, b):
    a1, a2 = _split2(a)
    b1, b2 = _split2(b)
    return _dot(a1, b1) + (_dot(a1, b2) + _dot(a2, b1))


def _dot_exact_lhs(m, a):
    a1, a2, a3 = _split3(a)
    mb = m.astype(BF16)
    return _dot(mb, a1) + (_dot(mb, a2) + _dot(mb, a3))


def _dot_exact_rhs(a, m):
    a1, a2, a3 = _split3(a)
    mb = m.astype(BF16)
    return _dot(a1, mb) + (_dot(a2, mb) + _dot(a3, mb))


def _params(*sem):
    return pltpu.CompilerParams(dimension_semantics=sem, vmem_limit_bytes=VMEM_LIMIT)


def _ada_kernel(c_ref, w_ref, b_ref, o_ref):
    c = c_ref[...]
    s = c * _sigmoid(c)
    o_ref[0] = _dot_hi(s, w_ref[0]) + b_ref[0]


def _ada(cc, w_ada, b_ada):
    L, D, N = w_ada.shape
    tn = 512
    return pl.pallas_call(
        _ada_kernel,
        grid=(L, N // tn),
        in_specs=[pl.BlockSpec((8, D), lambda l, j: (0, 0)),
                  pl.BlockSpec((1, D, tn), lambda l, j: (l, 0, j)),
                  pl.BlockSpec((1, 1, tn), lambda l, j: (l, 0, j))],
        out_specs=pl.BlockSpec((1, 8, tn), lambda l, j: (l, 0, j)),
        out_shape=jax.ShapeDtypeStruct((L, 8, N), F32),
        compiler_params=_params("parallel", "parallel"),
        name="ada",
    )(cc, w_ada, b_ada.reshape(L, 1, N))


def _inproj_kernel(x_ref, xp_ref, xn_ref, nw_ref, sh_ref, sc_ref, cos_ref, sin_ref, w_ref, cw_ref, cb_ref,
                   q_ref, k_ref, v_ref, z_ref, xbc_ref, gla_ref, misc_ref, *, nlb, bpb, cpb):
    def modulated(xf):
        ms = jnp.mean(xf * xf, axis=-1, keepdims=True)
        return ((xf * lax.rsqrt(ms + EPS) * nw_ref[...]) * (1.0 + sc_ref[0]) + sh_ref[0]).astype(BF16)

    hb = modulated(x_ref[...])
    cos = cos_ref[...]
    sin = sin_ref[...]
    lane = lax.broadcasted_iota(I32, cos.shape, 1)
    first = (lane % 32) < 16

    def rope(p):
        outs = []
        for hd in range(DA_HEADS):
            ph = p[:, hd * LANES:(hd + 1) * LANES]
            sw = jnp.where(first, pltpu.roll(ph, LANES - 16, 1), pltpu.roll(ph, 16, 1))
            outs.append(ph * cos + sw * sin)
        return jnp.concatenate(outs, axis=1)

    q = rope(_dot(hb, w_ref[:, C_Q:C_K])) * (DA_QK ** -0.5 * LOG2E)
    q_ref[...] = q.astype(BF16)
    k_ref[...] = rope(_dot(hb, w_ref[:, C_K:C_V])).astype(BF16)
    v_ref[...] = _dot(hb, w_ref[:, C_V:C_Z]).astype(BF16)
    z_ref[...] = _dot(hb, w_ref[:, C_Z:C_XBC])
    gla_ref[...] = _dot(hb, w_ref[:, C_GLA:C_MISC])
    misc_ref[...] = _dot(hb, w_ref[:, C_MISC:C_END])

    xbc = _dot(hb, w_ref[:, C_XBC:C_GLA])
    halo = _dot(modulated(jnp.concatenate([xp_ref[...], xn_ref[...]], axis=0)), w_ref[:, C_XBC:C_GLA])
    i = pl.program_id(0)
    pos = jnp.where(i < nlb, i % bpb, (i - nlb) % cpb)
    last = jnp.where(i < nlb, bpb - 1, cpb - 1)
    prev_row = halo[7:8, :] * (pos != 0).astype(F32)
    next_row = halo[8:9, :] * (pos != last).astype(F32)
    ridx = lax.broadcasted_iota(I32, xbc.shape, 0)
    xm1 = jnp.where(ridx == 0, prev_row, pltpu.roll(xbc, 1, 0))
    xp1 = jnp.where(ridx == xbc.shape[0] - 1, next_row, pltpu.roll(xbc, xbc.shape[0] - 1, 0))
    cw = cw_ref[...]
    y = cw[0:1] * xm1 + cw[1:2] * xbc + cw[2:3] * xp1 + cb_ref[...]
    xbc_ref[...] = y * _sigmoid(y)


def _inproj(x, nw, mod, cos_t, sin_t, w, cw, cb, lay):
    T, D = x.shape
    nlb, bpb, B = lay["nlb"], lay["bpb"], lay["B"]
    r8 = TM // 8
    mrow = lambda i: jnp.where(i < nlb, i // bpb, B)
    prow = lambda i: jnp.where(i < nlb, i % bpb, bpb)
    row = lambda n: pl.BlockSpec((TM, n), lambda i: (i, 0))
    spb = bpb + lay["cpb"]
    kvrow = lambda i: (jnp.where(i < nlb, (i // bpb) * spb + lay["cpb"] + i % bpb,
                                 ((i - nlb) // lay["cpb"]) * spb + (i - nlb) % lay["cpb"]), 0)
    kv = pl.BlockSpec((TM, DA_WIDTH), kvrow)
    outs = [(512, BF16), (512, BF16), (512, BF16), (256, F32), (512, F32), (768, F32), (128, F32)]
    out_specs = [row(n) for n, _ in outs]
    out_specs[1] = kv
    out_specs[2] = kv
    return pl.pallas_call(
        functools.partial(_inproj_kernel, nlb=nlb, bpb=bpb, cpb=lay["cpb"]),
        grid=(T // TM,),
        in_specs=[row(D),
                  pl.BlockSpec((8, D), lambda i: (jnp.maximum(i * r8 - 1, 0), 0)),
                  pl.BlockSpec((8, D), lambda i: (jnp.minimum((i + 1) * r8, T // 8 - 1), 0)),
                  pl.BlockSpec((1, D), lambda i: (0, 0)),
                  pl.BlockSpec((1, 1, D), lambda i: (mrow(i), 0, 0)),
                  pl.BlockSpec((1, 1, D), lambda i: (mrow(i), 0, 1)),
                  pl.BlockSpec((TM, LANES), lambda i: (prow(i), 0)),
                  pl.BlockSpec((TM, LANES), lambda i: (prow(i), 0)),
                  pl.BlockSpec((D, C_END), lambda i: (0, 0)),
                  pl.BlockSpec((3, SSD_CONV_CH), lambda i: (0, 0)),
                  pl.BlockSpec((1, SSD_CONV_CH), lambda i: (0, 0))],
        out_specs=out_specs,
        out_shape=[jax.ShapeDtypeStruct((T, n), dt) for n, dt in outs],
        compiler_params=_params("parallel"),
        name="inproj",
    )(x, x, x, nw, mod, mod, cos_t, sin_t, w, cw, cb)


def _lane_fold(x, op):
    f = x[:, 0:LANES]
    for i in range(1, x.shape[1] // LANES):
        f = op(f, x[:, i * LANES:(i + 1) * LANES])
    return f


def _attn_stack_maps(q_ref, q2_sc):
    tq = q_ref.shape[0]
    q = q_ref[...]
    lane = lax.broadcasted_iota(I32, q.shape, 1)
    zero = jnp.zeros_like(q)
    q2_sc[0:tq, :] = jnp.where(lane < DA_QK, q, zero)
    q2_sc[tq:2 * tq, :] = jnp.where(lane >= DA_QK, q, zero)


def _attn_scores_tile(q2_sc, k_ref, s_sc, slot, mf_sc, j, kt):
    kj = k_ref[pl.ds(pl.multiple_of(j * kt, kt), kt), :]
    s = lax.dot_general(q2_sc[...], kj, (((1,), (1,)), ((), ())), preferred_element_type=F32)
    s_sc[slot, j, :, 0:kt] = s
    mf_sc[...] = jnp.maximum(mf_sc[...], _lane_fold(s, jnp.maximum))


def _attn_weigh_tile(v_ref, s_sc, slot, p_sc, l_sc, acc_sc, m, j, kt):
    sub = next(w for w in (512, 256) if kt % w == 0)
    for c0 in range(0, kt, sub):
        p = jnp.exp2(s_sc[slot, j, :, c0:c0 + sub] - m)
        l_sc[...] += _lane_fold(p, jnp.add)
        p_sc[:, c0:c0 + sub] = p.astype(BF16)
    vj = v_ref[pl.ds(pl.multiple_of(j * kt, kt), kt), :]
    acc_sc[...] += _dot(p_sc[:, 0:kt], vj)


def _attn_finish(lam_ref, w_ref, o_ref, l_sc, acc_sc, lam_init):
    tq = o_ref.shape[0]
    lp = lam_ref[...]
    lam = (jnp.exp(jnp.sum(lp[0:1] * lp[1:2], axis=1, keepdims=True))
           - jnp.exp(jnp.sum(lp[2:3] * lp[3:4], axis=1, keepdims=True)) + lam_init)
    acc = acc_sc[...]
    l = jnp.sum(l_sc[...], axis=1, keepdims=True)
    o = acc[0:tq] / l[0:tq] - lam * (acc[tq:2 * tq] / l[tq:2 * tq])
    ms = jnp.mean(o * o, axis=-1, keepdims=True)
    o_ref[...] = (o * lax.rsqrt(ms + EPS) * w_ref[...]) * (1.0 - lam_init)


def _attn_kernel(q_ref, k_ref, v_ref, lam_ref, w_ref, o_ref, q2_sc, s_sc, p_sc, mf_sc, l_sc, acc_sc,
                 *, nt, kt, lam_init):
    _attn_stack_maps(q_ref, q2_sc)
    mf_sc[...] = jnp.full(mf_sc.shape, -jnp.inf, F32)
    l_sc[...] = jnp.zeros(l_sc.shape, F32)
    acc_sc[...] = jnp.zeros(acc_sc.shape, F32)

    def scores(j, carry):
        _attn_scores_tile(q2_sc, k_ref, s_sc, 0, mf_sc, j, kt)
        return carry

    lax.fori_loop(0, nt, scores, 0)
    m = jnp.max(mf_sc[...], axis=1, keepdims=True)

    def weigh(j, carry):
        _attn_weigh_tile(v_ref, s_sc, 0, p_sc, l_sc, acc_sc, m, j, kt)
        return carry

    lax.fori_loop(0, nt, weigh, 0)
    _attn_finish(lam_ref, w_ref, o_ref, l_sc, acc_sc, lam_init)


def _attn_pipe_kernel(q_ref, k_ref, v_ref, lam_ref, w_ref, o_ref, q2_sc, s_sc, p_sc, mf_sc, m_sc, l_sc, acc_sc,
                      *, nq, nt, kt, lam_init):
    t = pl.program_id(2)

    @pl.when(t < nq)
    def _():
        _attn_stack_maps(q_ref, q2_sc)
        mf_sc[...] = jnp.full(mf_sc.shape, -jnp.inf, F32)

    @pl.when(t > 0)
    def _():
        l_sc[...] = jnp.zeros(l_sc.shape, F32)
        acc_sc[...] = jnp.zeros(acc_sc.shape, F32)

    def run(slot_a, slot_b):
        m_prev = None if slot_b is None else m_sc[slot_b]

        def body(j, carry):
            if slot_b is not None:
                _attn_weigh_tile(v_ref, s_sc, slot_b, p_sc, l_sc, acc_sc, m_prev, j, kt)
            if slot_a is not None:
                _attn_scores_tile(q2_sc, k_ref, s_sc, slot_a, mf_sc, j, kt)
            return carry

        lax.fori_loop(0, nt, body, 0)
        if slot_a is not None:
            m_sc[slot_a] = jnp.max(mf_sc[...], axis=1, keepdims=True)

    inner = (t > 0) & (t < nq)
    pl.when(t == 0)(lambda: run(0, None))
    pl.when(inner & (t % 2 == 1))(lambda: run(1, 0))
    pl.when(inner & (t % 2 == 0))(lambda: run(0, 1))
    pl.when(t == nq)(lambda: run(None, (nq - 1) % 2))

    @pl.when(t > 0)
    def _():
        _attn_finish(lam_ref, w_ref, o_ref, l_sc, acc_sc, lam_init)


def _attention(q, k, v, lam_p, subln_w, lay, lam_init, *, ctx_queries):
    B, S, Lc = lay["B"], lay["S"], lay["Lc"]
    nkeys = Lc + S
    assert nkeys % Lc == 0
    small = [pl.BlockSpec((4, DA_QK), lambda b, h, t: (0, 0)), pl.BlockSpec((1, DA_V), lambda b, h, t: (0, 0))]
    stacked = lambda n, dt: pltpu.VMEM((2 * ATT_TQ, n), dt)

    kt = next(t for t in ATT_KT if nkeys % t == 0)
    nq = S // ATT_TQ
    kmap = lambda b, h, t: (b, h)
    a_lat = pl.pallas_call(
        functools.partial(_attn_pipe_kernel, nq=nq, nt=nkeys // kt, kt=kt, lam_init=lam_init),
        grid=(B, DA_HEADS, nq + 1),
        in_specs=[pl.BlockSpec((ATT_TQ, LANES), lambda b, h, t: (b * nq + jnp.minimum(t, nq - 1), h)),
                  pl.BlockSpec((nkeys, LANES), kmap),
                  pl.BlockSpec((nkeys, LANES), kmap)] + small,
        out_specs=pl.BlockSpec((ATT_TQ, LANES), lambda b, h, t: (b * nq + jnp.maximum(t - 1, 0), h)),
        out_shape=jax.ShapeDtypeStruct((B * S, DA_WIDTH), F32),
        scratch_shapes=[stacked(LANES, BF16),
                        pltpu.VMEM((2, nkeys // kt, 2 * ATT_TQ, kt), F32),
                        stacked(kt, BF16), stacked(LANES, F32),
                        pltpu.VMEM((2, 2 * ATT_TQ, 1), F32),
                        stacked(LANES, F32), stacked(LANES, F32)],
        compiler_params=_params("parallel", "parallel", "arbitrary"),
        name="attn",
    )(q, k, v, lam_p, subln_w)
    if not ctx_queries:
        return a_lat

    ktc = next(t for t in ATT_KT if Lc % t == 0)
    nqc = Lc // ATT_TQ
    cmap = lambda b, h, t: (b * (nkeys // Lc), h)
    a_ctx = pl.pallas_call(
        functools.partial(_attn_kernel, nt=Lc // ktc, kt=ktc, lam_init=lam_init),
        grid=(B, DA_HEADS, nqc),
        in_specs=[pl.BlockSpec((ATT_TQ, LANES), lambda b, h, t: ((B * S) // ATT_TQ + b * nqc + t, h)),
                  pl.BlockSpec((Lc, LANES), cmap),
                  pl.BlockSpec((Lc, LANES), cmap)] + small,
        out_specs=pl.BlockSpec((ATT_TQ, LANES), lambda b, h, t: (b * nqc + t, h)),
        out_shape=jax.ShapeDtypeStruct((B * Lc, DA_WIDTH), F32),
        scratch_shapes=[stacked(LANES, BF16),
                        pltpu.VMEM((1, Lc // ktc, 2 * ATT_TQ, ktc), F32),
                        stacked(ktc, BF16), stacked(LANES, F32), stacked(LANES, F32), stacked(LANES, F32)],
        compiler_params=_params("parallel", "parallel", "arbitrary"),
        name="attn_ctx",
    )(q, k, v, lam_p, subln_w)
    return jnp.concatenate([a_lat, a_ctx], axis=0)


def _ssd_dir(xbc, misc, a128, bias128, hs_ref, dirn, rev):
    Q = xbc.shape[0]
    dt = _softplus(misc + bias128)
    a = dt * a128
    ii = lax.broadcasted_iota(I32, (Q, Q), 0)
    jj = lax.broadcasted_iota(I32, (Q, Q), 1)
    M = (jj >= ii) if rev else (jj <= ii)
    acs = _dot_exact_lhs(M.astype(F32), a)
    acs_t = acs.T
    dt_t = dt.T
    xs = xbc[:, 0:SSD_WIDTH]
    Bt = xbc[:, SSD_WIDTH:SSD_WIDTH + 128].T
    Cm = xbc[:, SSD_WIDTH + 128:SSD_WIDTH + 256]
    last = 0 if rev else Q - 1
    ys = []
    for g in range(2):
        Cg = Cm[:, g * SSD_STATE:(g + 1) * SSD_STATE]
        Btg = Bt[g * SSD_STATE:(g + 1) * SSD_STATE, :]
        CB = _dot(Cg, Btg)
        for hh in range(2):
            h = 2 * g + hh
            li = dirn * SSD_HEADS + h
            acs_c = acs[:, li:li + 1]
            acs_r = acs_t[li:li + 1, :]
            dt_r = dt_t[li:li + 1, :]
            decay = jnp.exp(jnp.where(M, acs_c - acs_r, -jnp.inf))
            sc = CB * decay * dt_r
            x_h = xs[:, h * SSD_HEAD_DIM:(h + 1) * SSD_HEAD_DIM]
            hs = hs_ref[h]
            y = _dot(sc, x_h) + _dot(Cg * jnp.exp(acs_c), hs)
            tot = acs_r[:, last:last + 1]
            w_end = jnp.exp(tot - acs_r) * dt_r
            hs_ref[h] = jnp.exp(tot) * hs + _dot(Btg * w_end, x_h)
            ys.append(y)
    return jnp.concatenate(ys, axis=1)


def _scan_maps(lay, rows):
    ncc, ncl = lay["Lc"] // rows, lay["S"] // rows
    base = (lay["B"] * lay["S"]) // rows
    fmap = lambda b, c: (jnp.where(c < ncc, base + b * ncc + c, b * ncl + c - ncc), 0)
    bmap = lambda b, c: (jnp.where(c < ncc, base + b * ncc + (ncc - 1 - c),
                                   b * ncl + (ncl - 1 - (c - ncc))), 0)
    return ncc + ncl, fmap, bmap


def _gla_block(blk, misc, wgk, bgk, s_ref, rev):
    R = blk.shape[0]
    kw = GLA_KEY_WIDTH
    q, k, v = blk[:, 0:kw], blk[:, kw:2 * kw], blk[:, 2 * kw:2 * kw + GLA_WIDTH]
    pre = _dot_hi(misc, wgk) + bgk
    gk = (jnp.minimum(pre, 0.0) - jnp.log(1.0 + jnp.exp(-jnp.abs(pre)))) / GLA_NORMALIZER
    ii = lax.broadcasted_iota(I32, (R, R), 0)
    jj = lax.broadcasted_iota(I32, (R, R), 1)
    same = (ii // GLA_Q) == (jj // GLA_Q)
    M = same & ((jj >= ii) if rev else (jj <= ii))
    b = _dot_exact_lhs(M.astype(F32), gk)
    tot = _dot_exact_lhs(same.astype(F32), gk)
    q_t = (q * (GLA_DK ** -0.5)) * jnp.exp(b)
    k_t = k * jnp.exp(-b)
    k_end_t = (k * jnp.exp(tot - b)).T
    gk_t = gk.T
    lane_k = lax.broadcasted_iota(I32, (1, kw), 1) // GLA_DK
    lane_v = lax.broadcasted_iota(I32, (1, GLA_WIDTH), 1) // GLA_DV
    o = jnp.zeros((R, GLA_WIDTH), F32)
    for h in range(GLA_HEADS):
        qh = jnp.where(lane_k == h, q_t, 0.0)
        att = lax.dot_general(qh, k_t, (((1,), (1,)), ((), ())), preferred_element_type=F32)
        o = o + _dot(jnp.where(M, att, 0.0), jnp.where(lane_v == h, v, 0.0))
    col_chunk = lax.broadcasted_iota(I32, (1, R), 1) // GLA_Q
    S = s_ref[...]
    bd = (lax.broadcasted_iota(I32, S.shape, 0) // GLA_DK) == (lax.broadcasted_iota(I32, S.shape, 1) // GLA_DV)
    nsub = R // GLA_Q
    inter = [None] * nsub
    for c in (range(nsub - 1, -1, -1) if rev else range(nsub)):
        sel = col_chunk == c
        inter[c] = _dot(q_t[c * GLA_Q:(c + 1) * GLA_Q], S)
        decay = jnp.exp(jnp.sum(jnp.where(sel, gk_t, 0.0), axis=1, keepdims=True))
        S = jnp.where(bd, S * decay + _dot(jnp.where(sel, k_end_t, 0.0), v), 0.0)
    s_ref[...] = S
    return o + jnp.concatenate(inter, axis=0)


def _scan_kernel(xf_ref, gf_ref, mf_ref, xb_ref, gb_ref, mb_ref, a_ref, bias_ref, wgk_ref, bgk_ref,
                 yf_ref, yb_ref, of_ref, ob_ref, hf_sc, hb_sc, sf_sc, sb_sc):
    @pl.when(pl.program_id(1) == 0)
    def _():
        for sc in (hf_sc, hb_sc, sf_sc, sb_sc):
            sc[...] = jnp.zeros(sc.shape, F32)

    a128 = a_ref[...]
    bias128 = bias_ref[...]
    mf = mf_ref[...]
    mb = mb_ref[...]
    nsub = xf_ref.shape[0] // SSD_Q
    for c in range(nsub):
        r0 = c * SSD_Q
        yf_ref[r0:r0 + SSD_Q, :] = _ssd_dir(xf_ref[r0:r0 + SSD_Q, :], mf[r0:r0 + SSD_Q], a128, bias128,
                                            hf_sc, 0, False)
        r1 = (nsub - 1 - c) * SSD_Q
        yb_ref[r1:r1 + SSD_Q, :] = _ssd_dir(xb_ref[r1:r1 + SSD_Q, :], mb[r1:r1 + SSD_Q], a128, bias128,
                                            hb_sc, 1, True)
    of_ref[...] = _gla_block(gf_ref[...], mf, wgk_ref[0], bgk_ref[0], sf_sc, False)
    ob_ref[...] = _gla_block(gb_ref[...], mb, wgk_ref[1], bgk_ref[1], sb_sc, True)


def _scans(xbc_act, gla_in, misc, a128, bias128, wgk, bgk, lay):
    T = xbc_act.shape[0]
    R = GLA_Q * GLA_SUB
    assert R % SSD_Q == 0
    nc, fmap, bmap = _scan_maps(lay, R)
    vec = pl.BlockSpec((1, LANES), lambda b, c: (0, 0))
    ins = lambda m: [pl.BlockSpec((R, SSD_CONV_CH), m), pl.BlockSpec((R, 512), m), pl.BlockSpec((R, LANES), m)]
    outs = lambda n: [pl.BlockSpec((R, n), fmap), pl.BlockSpec((R, n), bmap)]
    return pl.pallas_call(
        _scan_kernel,
        grid=(lay["B"], nc),
        in_specs=ins(fmap) + ins(bmap) + [
            vec, vec,
            pl.BlockSpec((2, LANES, GLA_KEY_WIDTH), lambda b, c: (0, 0, 0)),
            pl.BlockSpec((2, 1, GLA_KEY_WIDTH), lambda b, c: (0, 0, 0))],
        out_specs=outs(SSD_WIDTH) + outs(GLA_WIDTH),
        out_shape=[jax.ShapeDtypeStruct((T, SSD_WIDTH), F32)] * 2 + [jax.ShapeDtypeStruct((T, GLA_WIDTH), F32)] * 2,
        scratch_shapes=[pltpu.VMEM((SSD_HEADS, SSD_STATE, SSD_HEAD_DIM), F32)] * 2
        + [pltpu.VMEM((GLA_KEY_WIDTH, GLA_WIDTH), F32)] * 2,
        compiler_params=_params("parallel", "arbitrary"),
        name="scans",
    )(xbc_act, gla_in, misc, xbc_act, gla_in, misc, a128, bias128, wgk, bgk)


def _outproj_kernel(x_ref, a_ref, yf_ref, yb_ref, xs_ref, z_ref, of_ref, ob_ref, g_ref,
                    dsk_ref, snw_ref, gnw_ref, w_ref, g1_ref, o_ref):
    y = yf_ref[...] + yb_ref[...] + dsk_ref[...] * xs_ref[...]
    z = z_ref[...]
    gs = y * (z * _sigmoid(z))
    half = SSD_WIDTH // 2
    parts = []
    for grp in range(2):
        seg = gs[:, grp * half:(grp + 1) * half]
        parts.append(seg * lax.rsqrt(jnp.mean(seg * seg, axis=-1, keepdims=True) + EPS))
    s = jnp.concatenate(parts, axis=1) * snw_ref[...]
    o = of_ref[...] + ob_ref[...]
    bd = ((lax.broadcasted_iota(I32, (GLA_WIDTH, GLA_WIDTH), 0) // GLA_DV)
          == (lax.broadcasted_iota(I32, (GLA_WIDTH, GLA_WIDTH), 1) // GLA_DV))
    ms = _dot_exact_rhs(o * o, jnp.where(bd, 1.0 / GLA_DV, 0.0))
    g = g_ref[...]
    c = (o * lax.rsqrt(ms + EPS) * gnw_ref[...]) * (g * _sigmoid(g))
    mix = jnp.concatenate([a_ref[...], s, c], axis=1).astype(BF16)
    o_ref[...] = x_ref[...] + g1_ref[0] * _dot(mix, w_ref[...])


def _outproj(x, a, yf, yb, xbc_act, z, of, ob, gla_in, dsk, snw, gnw, w_out, mod, lay, nrows):
    D = x.shape[1]
    nlb, bpb, B = lay["nlb"], lay["bpb"], lay["B"]
    mrow = lambda i: jnp.where(i < nlb, i // bpb, B)
    row = lambda n, cb=0: pl.BlockSpec((TM, n), lambda i: (i, cb))
    vec = lambda n: pl.BlockSpec((1, n), lambda i: (0, 0))
    return pl.pallas_call(
        _outproj_kernel,
        grid=(nrows // TM,),
        in_specs=[row(D), row(DA_WIDTH), row(SSD_WIDTH), row(SSD_WIDTH), row(SSD_WIDTH), row(SSD_WIDTH),
                  row(GLA_WIDTH), row(GLA_WIDTH), row(GLA_WIDTH, 2),
                  vec(SSD_WIDTH), vec(SSD_WIDTH), vec(GLA_WIDTH),
                  pl.BlockSpec((D, D), lambda i: (0, 0)),
                  pl.BlockSpec((1, 1, D), lambda i: (mrow(i), 0, 2))],
        out_specs=row(D),
        out_shape=jax.ShapeDtypeStruct((nrows, D), F32),
        compiler_params=_params("parallel"),
        name="outproj",
    )(x, a, yf, yb, xbc_act, z, of, ob, gla_in, dsk, snw, gnw, w_out, mod)


def _route_kernel(x_ref, nw_ref, sh_ref, sc_ref, wr_ref, br_ref,
                  h_ref, mi_ref, mf_ref, blk_ref, tot_ref, carry_sc):
    i = pl.program_id(0)

    @pl.when(i == 0)
    def _():
        carry_sc[...] = jnp.zeros(carry_sc.shape, F32)

    xf = x_ref[...]
    ms = jnp.mean(xf * xf, axis=-1, keepdims=True)
    h = (xf * lax.rsqrt(ms + EPS) * nw_ref[...]) * (1.0 + sc_ref[0]) + sh_ref[0]
    h_ref[...] = h
    tm = h.shape[0]
    lane = lax.broadcasted_iota(I32, (tm, LANES), 1)
    logits = jnp.where(lane < N_EXPERTS, _dot_hi(h, wr_ref[...]) + br_ref[...], -jnp.inf)
    vals, idxs, hots = [], [], []
    l = logits
    for _ in range(TOP_K):
        m = jnp.max(l, axis=1, keepdims=True)
        idx = jnp.min(jnp.where(l == m, lane, LANES), axis=1, keepdims=True)
        hot = lane == idx
        vals.append(m)
        idxs.append(idx)
        hots.append(hot)
        l = jnp.where(hot, -jnp.inf, l)
    es = [jnp.exp(v - vals[0]) for v in vals]
    den = es[0] + es[1] + es[2] + es[3]
    hot_all = jnp.zeros((tm, LANES), F32)
    for hot in hots:
        hot_all = hot_all + hot.astype(F32)
    ii = lax.broadcasted_iota(I32, (tm, tm), 0)
    jj = lax.broadcasted_iota(I32, (tm, tm), 1)
    before = _dot((jj < ii).astype(BF16), hot_all.astype(BF16))
    cnt = jnp.sum(hot_all, axis=0, keepdims=True)
    cntp = jnp.floor((cnt + (SEG_ALIGN - 1.0)) * (1.0 / SEG_ALIGN)) * SEG_ALIGN
    ei = lax.broadcasted_iota(I32, (LANES, LANES), 0)
    ej = lax.broadcasted_iota(I32, (LANES, LANES), 1)
    units = jnp.broadcast_to(cntp * (1.0 / SEG_ALIGN), (8, LANES)).astype(BF16)
    seg = _dot(units, (ei < ej).astype(BF16))[0:1] * SEG_ALIGN
    pos_e = seg + before
    mi = jnp.zeros((tm, LANES), I32)
    mf = jnp.zeros((tm, LANES), F32)
    for kk in range(TOP_K):
        spos = jnp.sum(jnp.where(hots[kk], pos_e, 0.0), axis=1, keepdims=True).astype(I32)
        mi = jnp.where(lane == kk, idxs[kk], mi)
        mi = jnp.where(lane == TOP_K + kk, spos, mi)
        mf = jnp.where(lane == kk, es[kk] / den, mf)
    mi_ref[...] = mi
    mf_ref[...] = mf
    rowi = lax.broadcasted_iota(I32, (8, LANES), 0)
    info = jnp.where(rowi == 0, cntp, jnp.where(rowi == 1, seg, jnp.where(rowi == 2, carry_sc[...], 0.0)))
    blk_ref[0] = info.astype(I32)
    carry_sc[...] = carry_sc[...] + cntp
    tot_ref[...] = carry_sc[...]


def _route(x, nw, mod, wr, br, lay):
    T, D = x.shape
    B, S = lay["B"], lay["S"]
    mrow = lambda i: jnp.where(i < (B * S) // TMR, i // (S // TMR), B)
    row = lambda n: pl.BlockSpec((TMR, n), lambda i: (i, 0))
    return pl.pallas_call(
        _route_kernel,
        grid=(T // TMR,),
        in_specs=[row(D),
                  pl.BlockSpec((1, D), lambda i: (0, 0)),
                  pl.BlockSpec((1, 1, D), lambda i: (mrow(i), 0, 3)),
                  pl.BlockSpec((1, 1, D), lambda i: (mrow(i), 0, 4)),
                  pl.BlockSpec((D, LANES), lambda i: (0, 0)),
                  pl.BlockSpec((1, LANES), lambda i: (0, 0))],
        out_specs=[row(D), row(LANES), row(LANES),
                   pl.BlockSpec((1, 8, LANES), lambda i: (i, 0, 0)),
                   pl.BlockSpec((1, LANES), lambda i: (0, 0))],
        out_shape=[jax.ShapeDtypeStruct((T, D), F32), jax.ShapeDtypeStruct((T, LANES), I32),
                   jax.ShapeDtypeStruct((T, LANES), F32),
                   jax.ShapeDtypeStruct((T // TMR, 8, LANES), I32),
                   jax.ShapeDtypeStruct((1, LANES), F32)],
        scratch_shapes=[pltpu.VMEM((1, LANES), F32)],
        compiler_params=_params("arbitrary"),
        name="route",
    )(x, nw, mod, mod, wr, br)


def _segment_starts(info_ref, base, make_copy):
    def start_expert(e, carry):
        n = info_ref[base + e] // SEG_ALIGN
        src0 = info_ref[base + N_EXPERTS + e]
        dst0 = info_ref[base + 2 * N_EXPERTS + e]

        def start_chunk(c, carry):
            make_copy(pl.multiple_of(src0 + c * SEG_ALIGN, SEG_ALIGN),
                      pl.multiple_of(dst0 + c * SEG_ALIGN, SEG_ALIGN)).start()
            return carry

        lax.fori_loop(0, n, start_chunk, 0)
        return carry

    lax.fori_loop(0, N_EXPERTS, start_expert, 0)


def _segment_waits(info_ref, base, make_copy):
    total = lax.fori_loop(0, N_EXPERTS, lambda e, n: n + info_ref[base + e] // SEG_ALIGN, 0)

    def wait_chunk(c, carry):
        make_copy(0, 0).wait()
        return carry

    lax.fori_loop(0, total, wait_chunk, 0)


def _segment_copies(info_ref, base, make_copy):
    _segment_starts(info_ref, base, make_copy)
    _segment_waits(info_ref, base, make_copy)


def _dispatch_kernel(info_ref, h_ref, mi_ref, xb_ref, stg_sc, sem):
    tm = h_ref.shape[0]
    rows = stg_sc.shape[0]
    seg_copy = lambda s, d: pltpu.make_async_copy(
        stg_sc.at[pl.ds(s, SEG_ALIGN)], xb_ref.at[pl.ds(d, SEG_ALIGN)], sem)

    @pl.when(pl.program_id(0) == 0)
    def _():
        stg_sc[0:MOE_BLK, :] = jnp.zeros((MOE_BLK, stg_sc.shape[1]), BF16)
        _segment_copies(info_ref, INFO_TAIL, seg_copy)

        def spare_copy(j):
            return pltpu.make_async_copy(stg_sc.at[pl.ds(0, MOE_BLK)],
                                         xb_ref.at[pl.ds(pl.multiple_of(j * MOE_BLK, MOE_BLK), MOE_BLK)], sem)

        def start_spare(j, carry):
            spare_copy(j).start()
            return carry

        def wait_spare(j, carry):
            spare_copy(j).wait()
            return carry

        first_spare = info_ref[INFO_NUSED]
        lax.fori_loop(first_spare, xb_ref.shape[0] // MOE_BLK, start_spare, 0)
        lax.fori_loop(first_spare, xb_ref.shape[0] // MOE_BLK, wait_spare, 0)

    spos_t = mi_ref[...].astype(F32).T
    r = lax.broadcasted_iota(I32, (rows, tm), 0).astype(F32)
    pm = r == spos_t[TOP_K:TOP_K + 1]
    for kk in range(1, TOP_K):
        pm = pm | (r == spos_t[TOP_K + kk:TOP_K + kk + 1])
    stg_sc[...] = _dot(pm.astype(BF16), h_ref[...].astype(BF16)).astype(BF16)
    _segment_copies(info_ref, 0, seg_copy)


def _dispatch(info, h, mi, rows):
    T, D = h.shape
    return pl.pallas_call(
        _dispatch_kernel,
        grid=(T // TMR,),
        in_specs=[pl.BlockSpec((INFO_LANES,), lambda i: (i,), memory_space=pltpu.SMEM),
                  pl.BlockSpec((TMR, D), lambda i: (i, 0)),
                  pl.BlockSpec((TMR, LANES), lambda i: (i, 0))],
        out_specs=pl.BlockSpec(memory_space=pl.ANY),
        out_shape=jax.ShapeDtypeStruct((rows, D), BF16),
        scratch_shapes=[pltpu.VMEM((STG_ROWS, D), BF16), pltpu.SemaphoreType.DMA],
        compiler_params=_params("arbitrary"),
        name="dispatch",
    )(info, h, mi)


def _gmm_kernel(be_ref, nu_ref, nxt_ref, slot_ref, x_ref, wgu_hbm, bgu_ref, wdn_hbm, bdn_ref, o_ref,
                gu_buf, dn_buf, wgu_sc, wdn_sc, sems, *, layer):
    i = pl.program_id(0)
    e = be_ref[i]
    prev = be_ref[jnp.maximum(i - 1, 0)]

    def fetch(expert, slot):
        return (pltpu.make_async_copy(wgu_hbm.at[layer, expert], gu_buf.at[slot], sems.at[0, slot]),
                pltpu.make_async_copy(wdn_hbm.at[layer, expert], dn_buf.at[slot], sems.at[1, slot]))

    @pl.when(i == 0)
    def _():
        for cp in fetch(e, slot_ref[0]):
            cp.start()

    @pl.when(((i == 0) | (e != prev)) & (i < nu_ref[0]))
    def _():
        slot = slot_ref[i]
        for cp in fetch(e, slot):
            cp.wait()
        wgu_sc[...] = gu_buf[slot].astype(BF16)
        wdn_sc[...] = dn_buf[slot].astype(BF16)

        @pl.when(nxt_ref[i] >= 0)
        def _():
            for cp in fetch(nxt_ref[i], 1 - slot):
                cp.start()

    @pl.when(i < nu_ref[0])
    def _():
        gu = _dot(x_ref[...], wgu_sc[...]) + bgu_ref[0, 0]
        glu = jnp.minimum(gu[:, 0:D_EXPERT], SWIGLU_LIMIT)
        lin = jnp.clip(gu[:, D_EXPERT:2 * D_EXPERT], -SWIGLU_LIMIT, SWIGLU_LIMIT)
        act = glu * _sigmoid(SWIGLU_ALPHA * glu) * (lin + 1.0)
        o_ref[...] = (_dot(act.astype(BF16), wdn_sc[...]) + bdn_ref[0, 0]).astype(BF16)

    @pl.when(i >= nu_ref[0])
    def _():
        o_ref[...] = jnp.zeros(o_ref.shape, BF16)


def _gmm(block_e, n_used, nxt, slot, xb, wgu, bgu, wdn, bdn, l):
    P, D = xb.shape
    L, E, _, F2 = wgu.shape
    xmap = lambda i, be, nu, nx, sl: (jnp.minimum(i, nu[0] - 1), 0)
    bmap = lambda i, be, nu, nx, sl: (l, be[i], 0, 0)
    return pl.pallas_call(
        functools.partial(_gmm_kernel, layer=l),
        grid_spec=pltpu.PrefetchScalarGridSpec(
            num_scalar_prefetch=4,
            grid=(P // MOE_BLK,),
            in_specs=[pl.BlockSpec((MOE_BLK, D), xmap),
                      pl.BlockSpec(memory_space=pl.ANY),
                      pl.BlockSpec((1, 1, 1, F2), bmap),
                      pl.BlockSpec(memory_space=pl.ANY),
                      pl.BlockSpec((1, 1, 1, D), bmap)],
            out_specs=pl.BlockSpec((MOE_BLK, D), lambda i, be, nu, nx, sl: (i, 0)),
            scratch_shapes=[pltpu.VMEM((2, D, F2), F32), pltpu.VMEM((2, F2 // 2, D), F32),
                            pltpu.VMEM((D, F2), BF16), pltpu.VMEM((F2 // 2, D), BF16),
                            pltpu.SemaphoreType.DMA((2, 2))]),
        out_shape=jax.ShapeDtypeStruct((P, D), BF16),
        compiler_params=_params("arbitrary"),
        name="gmm",
    )(block_e, n_used, nxt, slot, xb, wgu, bgu.reshape(L, E, 1, F2), wdn, bdn.reshape(L, E, 1, D))


def _combine_kernel(info_ref, x_ref, mi_ref, gate_ref, g2_ref, fw_ref, yb_ref, o_ref, stg_sc, sem, *, final):
    tm = x_ref.shape[0]
    rows = stg_sc.shape[0]

    @pl.when(pl.program_id(0) == 0)
    def _():
        stg_sc[...] = jnp.zeros(stg_sc.shape, BF16)

    _segment_copies(info_ref, 0, lambda s, d: pltpu.make_async_copy(
        yb_ref.at[pl.ds(d, SEG_ALIGN)], stg_sc.at[pl.ds(s, SEG_ALIGN)], sem))

    spos = mi_ref[...].astype(F32)
    gate = gate_ref[...]
    r = lax.broadcasted_iota(I32, (tm, rows), 1).astype(F32)
    g = jnp.where(r == spos[:, TOP_K:TOP_K + 1], gate[:, 0:1], 0.0)
    for kk in range(1, TOP_K):
        g = g + jnp.where(r == spos[:, TOP_K + kk:TOP_K + kk + 1], gate[:, kk:kk + 1], 0.0)
    g_hi, g_lo = _split2(g)
    y = stg_sc[...]
    out = x_ref[...] + g2_ref[0] * (_dot(g_hi, y) + _dot(g_lo, y))
    if final:
        ms = jnp.mean(out * out, axis=-1, keepdims=True)
        out = out * lax.rsqrt(ms + EPS) * fw_ref[...]
    o_ref[...] = out


def _combine(info, x, mi, gates, mod, fw, yb, lay, final):
    T, D = x.shape
    B, S = lay["B"], lay["S"]
    mrow = lambda i: jnp.where(i < (B * S) // TMR, i // (S // TMR), B)
    return pl.pallas_call(
        functools.partial(_combine_kernel, final=final),
        grid=(T // TMR,),
        in_specs=[pl.BlockSpec((INFO_LANES,), lambda i: (i,), memory_space=pltpu.SMEM),
                  pl.BlockSpec((TMR, D), lambda i: (i, 0)),
                  pl.BlockSpec((TMR, LANES), lambda i: (i, 0)),
                  pl.BlockSpec((TMR, LANES), lambda i: (i, 0)),
                  pl.BlockSpec((1, 1, D), lambda i: (mrow(i), 0, 5)),
                  pl.BlockSpec((1, D), lambda i: (0, 0)),
                  pl.BlockSpec(memory_space=pl.ANY)],
        out_specs=pl.BlockSpec((TMR, D), lambda i: (i, 0)),
        out_shape=jax.ShapeDtypeStruct((T, D), F32),
        scratch_shapes=[pltpu.VMEM((STG_ROWS, D), BF16), pltpu.SemaphoreType.DMA],
        compiler_params=_params("arbitrary"),
        name="combine",
    )(info, x, mi, gates, mod, fw, yb)


def _moe(x, nw, mod, wr, br, wgu, bgu, wdn, bdn, fw, lay, l, final):
    T, D = x.shape
    ntb = T // TMR
    h, mi, mf, blk, tot = _route(x, nw, mod, wr, br, lay)
    counts = tot[0, :N_EXPERTS].astype(I32)
    padded = (counts + MOE_BLK - 1) // MOE_BLK * MOE_BLK
    pad_ends = jnp.cumsum(padded)
    pad_starts = pad_ends - padded
    nblk = -(-(T * TOP_K + ntb * N_EXPERTS * (SEG_ALIGN - 1)) // MOE_BLK) + N_EXPERTS
    starts = jnp.arange(nblk, dtype=I32) * MOE_BLK
    block_e = jnp.minimum(jnp.sum((pad_ends[None, :] <= starts[:, None]).astype(I32), axis=1), N_EXPERTS - 1)
    n_used = (pad_ends[N_EXPERTS - 1:] // MOE_BLK).astype(I32)
    rep = lambda v: jnp.broadcast_to(v[None, :], (ntb, v.shape[0]))
    zeros_e = jnp.zeros((N_EXPERTS,), I32)
    info = jnp.concatenate([
        blk[:, 0, :N_EXPERTS], blk[:, 1, :N_EXPERTS], blk[:, 2, :N_EXPERTS] + pad_starts[None, :], rep(zeros_e),
        rep(padded - counts), rep(zeros_e), rep(pad_starts + counts),
        rep(jnp.concatenate([n_used, jnp.zeros((INFO_LANES - INFO_NUSED - 1,), I32)]))], axis=1).reshape(-1)
    eid = jnp.arange(N_EXPERTS, dtype=I32)
    has = padded > 0
    later = jnp.where(has[None, :] & (eid[None, :] > eid[:, None]), eid[None, :], N_EXPERTS)
    nxt_e = jnp.min(later, axis=1)
    nxt_e = jnp.where(nxt_e < N_EXPERTS, nxt_e, -1)
    slot_e = (jnp.cumsum(has.astype(I32)) - 1) % 2
    xb = _dispatch(info, h, mi, nblk * MOE_BLK)
    yb = _gmm(block_e, n_used, jnp.take(nxt_e, block_e), jnp.take(slot_e, block_e), xb, wgu, bgu, wdn, bdn, l)
    return _combine(info, x, mi, mf, mod, fw, yb, lay, final)


def _rope_tables(S):
    t = np.arange(S)
    row = (t // GRID_W).astype(np.float64)
    col = (t % GRID_W).astype(np.float64)
    lane = np.arange(LANES)
    j = lane % 16
    inv = ROPE_THETA ** (-(j.astype(np.float32)) / np.float32(16.0))
    pos = np.where((lane % 64) < 32, row[:, None], col[:, None]).astype(np.float32)
    ang = (pos * inv.astype(np.float32)[None, :]).astype(np.float32)
    sign = np.where((lane % 32) < 16, -1.0, 1.0).astype(np.float32)
    cos_t = np.concatenate([np.cos(ang), np.ones((TM, LANES), np.float32)], axis=0)
    sin_t = np.concatenate([np.sin(ang) * sign[None, :], np.zeros((TM, LANES), np.float32)], axis=0)
    return cos_t.astype(np.float32), sin_t.astype(np.float32)


def kernel(x, c, ctx, c_ctx, w_ada, b_ada, norm1_w, w_in, da_lambda, da_subln_w, ssd_conv_w, ssd_conv_b,
           ssd_a_log, ssd_dt_bias, ssd_d, ssd_norm_w, gla_gk_up, gla_gk_b, gla_norm_w, w_out, norm2_w,
           w_router, b_router, w_gate_up, b_gate_up, w_down, b_down, final_norm_w):
    B, S, D = x.shape
    Lc = ctx.shape[1]
    depth = w_ada.shape[0]
    assert S % TMR == 0 and Lc % TM == 0 and (B * Lc) % TMR == 0 and S % GRID_W == 0
    lay = dict(B=B, S=S, Lc=Lc, nlb=(B * S) // TM, bpb=S // TM, cpb=Lc // TM)
    n_lat = B * S

    xs = jnp.concatenate([x.reshape(B * S, D), ctx.reshape(B * Lc, D)], axis=0)

    cc = jnp.zeros((8, D), F32).at[0:B].set(c).at[B].set(c_ctx)
    mod_all = _ada(cc, w_ada, b_ada)

    cos_t, sin_t = (jnp.asarray(t) for t in _rope_tables(S))

    for l in range(depth):
        last = l == depth - 1
        lam_init = 0.8 - 0.6 * math.exp(-0.3 * l)
        mod = mod_all[l, 0:B + 1].reshape(B + 1, 1, 6 * D)

        wi = w_in[l]
        misc_w = jnp.zeros((D, LANES), F32)
        misc_w = misc_w.at[:, MISC_DT:MISC_DT + 8].set(wi[:, 2304:2312])
        misc_w = misc_w.at[:, MISC_CODE:MISC_CODE + 2 * GLA_RANK].set(wi[:, 3080:3112])
        w_re = jnp.concatenate([wi[:, 0:2304], wi[:, 2312:3080], misc_w], axis=1).astype(BF16)
        a128 = jnp.zeros((1, LANES), F32).at[0, 0:8].set(-jnp.exp(ssd_a_log[l].astype(F32)).reshape(-1))
        bias128 = jnp.zeros((1, LANES), F32).at[0, 0:8].set(ssd_dt_bias[l].astype(F32).reshape(-1))
        wgk = jnp.zeros((2, LANES, GLA_KEY_WIDTH), F32)
        for d in range(2):
            wgk = wgk.at[d, MISC_CODE + d * GLA_RANK:MISC_CODE + (d + 1) * GLA_RANK, :].set(gla_gk_up[l, d])
        bgk = gla_gk_b[l].reshape(2, 1, GLA_KEY_WIDTH)
        dsk = jnp.repeat(ssd_d[l], SSD_HEAD_DIM).reshape(1, SSD_WIDTH)
        snw = ssd_norm_w[l].reshape(1, SSD_WIDTH)
        gnw = jnp.tile(gla_norm_w[l], GLA_HEADS).reshape(1, GLA_WIDTH)
        wr = jnp.zeros((D, LANES), F32).at[:, 0:N_EXPERTS].set(w_router[l])
        br = jnp.zeros((1, LANES), F32).at[0, 0:N_EXPERTS].set(b_router[l])

        q, k, v, z, xbc_act, gla_in, misc = _inproj(xs, norm1_w[l].reshape(1, D), mod, cos_t, sin_t, w_re,
                                                    ssd_conv_w[l], ssd_conv_b[l].reshape(1, SSD_CONV_CH), lay)

        a = _attention(q, k, v, da_lambda[l], da_subln_w[l].reshape(1, DA_V), lay, lam_init,
                       ctx_queries=not last)
        yf, yb, of, ob = _scans(xbc_act, gla_in, misc, a128, bias128, wgk, bgk, lay)

        nrows = n_lat if last else xs.shape[0]
        xs = _outproj(xs, a, yf, yb, xbc_act, z, of, ob, gla_in, dsk, snw, gnw,
                      w_out[l].astype(BF16), mod, lay, nrows)
        xs = _moe(xs, norm2_w[l].reshape(1, D), mod, wr, br, w_gate_up, b_gate_up, w_down, b_down,
                  final_norm_w.reshape(1, D), lay, l, last)

    return xs.reshape(B, S, D)
```

```python
import functools
import math

import numpy as np
import jax
import jax.numpy as jnp
from jax import lax
from jax.experimental import pallas as pl
from jax.experimental.pallas import tpu as pltpu

F32 = jnp.float32
BF16 = jnp.bfloat16
I32 = jnp.int32

GRID_W = 64
EPS = 1e-6
DA_HEADS = 4
DA_QK = 64
DA_V = 128
DA_WIDTH = 512
ROPE_THETA = 10000.0
SSD_HEADS = 4
SSD_HEAD_DIM = 64
SSD_WIDTH = 256
SSD_STATE = 64
SSD_CONV_CH = 512
GLA_HEADS = 4
GLA_DK = 32
GLA_DV = 64
GLA_KEY_WIDTH = 128
GLA_WIDTH = 256
GLA_RANK = 16
GLA_NORMALIZER = 16.0
N_EXPERTS = 32
TOP_K = 4
D_EXPERT = 1024
SWIGLU_LIMIT = 7.0
SWIGLU_ALPHA = 1.702

LANES = 128
TM = 256
SSD_Q = 128
GLA_Q = 64
GLA_SUB = 4
ATT_TQ = 256
ATT_KT = (2816, 768, 512, 256)
LOG2E = 1.4426950408889634
MOE_BLK = 256
TMR = 512
SEG_ALIGN = 16
STG_ROWS = TMR * TOP_K + N_EXPERTS * SEG_ALIGN
INFO_LANES = 256
INFO_TAIL = 128
INFO_NUSED = 224
VMEM_LIMIT = 56 * 1024 * 1024

ROPE_HALF = DA_QK // 2
ROPE_PAIR = ROPE_HALF // 2
GLA_QKV = 2 * GLA_KEY_WIDTH + GLA_WIDTH
GLA_SLAB = GLA_QKV + GLA_WIDTH
N_DT = 2 * SSD_HEADS
N_CODE = 2 * GLA_RANK
ADA_TN = 512

O_DT = 3 * DA_WIDTH + SSD_WIDTH + SSD_CONV_CH
O_GLA = O_DT + N_DT
O_CODE = O_GLA + GLA_SLAB
C_Q = 0
C_K = C_Q + DA_WIDTH
C_V = C_K + DA_WIDTH
C_Z = C_V + DA_WIDTH
C_XBC = C_Z + SSD_WIDTH
C_GLA = C_XBC + SSD_CONV_CH
C_MISC = C_GLA + GLA_SLAB
C_END = C_MISC + LANES
MISC_DT = 0
MISC_CODE = N_DT


def _sigmoid(x):
    return 1.0 / (1.0 + jnp.exp(-x))


def _softplus(x):
    return jnp.maximum(x, 0.0) + jnp.log(1.0 + jnp.exp(-jnp.abs(x)))


def _split2(a):
    hi = a.astype(BF16)
    lo = (a - hi.astype(F32)).astype(BF16)
    return hi, lo


def _split3(a):
    a1 = a.astype(BF16)
    r1 = a - a1.astype(F32)
    a2 = r1.astype(BF16)
    a3 = (r1 - a2.astype(F32)).astype(BF16)
    return a1, a2, a3


def _dot(a, b):
    return jnp.dot(a, b, preferred_element_type=F32)


def _dot_hi(a, b):
    a1, a2 = _split2(a)
    b1, b2 = _split2(b)
    return _dot(a1, b1) + (_dot(a1, b2) + _dot(a2, b1))


def _dot_exact_lhs(m, a):
    a1, a2, a3 = _split3(a)
    mb = m.astype(BF16)
    return _dot(mb, a1) + (_dot(mb, a2) + _dot(mb, a3))


def _dot_exact_rhs(a, m):
    a1, a2, a3 = _split3(a)
    mb = m.astype(BF16)
    return _dot(a1, mb) + (_dot(a2, mb) + _dot(a3, mb))


def _params(*sem):
    return pltpu.CompilerParams(dimension_semantics=sem, vmem_limit_bytes=VMEM_LIMIT)


def _ada_kernel(c_ref, w_ref, b_ref, o_ref):
    c = c_ref[...]
    s = c * _sigmoid(c)
    o_ref[0] = _dot_hi(s, w_ref[0]) + b_ref[0]


def _ada(cc, w_ada, b_ada):
    L, D, N = w_ada.shape
    tn = ADA_TN
    return pl.pallas_call(
        _ada_kernel,
        grid=(L, N // tn),
        in_specs=[pl.BlockSpec((8, D), lambda l, j: (0, 0)),
                  pl.BlockSpec((1, D, tn), lambda l, j: (l, 0, j)),
                  pl.BlockSpec((1, 1, tn), lambda l, j: (l, 0, j))],
        out_specs=pl.BlockSpec((1, 8, tn), lambda l, j: (l, 0, j)),
        out_shape=jax.ShapeDtypeStruct((L, 8, N), F32),
        compiler_params=_params("parallel", "parallel"),
        name="ada",
    )(cc, w_ada, b_ada.reshape(L, 1, N))


def _inproj_kernel(x_ref, xp_ref, xn_ref, nw_ref, sh_ref, sc_ref, cos_ref, sin_ref, w_ref, cw_ref, cb_ref,
                   q_ref, k_ref, v_ref, z_ref, xbc_ref, gla_ref, misc_ref, *, nlb, bpb, cpb):
    def modulated(xf):
        ms = jnp.mean(xf * xf, axis=-1, keepdims=True)
        return ((xf * lax.rsqrt(ms + EPS) * nw_ref[...]) * (1.0 + sc_ref[0]) + sh_ref[0]).astype(BF16)

    hb = modulated(x_ref[...])
    cos = cos_ref[...]
    sin = sin_ref[...]
    lane = lax.broadcasted_iota(I32, cos.shape, 1)
    first = (lane % ROPE_HALF) < ROPE_PAIR

    def rope(p):
        outs = []
        for hd in range(DA_HEADS):
            ph = p[:, hd * LANES:(hd + 1) * LANES]
            sw = jnp.where(first, pltpu.roll(ph, LANES - ROPE_PAIR, 1), pltpu.roll(ph, ROPE_PAIR, 1))
            outs.append(ph * cos + sw * sin)
        return jnp.concatenate(outs, axis=1)

    q = rope(_dot(hb, w_ref[:, C_Q:C_K])) * (DA_QK ** -0.5 * LOG2E)
    q_ref[...] = q.astype(BF16)
    k_ref[...] = rope(_dot(hb, w_ref[:, C_K:C_V])).astype(BF16)
    v_ref[...] = _dot(hb, w_ref[:, C_V:C_Z]).astype(BF16)
    z_ref[...] = _dot(hb, w_ref[:, C_Z:C_XBC])
    gla_ref[...] = _dot(hb, w_ref[:, C_GLA:C_MISC])
    misc_ref[...] = _dot(hb, w_ref[:, C_MISC:C_END])

    xbc = _dot(hb, w_ref[:, C_XBC:C_GLA])
    halo = _dot(modulated(jnp.concatenate([xp_ref[...], xn_ref[...]], axis=0)), w_ref[:, C_XBC:C_GLA])
    i = pl.program_id(0)
    pos = jnp.where(i < nlb, i % bpb, (i - nlb) % cpb)
    last = jnp.where(i < nlb, bpb - 1, cpb - 1)
    prev_row = halo[7:8, :] * (pos != 0).astype(F32)
    next_row = halo[8:9, :] * (pos != last).astype(F32)
    ridx = lax.broadcasted_iota(I32, xbc.shape, 0)
    xm1 = jnp.where(ridx == 0, prev_row, pltpu.roll(xbc, 1, 0))
    xp1 = jnp.where(ridx == xbc.shape[0] - 1, next_row, pltpu.roll(xbc, xbc.shape[0] - 1, 0))
    cw = cw_ref[...]
    y = cw[0:1] * xm1 + cw[1:2] * xbc + cw[2:3] * xp1 + cb_ref[...]
    xbc_ref[...] = y * _sigmoid(y)


def _inproj(x, nw, mod, cos_t, sin_t, w, cw, cb, lay):
    T, D = x.shape
    nlb, bpb, B = lay["nlb"], lay["bpb"], lay["B"]
    r8 = TM // 8
    mrow = lambda i: jnp.where(i < nlb, i // bpb, B)
    prow = lambda i: jnp.where(i < nlb, i % bpb, bpb)
    row = lambda n: pl.BlockSpec((TM, n), lambda i: (i, 0))
    spb = bpb + lay["cpb"]
    kvrow = lambda i: (jnp.where(i < nlb, (i // bpb) * spb + lay["cpb"] + i % bpb,
                                 ((i - nlb) // lay["cpb"]) * spb + (i - nlb) % lay["cpb"]), 0)
    kv = pl.BlockSpec((TM, DA_WIDTH), kvrow)
    outs = [(DA_WIDTH, BF16), (DA_WIDTH, BF16), (DA_WIDTH, BF16), (SSD_WIDTH, F32), (SSD_CONV_CH, F32),
            (GLA_SLAB, F32), (LANES, F32)]
    out_specs = [row(n) for n, _ in outs]
    out_specs[1] = kv
    out_specs[2] = kv
    return pl.pallas_call(
        functools.partial(_inproj_kernel, nlb=nlb, bpb=bpb, cpb=lay["cpb"]),
        grid=(T // TM,),
        in_specs=[row(D),
                  pl.BlockSpec((8, D), lambda i: (jnp.maximum(i * r8 - 1, 0), 0)),
                  pl.BlockSpec((8, D), lambda i: (jnp.minimum((i + 1) * r8, T // 8 - 1), 0)),
                  pl.BlockSpec((1, D), lambda i: (0, 0)),
                  pl.BlockSpec((1, 1, D), lambda i: (mrow(i), 0, 0)),
                  pl.BlockSpec((1, 1, D), lambda i: (mrow(i), 0, 1)),
                  pl.BlockSpec((TM, LANES), lambda i: (prow(i), 0)),
                  pl.BlockSpec((TM, LANES), lambda i: (prow(i), 0)),
                  pl.BlockSpec((D, C_END), lambda i: (0, 0)),
                  pl.BlockSpec((3, SSD_CONV_CH), lambda i: (0, 0)),
                  pl.BlockSpec((1, SSD_CONV_CH), lambda i: (0, 0))],
        out_specs=out_specs,
        out_shape=[jax.ShapeDtypeStruct((T, n), dt) for n, dt in outs],
        compiler_params=_params("parallel"),
        name="inproj",
    )(x, x, x, nw, mod, mod, cos_t, sin_t, w, cw, cb)


def _lane_fold(x, op):
    f = x[:, 0:LANES]
    for i in range(1, x.shape[1] // LANES):
        f = op(f, x[:, i * LANES:(i + 1) * LANES])
    return f


def _attn_stack_maps(q_ref, q2_sc):
    tq = q_ref.shape[0]
    q = q_ref[...]
    lane = lax.broadcasted_iota(I32, q.shape, 1)
    zero = jnp.zeros_like(q)
    q2_sc[0:tq, :] = jnp.where(lane < DA_QK, q, zero)
    q2_sc[tq:2 * tq, :] = jnp.where(lane >= DA_QK, q, zero)


def _attn_scores_tile(q2_sc, k_ref, s_sc, slot, mf_sc, j, kt):
    kj = k_ref[pl.ds(pl.multiple_of(j * kt, kt), kt), :]
    s = lax.dot_general(q2_sc[...], kj, (((1,), (1,)), ((), ())), preferred_element_type=F32)
    s_sc[slot, j, :, 0:kt] = s
    mf_sc[...] = jnp.maximum(mf_sc[...], _lane_fold(s, jnp.maximum))


def _attn_weigh_tile(v_ref, s_sc, slot, p_sc, l_sc, acc_sc, m, j, kt):
    sub = next(w for w in (512, 256) if kt % w == 0)
    for c0 in range(0, kt, sub):
        p = jnp.exp2(s_sc[slot, j, :, c0:c0 + sub] - m)
        l_sc[...] += _lane_fold(p, jnp.add)
        p_sc[:, c0:c0 + sub] = p.astype(BF16)
    vj = v_ref[pl.ds(pl.multiple_of(j * kt, kt), kt), :]
    acc_sc[...] += _dot(p_sc[:, 0:kt], vj)


def _attn_finish(lam_ref, w_ref, o_ref, l_sc, acc_sc, lam_init):
    tq = o_ref.shape[0]
    lp = lam_ref[...]
    lam = (jnp.exp(jnp.sum(lp[0:1] * lp[1:2], axis=1, keepdims=True))
           - jnp.exp(jnp.sum(lp[2:3] * lp[3:4], axis=1, keepdims=True)) + lam_init)
    acc = acc_sc[...]
    l = jnp.sum(l_sc[...], axis=1, keepdims=True)
    o = acc[0:tq] / l[0:tq] - lam * (acc[tq:2 * tq] / l[tq:2 * tq])
    ms = jnp.mean(o * o, axis=-1, keepdims=True)
    o_ref[...] = (o * lax.rsqrt(ms + EPS) * w_ref[...]) * (1.0 - lam_init)


def _attn_kernel(q_ref, k_ref, v_ref, lam_ref, w_ref, o_ref, q2_sc, s_sc, p_sc, mf_sc, l_sc, acc_sc,
                 *, nt, kt, lam_init):
    _attn_stack_maps(q_ref, q2_sc)
    mf_sc[...] = jnp.full(mf_sc.shape, -jnp.inf, F32)
    l_sc[...] = jnp.zeros(l_sc.shape, F32)
    acc_sc[...] = jnp.zeros(acc_sc.shape, F32)

    def scores(j, carry):
        _attn_scores_tile(q2_sc, k_ref, s_sc, 0, mf_sc, j, kt)
        return carry

    lax.fori_loop(0, nt, scores, 0)
    m = jnp.max(mf_sc[...], axis=1, keepdims=True)

    def weigh(j, carry):
        _attn_weigh_tile(v_ref, s_sc, 0, p_sc, l_sc, acc_sc, m, j, kt)
        return carry

    lax.fori_loop(0, nt, weigh, 0)
    _attn_finish(lam_ref, w_ref, o_ref, l_sc, acc_sc, lam_init)


def _attn_pipe_kernel(q_ref, k_ref, v_ref, lam_ref, w_ref, o_ref, q2_sc, s_sc, p_sc, mf_sc, m_sc, l_sc, acc_sc,
                      *, nq, nt, kt, lam_init):
    t = pl.program_id(2)

    @pl.when(t < nq)
    def _():
        _attn_stack_maps(q_ref, q2_sc)
        mf_sc[...] = jnp.full(mf_sc.shape, -jnp.inf, F32)

    @pl.when(t > 0)
    def _():
        l_sc[...] = jnp.zeros(l_sc.shape, F32)
        acc_sc[...] = jnp.zeros(acc_sc.shape, F32)

    def run(slot_a, slot_b):
        m_prev = None if slot_b is None else m_sc[slot_b]

        def body(j, carry):
            if slot_b is not None:
                _attn_weigh_tile(v_ref, s_sc, slot_b, p_sc, l_sc, acc_sc, m_prev, j, kt)
            if slot_a is not None:
                _attn_scores_tile(q2_sc, k_ref, s_sc, slot_a, mf_sc, j, kt)
            return carry

        lax.fori_loop(0, nt, body, 0)
        if slot_a is not None:
            m_sc[slot_a] = jnp.max(mf_sc[...], axis=1, keepdims=True)

    inner = (t > 0) & (t < nq)
    pl.when(t == 0)(lambda: run(0, None))
    pl.when(inner & (t % 2 == 1))(lambda: run(1, 0))
    pl.when(inner & (t % 2 == 0))(lambda: run(0, 1))
    pl.when(t == nq)(lambda: run(None, (nq - 1) % 2))

    @pl.when(t > 0)
    def _():
        _attn_finish(lam_ref, w_ref, o_ref, l_sc, acc_sc, lam_init)


def _attention(q, k, v, lam_p, subln_w, lay, lam_init, *, ctx_queries):
    B, S, Lc = lay["B"], lay["S"], lay["Lc"]
    nkeys = Lc + S
    assert nkeys % Lc == 0
    small = [pl.BlockSpec((4, DA_QK), lambda b, h, t: (0, 0)), pl.BlockSpec((1, DA_V), lambda b, h, t: (0, 0))]
    stacked = lambda n, dt: pltpu.VMEM((2 * ATT_TQ, n), dt)

    kt = next(t for t in ATT_KT if nkeys % t == 0)
    nq = S // ATT_TQ
    kmap = lambda b, h, t: (b, h)
    a_lat = pl.pallas_call(
        functools.partial(_attn_pipe_kernel, nq=nq, nt=nkeys // kt, kt=kt, lam_init=lam_init),
        grid=(B, DA_HEADS, nq + 1),
        in_specs=[pl.BlockSpec((ATT_TQ, LANES), lambda b, h, t: (b * nq + jnp.minimum(t, nq - 1), h)),
                  pl.BlockSpec((nkeys, LANES), kmap),
                  pl.BlockSpec((nkeys, LANES), kmap)] + small,
        out_specs=pl.BlockSpec((ATT_TQ, LANES), lambda b, h, t: (b * nq + jnp.maximum(t - 1, 0), h)),
        out_shape=jax.ShapeDtypeStruct((B * S, DA_WIDTH), F32),
        scratch_shapes=[stacked(LANES, BF16),
                        pltpu.VMEM((2, nkeys // kt, 2 * ATT_TQ, kt), F32),
                        stacked(kt, BF16), stacked(LANES, F32),
                        pltpu.VMEM((2, 2 * ATT_TQ, 1), F32),
                        stacked(LANES, F32), stacked(LANES, F32)],
        compiler_params=_params("parallel", "parallel", "arbitrary"),
        name="attn",
    )(q, k, v, lam_p, subln_w)
    if not ctx_queries:
        return a_lat

    ktc = next(t for t in ATT_KT if Lc % t == 0)
    nqc = Lc // ATT_TQ
    cmap = lambda b, h, t: (b * (nkeys // Lc), h)
    a_ctx = pl.pallas_call(
        functools.partial(_attn_kernel, nt=Lc // ktc, kt=ktc, lam_init=lam_init),
        grid=(B, DA_HEADS, nqc),
        in_specs=[pl.BlockSpec((ATT_TQ, LANES), lambda b, h, t: ((B * S) // ATT_TQ + b * nqc + t, h)),
                  pl.BlockSpec((Lc, LANES), cmap),
                  pl.BlockSpec((Lc, LANES), cmap)] + small,
        out_specs=pl.BlockSpec((ATT_TQ, LANES), lambda b, h, t: (b * nqc + t, h)),
        out_shape=jax.ShapeDtypeStruct((B * Lc, DA_WIDTH), F32),
        scratch_shapes=[stacked(LANES, BF16),
                        pltpu.VMEM((1, Lc // ktc, 2 * ATT_TQ, ktc), F32),
                        stacked(ktc, BF16), stacked(LANES, F32), stacked(LANES, F32), stacked(LANES, F32)],
        compiler_params=_params("parallel", "parallel", "arbitrary"),
        name="attn_ctx",
    )(q, k, v, lam_p, subln_w)
    return jnp.concatenate([a_lat, a_ctx], axis=0)


def _ssd_dir(xbc, misc, a128, bias128, hs_ref, dirn, rev):
    Q = xbc.shape[0]
    dt = _softplus(misc + bias128)
    a = dt * a128
    ii = lax.broadcasted_iota(I32, (Q, Q), 0)
    jj = lax.broadcasted_iota(I32, (Q, Q), 1)
    M = (jj >= ii) if rev else (jj <= ii)
    acs = _dot_exact_lhs(M.astype(F32), a)
    acs_t = acs.T
    dt_t = dt.T
    xs = xbc[:, 0:SSD_WIDTH]
    Bt = xbc[:, SSD_WIDTH:SSD_WIDTH + 128].T
    Cm = xbc[:, SSD_WIDTH + 128:SSD_WIDTH + 256]
    last = 0 if rev else Q - 1
    ys = []
    for g in range(2):
        Cg = Cm[:, g * SSD_STATE:(g + 1) * SSD_STATE]
        Btg = Bt[g * SSD_STATE:(g + 1) * SSD_STATE, :]
        CB = _dot(Cg, Btg)
        for hh in range(2):
            h = 2 * g + hh
            li = dirn * SSD_HEADS + h
            acs_c = acs[:, li:li + 1]
            acs_r = acs_t[li:li + 1, :]
            dt_r = dt_t[li:li + 1, :]
            decay = jnp.exp(jnp.where(M, acs_c - acs_r, -jnp.inf))
            sc = CB * decay * dt_r
            x_h = xs[:, h * SSD_HEAD_DIM:(h + 1) * SSD_HEAD_DIM]
            hs = hs_ref[h]
            y = _dot(sc, x_h) + _dot(Cg * jnp.exp(acs_c), hs)
            tot = acs_r[:, last:last + 1]
            w_end = jnp.exp(tot - acs_r) * dt_r
            hs_ref[h] = jnp.exp(tot) * hs + _dot(Btg * w_end, x_h)
            ys.append(y)
    return jnp.concatenate(ys, axis=1)


def _scan_maps(lay, rows):
    ncc, ncl = lay["Lc"] // rows, lay["S"] // rows
    base = (lay["B"] * lay["S"]) // rows
    fmap = lambda b, c: (jnp.where(c < ncc, base + b * ncc + c, b * ncl + c - ncc), 0)
    bmap = lambda b, c: (jnp.where(c < ncc, base + b * ncc + (ncc - 1 - c),
                                   b * ncl + (ncl - 1 - (c - ncc))), 0)
    return ncc + ncl, fmap, bmap


def _gla_block(blk, misc, wgk, bgk, s_ref, rev):
    R = blk.shape[0]
    kw = GLA_KEY_WIDTH
    q, k, v = blk[:, 0:kw], blk[:, kw:2 * kw], blk[:, 2 * kw:2 * kw + GLA_WIDTH]
    pre = _dot_hi(misc, wgk) + bgk
    gk = (jnp.minimum(pre, 0.0) - jnp.log(1.0 + jnp.exp(-jnp.abs(pre)))) / GLA_NORMALIZER
    ii = lax.broadcasted_iota(I32, (R, R), 0)
    jj = lax.broadcasted_iota(I32, (R, R), 1)
    same = (ii // GLA_Q) == (jj // GLA_Q)
    M = same & ((jj >= ii) if rev else (jj <= ii))
    b = _dot_exact_lhs(M.astype(F32), gk)
    tot = _dot_exact_lhs(same.astype(F32), gk)
    q_t = (q * (GLA_DK ** -0.5)) * jnp.exp(b)
    k_t = k * jnp.exp(-b)
    k_end_t = (k * jnp.exp(tot - b)).T
    gk_t = gk.T
    lane_k = lax.broadcasted_iota(I32, (1, kw), 1) // GLA_DK
    lane_v = lax.broadcasted_iota(I32, (1, GLA_WIDTH), 1) // GLA_DV
    o = jnp.zeros((R, GLA_WIDTH), F32)
    for h in range(GLA_HEADS):
        qh = jnp.where(lane_k == h, q_t, 0.0)
        att = lax.dot_general(qh, k_t, (((1,), (1,)), ((), ())), preferred_element_type=F32)
        o = o + _dot(jnp.where(M, att, 0.0), jnp.where(lane_v == h, v, 0.0))
    col_chunk = lax.broadcasted_iota(I32, (1, R), 1) // GLA_Q
    S = s_ref[...]
    bd = (lax.broadcasted_iota(I32, S.shape, 0) // GLA_DK) == (lax.broadcasted_iota(I32, S.shape, 1) // GLA_DV)
    nsub = R // GLA_Q
    inter = [None] * nsub
    for c in (range(nsub - 1, -1, -1) if rev else range(nsub)):
        sel = col_chunk == c
        inter[c] = _dot(q_t[c * GLA_Q:(c + 1) * GLA_Q], S)
        decay = jnp.exp(jnp.sum(jnp.where(sel, gk_t, 0.0), axis=1, keepdims=True))
        S = jnp.where(bd, S * decay + _dot(jnp.where(sel, k_end_t, 0.0), v), 0.0)
    s_ref[...] = S
    return o + jnp.concatenate(inter, axis=0)


def _scan_kernel(xf_ref, gf_ref, mf_ref, xb_ref, gb_ref, mb_ref, a_ref, bias_ref, wgk_ref, bgk_ref,
                 yf_ref, yb_ref, of_ref, ob_ref, hf_sc, hb_sc, sf_sc, sb_sc):
    @pl.when(pl.program_id(1) == 0)
    def _():
        for sc in (hf_sc, hb_sc, sf_sc, sb_sc):
            sc[...] = jnp.zeros(sc.shape, F32)

    a128 = a_ref[...]
    bias128 = bias_ref[...]
    mf = mf_ref[...]
    mb = mb_ref[...]
    nsub = xf_ref.shape[0] // SSD_Q
    for c in range(nsub):
        r0 = c * SSD_Q
        yf_ref[r0:r0 + SSD_Q, :] = _ssd_dir(xf_ref[r0:r0 + SSD_Q, :], mf[r0:r0 + SSD_Q], a128, bias128,
                                            hf_sc, 0, False)
        r1 = (nsub - 1 - c) * SSD_Q
        yb_ref[r1:r1 + SSD_Q, :] = _ssd_dir(xb_ref[r1:r1 + SSD_Q, :], mb[r1:r1 + SSD_Q], a128, bias128,
                                            hb_sc, 1, True)
    of_ref[...] = _gla_block(gf_ref[...], mf, wgk_ref[0], bgk_ref[0], sf_sc, False)
    ob_ref[...] = _gla_block(gb_ref[...], mb, wgk_ref[1], bgk_ref[1], sb_sc, True)


def _scans(xbc_act, gla_in, misc, a128, bias128, wgk, bgk, lay):
    T = xbc_act.shape[0]
    R = GLA_Q * GLA_SUB
    assert R % SSD_Q == 0
    nc, fmap, bmap = _scan_maps(lay, R)
    vec = pl.BlockSpec((1, LANES), lambda b, c: (0, 0))
    ins = lambda m: [pl.BlockSpec((R, SSD_CONV_CH), m), pl.BlockSpec((R, GLA_QKV), m), pl.BlockSpec((R, LANES), m)]
    outs = lambda n: [pl.BlockSpec((R, n), fmap), pl.BlockSpec((R, n), bmap)]
    return pl.pallas_call(
        _scan_kernel,
        grid=(lay["B"], nc),
        in_specs=ins(fmap) + ins(bmap) + [
            vec, vec,
            pl.BlockSpec((2, LANES, GLA_KEY_WIDTH), lambda b, c: (0, 0, 0)),
            pl.BlockSpec((2, 1, GLA_KEY_WIDTH), lambda b, c: (0, 0, 0))],
        out_specs=outs(SSD_WIDTH) + outs(GLA_WIDTH),
        out_shape=[jax.ShapeDtypeStruct((T, SSD_WIDTH), F32)] * 2 + [jax.ShapeDtypeStruct((T, GLA_WIDTH), F32)] * 2,
        scratch_shapes=[pltpu.VMEM((SSD_HEADS, SSD_STATE, SSD_HEAD_DIM), F32)] * 2
        + [pltpu.VMEM((GLA_KEY_WIDTH, GLA_WIDTH), F32)] * 2,
        compiler_params=_params("parallel", "arbitrary"),
        name="scans",
    )(xbc_act, gla_in, misc, xbc_act, gla_in, misc, a128, bias128, wgk, bgk)


def _outproj_kernel(x_ref, a_ref, yf_ref, yb_ref, xs_ref, z_ref, of_ref, ob_ref, g_ref,
                    dsk_ref, snw_ref, gnw_ref, w_ref, g1_ref, o_ref):
    y = yf_ref[...] + yb_ref[...] + dsk_ref[...] * xs_ref[...]
    z = z_ref[...]
    gs = y * (z * _sigmoid(z))
    half = SSD_WIDTH // 2
    parts = []
    for grp in range(2):
        seg = gs[:, grp * half:(grp + 1) * half]
        parts.append(seg * lax.rsqrt(jnp.mean(seg * seg, axis=-1, keepdims=True) + EPS))
    s = jnp.concatenate(parts, axis=1) * snw_ref[...]
    o = of_ref[...] + ob_ref[...]
    bd = ((lax.broadcasted_iota(I32, (GLA_WIDTH, GLA_WIDTH), 0) // GLA_DV)
          == (lax.broadcasted_iota(I32, (GLA_WIDTH, GLA_WIDTH), 1) // GLA_DV))
    ms = _dot_exact_rhs(o * o, jnp.where(bd, 1.0 / GLA_DV, 0.0))
    g = g_ref[...]
    c = (o * lax.rsqrt(ms + EPS) * gnw_ref[...]) * (g * _sigmoid(g))
    mix = jnp.concatenate([a_ref[...], s, c], axis=1).astype(BF16)
    o_ref[...] = x_ref[...] + g1_ref[0] * _dot(mix, w_ref[...])


def _outproj(x, a, yf, yb, xbc_act, z, of, ob, gla_in, dsk, snw, gnw, w_out, mod, lay, nrows):
    D = x.shape[1]
    nlb, bpb, B = lay["nlb"], lay["bpb"], lay["B"]
    mrow = lambda i: jnp.where(i < nlb, i // bpb, B)
    row = lambda n, cb=0: pl.BlockSpec((TM, n), lambda i: (i, cb))
    vec = lambda n: pl.BlockSpec((1, n), lambda i: (0, 0))
    return pl.pallas_call(
        _outproj_kernel,
        grid=(nrows // TM,),
        in_specs=[row(D), row(DA_WIDTH), row(SSD_WIDTH), row(SSD_WIDTH), row(SSD_WIDTH), row(SSD_WIDTH),
                  row(GLA_WIDTH), row(GLA_WIDTH), row(GLA_WIDTH, 2),
                  vec(SSD_WIDTH), vec(SSD_WIDTH), vec(GLA_WIDTH),
                  pl.BlockSpec((D, D), lambda i: (0, 0)),
                  pl.BlockSpec((1, 1, D), lambda i: (mrow(i), 0, 2))],
        out_specs=row(D),
        out_shape=jax.ShapeDtypeStruct((nrows, D), F32),
        compiler_params=_params("parallel"),
        name="outproj",
    )(x, a, yf, yb, xbc_act, z, of, ob, gla_in, dsk, snw, gnw, w_out, mod)


def _route_kernel(x_ref, nw_ref, sh_ref, sc_ref, wr_ref, br_ref,
                  h_ref, mi_ref, mf_ref, blk_ref, tot_ref, carry_sc):
    i = pl.program_id(0)

    @pl.when(i == 0)
    def _():
        carry_sc[...] = jnp.zeros(carry_sc.shape, F32)

    xf = x_ref[...]
    ms = jnp.mean(xf * xf, axis=-1, keepdims=True)
    h = (xf * lax.rsqrt(ms + EPS) * nw_ref[...]) * (1.0 + sc_ref[0]) + sh_ref[0]
    h_ref[...] = h
    tm = h.shape[0]
    lane = lax.broadcasted_iota(I32, (tm, LANES), 1)
    logits = jnp.where(lane < N_EXPERTS, _dot_hi(h, wr_ref[...]) + br_ref[...], -jnp.inf)
    vals, idxs, hots = [], [], []
    l = logits
    for _ in range(TOP_K):
        m = jnp.max(l, axis=1, keepdims=True)
        idx = jnp.min(jnp.where(l == m, lane, LANES), axis=1, keepdims=True)
        hot = lane == idx
        vals.append(m)
        idxs.append(idx)
        hots.append(hot)
        l = jnp.where(hot, -jnp.inf, l)
    es = [jnp.exp(v - vals[0]) for v in vals]
    den = es[0] + es[1] + es[2] + es[3]
    hot_all = jnp.zeros((tm, LANES), F32)
    for hot in hots:
        hot_all = hot_all + hot.astype(F32)
    ii = lax.broadcasted_iota(I32, (tm, tm), 0)
    jj = lax.broadcasted_iota(I32, (tm, tm), 1)
    before = _dot((jj < ii).astype(BF16), hot_all.astype(BF16))
    cnt = jnp.sum(hot_all, axis=0, keepdims=True)
    cntp = jnp.floor((cnt + (SEG_ALIGN - 1.0)) * (1.0 / SEG_ALIGN)) * SEG_ALIGN
    ei = lax.broadcasted_iota(I32, (LANES, LANES), 0)
    ej = lax.broadcasted_iota(I32, (LANES, LANES), 1)
    units = jnp.broadcast_to(cntp * (1.0 / SEG_ALIGN), (8, LANES)).astype(BF16)
    seg = _dot(units, (ei < ej).astype(BF16))[0:1] * SEG_ALIGN
    pos_e = seg + before
    mi = jnp.zeros((tm, LANES), I32)
    mf = jnp.zeros((tm, LANES), F32)
    for kk in range(TOP_K):
        spos = jnp.sum(jnp.where(hots[kk], pos_e, 0.0), axis=1, keepdims=True).astype(I32)
        mi = jnp.where(lane == kk, idxs[kk], mi)
        mi = jnp.where(lane == TOP_K + kk, spos, mi)
        mf = jnp.where(lane == kk, es[kk] / den, mf)
    mi_ref[...] = mi
    mf_ref[...] = mf
    rowi = lax.broadcasted_iota(I32, (8, LANES), 0)
    info = jnp.where(rowi == 0, cntp, jnp.where(rowi == 1, seg, jnp.where(rowi == 2, carry_sc[...], 0.0)))
    blk_ref[0] = info.astype(I32)
    carry_sc[...] = carry_sc[...] + cntp
    tot_ref[...] = carry_sc[...]


def _route(x, nw, mod, wr, br, lay):
    T, D = x.shape
    B, S = lay["B"], lay["S"]
    mrow = lambda i: jnp.where(i < (B * S) // TMR, i // (S // TMR), B)
    row = lambda n: pl.BlockSpec((TMR, n), lambda i: (i, 0))
    return pl.pallas_call(
        _route_kernel,
        grid=(T // TMR,),
        in_specs=[row(D),
                  pl.BlockSpec((1, D), lambda i: (0, 0)),
                  pl.BlockSpec((1, 1, D), lambda i: (mrow(i), 0, 3)),
                  pl.BlockSpec((1, 1, D), lambda i: (mrow(i), 0, 4)),
                  pl.BlockSpec((D, LANES), lambda i: (0, 0)),
                  pl.BlockSpec((1, LANES), lambda i: (0, 0))],
        out_specs=[row(D), row(LANES), row(LANES),
                   pl.BlockSpec((1, 8, LANES), lambda i: (i, 0, 0)),
                   pl.BlockSpec((1, LANES), lambda i: (0, 0))],
        out_shape=[jax.ShapeDtypeStruct((T, D), F32), jax.ShapeDtypeStruct((T, LANES), I32),
                   jax.ShapeDtypeStruct((T, LANES), F32),
                   jax.ShapeDtypeStruct((T // TMR, 8, LANES), I32),
                   jax.ShapeDtypeStruct((1, LANES), F32)],
        scratch_shapes=[pltpu.VMEM((1, LANES), F32)],
        compiler_params=_params("arbitrary"),
        name="route",
    )(x, nw, mod, mod, wr, br)


def _segment_starts(info_ref, base, make_copy):
    def start_expert(e, carry):
        n = info_ref[base + e] // SEG_ALIGN
        src0 = info_ref[base + N_EXPERTS + e]
        dst0 = info_ref[base + 2 * N_EXPERTS + e]

        def start_chunk(c, carry):
            make_copy(pl.multiple_of(src0 + c * SEG_ALIGN, SEG_ALIGN),
                      pl.multiple_of(dst0 + c * SEG_ALIGN, SEG_ALIGN)).start()
            return carry

        lax.fori_loop(0, n, start_chunk, 0)
        return carry

    lax.fori_loop(0, N_EXPERTS, start_expert, 0)


def _segment_waits(info_ref, base, make_copy):
    total = lax.fori_loop(0, N_EXPERTS, lambda e, n: n + info_ref[base + e] // SEG_ALIGN, 0)

    def wait_chunk(c, carry):
        make_copy(0, 0).wait()
        return carry

    lax.fori_loop(0, total, wait_chunk, 0)


def _segment_copies(info_ref, base, make_copy):
    _segment_starts(info_ref, base, make_copy)
    _segment_waits(info_ref, base, make_copy)


def _dispatch_kernel(info_ref, h_ref, mi_ref, xb_ref, stg_sc, sem):
    tm = h_ref.shape[0]
    rows = stg_sc.shape[0]
    seg_copy = lambda s, d: pltpu.make_async_copy(
        stg_sc.at[pl.ds(s, SEG_ALIGN)], xb_ref.at[pl.ds(d, SEG_ALIGN)], sem)

    @pl.when(pl.program_id(0) == 0)
    def _():
        stg_sc[0:MOE_BLK, :] = jnp.zeros((MOE_BLK, stg_sc.shape[1]), BF16)
        _segment_copies(info_ref, INFO_TAIL, seg_copy)

        def spare_copy(j):
            return pltpu.make_async_copy(stg_sc.at[pl.ds(0, MOE_BLK)],
                                         xb_ref.at[pl.ds(pl.multiple_of(j * MOE_BLK, MOE_BLK), MOE_BLK)], sem)

        def start_spare(j, carry):
            spare_copy(j).start()
            return carry

        def wait_spare(j, carry):
            spare_copy(j).wait()
            return carry

        first_spare = info_ref[INFO_NUSED]
        lax.fori_loop(first_spare, xb_ref.shape[0] // MOE_BLK, start_spare, 0)
        lax.fori_loop(first_spare, xb_ref.shape[0] // MOE_BLK, wait_spare, 0)

    spos_t = mi_ref[...].astype(F32).T
    r = lax.broadcasted_iota(I32, (rows, tm), 0).astype(F32)
    pm = r == spos_t[TOP_K:TOP_K + 1]
    for kk in range(1, TOP_K):
        pm = pm | (r == spos_t[TOP_K + kk:TOP_K + kk + 1])
    stg_sc[...] = _dot(pm.astype(BF16), h_ref[...].astype(BF16)).astype(BF16)
    _segment_copies(info_ref, 0, seg_copy)


def _dispatch(info, h, mi, rows):
    T, D = h.shape
    return pl.pallas_call(
        _dispatch_kernel,
        grid=(T // TMR,),
        in_specs=[pl.BlockSpec((INFO_LANES,), lambda i: (i,), memory_space=pltpu.SMEM),
                  pl.BlockSpec((TMR, D), lambda i: (i, 0)),
                  pl.BlockSpec((TMR, LANES), lambda i: (i, 0))],
        out_specs=pl.BlockSpec(memory_space=pl.ANY),
        out_shape=jax.ShapeDtypeStruct((rows, D), BF16),
        scratch_shapes=[pltpu.VMEM((STG_ROWS, D), BF16), pltpu.SemaphoreType.DMA],
        compiler_params=_params("arbitrary"),
        name="dispatch",
    )(info, h, mi)


def _gmm_kernel(be_ref, nu_ref, nxt_ref, slot_ref, x_ref, wgu_hbm, bgu_ref, wdn_hbm, bdn_ref, o_ref,
                gu_buf, dn_buf, wgu_sc, wdn_sc, sems, *, layer):
    i = pl.program_id(0)
    e = be_ref[i]
    prev = be_ref[jnp.maximum(i - 1, 0)]

    def fetch(expert, slot):
        return (pltpu.make_async_copy(wgu_hbm.at[layer, expert], gu_buf.at[slot], sems.at[0, slot]),
                pltpu.make_async_copy(wdn_hbm.at[layer, expert], dn_buf.at[slot], sems.at[1, slot]))

    @pl.when(i == 0)
    def _():
        for cp in fetch(e, slot_ref[0]):
            cp.start()

    @pl.when(((i == 0) | (e != prev)) & (i < nu_ref[0]))
    def _():
        slot = slot_ref[i]
        for cp in fetch(e, slot):
            cp.wait()
        wgu_sc[...] = gu_buf[slot].astype(BF16)
        wdn_sc[...] = dn_buf[slot].astype(BF16)

        @pl.when(nxt_ref[i] >= 0)
        def _():
            for cp in fetch(nxt_ref[i], 1 - slot):
                cp.start()

    @pl.when(i < nu_ref[0])
    def _():
        gu = _dot(x_ref[...], wgu_sc[...]) + bgu_ref[0, 0]
        glu = jnp.minimum(gu[:, 0:D_EXPERT], SWIGLU_LIMIT)
        lin = jnp.clip(gu[:, D_EXPERT:2 * D_EXPERT], -SWIGLU_LIMIT, SWIGLU_LIMIT)
        act = glu * _sigmoid(SWIGLU_ALPHA * glu) * (lin + 1.0)
        o_ref[...] = (_dot(act.astype(BF16), wdn_sc[...]) + bdn_ref[0, 0]).astype(BF16)

    @pl.when(i >= nu_ref[0])
    def _():
        o_ref[...] = jnp.zeros(o_ref.shape, BF16)


def _gmm(block_e, n_used, nxt, slot, xb, wgu, bgu, wdn, bdn, l):
    P, D = xb.shape
    L, E, _, F2 = wgu.shape
    xmap = lambda i, be, nu, nx, sl: (jnp.minimum(i, nu[0] - 1), 0)
    bmap = lambda i, be, nu, nx, sl: (l, be[i], 0, 0)
    return pl.pallas_call(
        functools.partial(_gmm_kernel, layer=l),
        grid_spec=pltpu.PrefetchScalarGridSpec(
            num_scalar_prefetch=4,
            grid=(P // MOE_BLK,),
            in_specs=[pl.BlockSpec((MOE_BLK, D), xmap),
                      pl.BlockSpec(memory_space=pl.ANY),
                      pl.BlockSpec((1, 1, 1, F2), bmap),
                      pl.BlockSpec(memory_space=pl.ANY),
                      pl.BlockSpec((1, 1, 1, D), bmap)],
            out_specs=pl.BlockSpec((MOE_BLK, D), lambda i, be, nu, nx, sl: (i, 0)),
            scratch_shapes=[pltpu.VMEM((2, D, F2), F32), pltpu.VMEM((2, F2 // 2, D), F32),
                            pltpu.VMEM((D, F2), BF16), pltpu.VMEM((F2 // 2, D), BF16),
                            pltpu.SemaphoreType.DMA((2, 2))]),
        out_shape=jax.ShapeDtypeStruct((P, D), BF16),
        compiler_params=_params("arbitrary"),
        name="gmm",
    )(block_e, n_used, nxt, slot, xb, wgu, bgu.reshape(L, E, 1, F2), wdn, bdn.reshape(L, E, 1, D))


def _combine_kernel(info_ref, x_ref, mi_ref, gate_ref, g2_ref, fw_ref, yb_ref, o_ref, stg_sc, sem, *, final):
    tm = x_ref.shape[0]
    rows = stg_sc.shape[0]

    @pl.when(pl.program_id(0) == 0)
    def _():
        stg_sc[...] = jnp.zeros(stg_sc.shape, BF16)

    _segment_copies(info_ref, 0, lambda s, d: pltpu.make_async_copy(
        yb_ref.at[pl.ds(d, SEG_ALIGN)], stg_sc.at[pl.ds(s, SEG_ALIGN)], sem))

    spos = mi_ref[...].astype(F32)
    gate = gate_ref[...]
    r = lax.broadcasted_iota(I32, (tm, rows), 1).astype(F32)
    g = jnp.where(r == spos[:, TOP_K:TOP_K + 1], gate[:, 0:1], 0.0)
    for kk in range(1, TOP_K):
        g = g + jnp.where(r == spos[:, TOP_K + kk:TOP_K + kk + 1], gate[:, kk:kk + 1], 0.0)
    g_hi, g_lo = _split2(g)
    y = stg_sc[...]
    out = x_ref[...] + g2_ref[0] * (_dot(g_hi, y) + _dot(g_lo, y))
    if final:
        ms = jnp.mean(out * out, axis=-1, keepdims=True)
        out = out * lax.rsqrt(ms + EPS) * fw_ref[...]
    o_ref[...] = out


def _combine(info, x, mi, gates, mod, fw, yb, lay, final):
    T, D = x.shape
    B, S = lay["B"], lay["S"]
    mrow = lambda i: jnp.where(i < (B * S) // TMR, i // (S // TMR), B)
    return pl.pallas_call(
        functools.partial(_combine_kernel, final=final),
        grid=(T // TMR,),
        in_specs=[pl.BlockSpec((INFO_LANES,), lambda i: (i,), memory_space=pltpu.SMEM),
                  pl.BlockSpec((TMR, D), lambda i: (i, 0)),
                  pl.BlockSpec((TMR, LANES), lambda i: (i, 0)),
                  pl.BlockSpec((TMR, LANES), lambda i: (i, 0)),
                  pl.BlockSpec((1, 1, D), lambda i: (mrow(i), 0, 5)),
                  pl.BlockSpec((1, D), lambda i: (0, 0)),
                  pl.BlockSpec(memory_space=pl.ANY)],
        out_specs=pl.BlockSpec((TMR, D), lambda i: (i, 0)),
        out_shape=jax.ShapeDtypeStruct((T, D), F32),
        scratch_shapes=[pltpu.VMEM((STG_ROWS, D), BF16), pltpu.SemaphoreType.DMA],
        compiler_params=_params("arbitrary"),
        name="combine",
    )(info, x, mi, gates, mod, fw, yb)


def _moe(x, nw, mod, wr, br, wgu, bgu, wdn, bdn, fw, lay, l, final):
    T, D = x.shape
    ntb = T // TMR
    h, mi, mf, blk, tot = _route(x, nw, mod, wr, br, lay)
    counts = tot[0, :N_EXPERTS].astype(I32)
    padded = (counts + MOE_BLK - 1) // MOE_BLK * MOE_BLK
    pad_ends = jnp.cumsum(padded)
    pad_starts = pad_ends - padded
    nblk = -(-(T * TOP_K + ntb * N_EXPERTS * (SEG_ALIGN - 1)) // MOE_BLK) + N_EXPERTS
    starts = jnp.arange(nblk, dtype=I32) * MOE_BLK
    block_e = jnp.minimum(jnp.sum((pad_ends[None, :] <= starts[:, None]).astype(I32), axis=1), N_EXPERTS - 1)
    n_used = (pad_ends[N_EXPERTS - 1:] // MOE_BLK).astype(I32)
    rep = lambda v: jnp.broadcast_to(v[None, :], (ntb, v.shape[0]))
    zeros_e = jnp.zeros((N_EXPERTS,), I32)
    info = jnp.concatenate([
        blk[:, 0, :N_EXPERTS], blk[:, 1, :N_EXPERTS], blk[:, 2, :N_EXPERTS] + pad_starts[None, :], rep(zeros_e),
        rep(padded - counts), rep(zeros_e), rep(pad_starts + counts),
        rep(jnp.concatenate([n_used, jnp.zeros((INFO_LANES - INFO_NUSED - 1,), I32)]))], axis=1).reshape(-1)
    eid = jnp.arange(N_EXPERTS, dtype=I32)
    has = padded > 0
    later = jnp.where(has[None, :] & (eid[None, :] > eid[:, None]), eid[None, :], N_EXPERTS)
    nxt_e = jnp.min(later, axis=1)
    nxt_e = jnp.where(nxt_e < N_EXPERTS, nxt_e, -1)
    slot_e = (jnp.cumsum(has.astype(I32)) - 1) % 2
    xb = _dispatch(info, h, mi, nblk * MOE_BLK)
    yb = _gmm(block_e, n_used, jnp.take(nxt_e, block_e), jnp.take(slot_e, block_e), xb, wgu, bgu, wdn, bdn, l)
    return _combine(info, x, mi, mf, mod, fw, yb, lay, final)


def _rope_tables(S):
    t = np.arange(S)
    row = (t // GRID_W).astype(np.float64)
    col = (t % GRID_W).astype(np.float64)
    lane = np.arange(LANES)
    j = lane % ROPE_PAIR
    inv = ROPE_THETA ** (-(j.astype(np.float32)) / np.float32(ROPE_PAIR))
    pos = np.where((lane % DA_QK) < ROPE_HALF, row[:, None], col[:, None]).astype(np.float32)
    ang = (pos * inv.astype(np.float32)[None, :]).astype(np.float32)
    sign = np.where((lane % ROPE_HALF) < ROPE_PAIR, -1.0, 1.0).astype(np.float32)
    cos_t = np.concatenate([np.cos(ang), np.ones((TM, LANES), np.float32)], axis=0)
    sin_t = np.concatenate([np.sin(ang) * sign[None, :], np.zeros((TM, LANES), np.float32)], axis=0)
    return cos_t.astype(np.float32), sin_t.astype(np.float32)


def kernel(x, c, ctx, c_ctx, w_ada, b_ada, norm1_w, w_in, da_lambda, da_subln_w, ssd_conv_w, ssd_conv_b,
           ssd_a_log, ssd_dt_bias, ssd_d, ssd_norm_w, gla_gk_up, gla_gk_b, gla_norm_w, w_out, norm2_w,
           w_router, b_router, w_gate_up, b_gate_up, w_down, b_down, final_norm_w):
    B, S, D = x.shape
    Lc = ctx.shape[1]
    depth = w_ada.shape[0]
    assert S % TMR == 0 and Lc % TM == 0 and (B * Lc) % TMR == 0 and S % GRID_W == 0
    lay = dict(B=B, S=S, Lc=Lc, nlb=(B * S) // TM, bpb=S // TM, cpb=Lc // TM)
    n_lat = B * S

    xs = jnp.concatenate([x.reshape(B * S, D), ctx.reshape(B * Lc, D)], axis=0)

    cc = jnp.zeros((8, D), F32).at[0:B].set(c).at[B].set(c_ctx)
    mod_all = _ada(cc, w_ada, b_ada)

    cos_t, sin_t = (jnp.asarray(t) for t in _rope_tables(S))

    for l in range(depth):
        last = l == depth - 1
        lam_init = 0.8 - 0.6 * math.exp(-0.3 * l)
        mod = mod_all[l, 0:B + 1].reshape(B + 1, 1, 6 * D)

        wi = w_in[l]
        misc_w = jnp.zeros((D, LANES), F32)
        misc_w = misc_w.at[:, MISC_DT:MISC_DT + N_DT].set(wi[:, O_DT:O_GLA])
        misc_w = misc_w.at[:, MISC_CODE:MISC_CODE + N_CODE].set(wi[:, O_CODE:O_CODE + N_CODE])
        w_re = jnp.concatenate([wi[:, 0:O_DT], wi[:, O_GLA:O_CODE], misc_w], axis=1).astype(BF16)
        a128 = jnp.zeros((1, LANES), F32).at[0, MISC_DT:MISC_DT + N_DT].set(
            -jnp.exp(ssd_a_log[l].astype(F32)).reshape(-1))
        bias128 = jnp.zeros((1, LANES), F32).at[0, MISC_DT:MISC_DT + N_DT].set(
            ssd_dt_bias[l].astype(F32).reshape(-1))
        wgk = jnp.zeros((2, LANES, GLA_KEY_WIDTH), F32)
        for d in range(2):
            wgk = wgk.at[d, MISC_CODE + d * GLA_RANK:MISC_CODE + (d + 1) * GLA_RANK, :].set(gla_gk_up[l, d])
        bgk = gla_gk_b[l].reshape(2, 1, GLA_KEY_WIDTH)
        dsk = jnp.repeat(ssd_d[l], SSD_HEAD_DIM).reshape(1, SSD_WIDTH)
        snw = ssd_norm_w[l].reshape(1, SSD_WIDTH)
        gnw = jnp.tile(gla_norm_w[l], GLA_HEADS).reshape(1, GLA_WIDTH)
        wr = jnp.zeros((D, LANES), F32).at[:, 0:N_EXPERTS].set(w_router[l])
        br = jnp.zeros((1, LANES), F32).at[0, 0:N_EXPERTS].set(b_router[l])

        q, k, v, z, xbc_act, gla_in, misc = _inproj(xs, norm1_w[l].reshape(1, D), mod, cos_t, sin_t, w_re,
                                                    ssd_conv_w[l], ssd_conv_b[l].reshape(1, SSD_CONV_CH), lay)

        a = _attention(q, k, v, da_lambda[l], da_subln_w[l].reshape(1, DA_V), lay, lam_init,
                       ctx_queries=not last)
        yf, yb, of, ob = _scans(xbc_act, gla_in, misc, a128, bias128, wgk, bgk, lay)

        nrows = n_lat if last else xs.shape[0]
        xs = _outproj(xs, a, yf, yb, xbc_act, z, of, ob, gla_in, dsk, snw, gnw,
                      w_out[l].astype(BF16), mod, lay, nrows)
        xs = _moe(xs, norm2_w[l].reshape(1, D), mod, wr, br, w_gate_up, b_gate_up, w_down, b_down,
                  final_norm_w.reshape(1, D), lay, l, last)

    return xs.reshape(B, S, D)
```

```python
import functools
import math

import numpy as np
import jax
import jax.numpy as jnp
from jax import lax
from jax.experimental import pallas as pl
from jax.experimental.pallas import tpu as pltpu

F32 = jnp.float32
BF16 = jnp.bfloat16
I32 = jnp.int32

GRID_W = 64
EPS = 1e-6
DA_HEADS = 4
DA_QK = 64
DA_V = 128
DA_WIDTH = 512
ROPE_THETA = 10000.0
SSD_HEADS = 4
SSD_HEAD_DIM = 64
SSD_WIDTH = 256
SSD_STATE = 64
SSD_CONV_CH = 512
GLA_HEADS = 4
GLA_DK = 32
GLA_DV = 64
GLA_KEY_WIDTH = 128
GLA_WIDTH = 256
GLA_RANK = 16
GLA_NORMALIZER = 16.0
N_EXPERTS = 32
TOP_K = 4
D_EXPERT = 1024
SWIGLU_LIMIT = 7.0
SWIGLU_ALPHA = 1.702

LANES = 128
TM = 256
SSD_Q = 128
GLA_Q = 64
GLA_SUB = 4
ATT_TQ = 256
ATT_KT = (2816, 768, 512, 256)
LOG2E = 1.4426950408889634
MOE_BLK = 512
TMR = 512
SEG_ALIGN = 16
STG_ROWS = TMR * TOP_K + N_EXPERTS * SEG_ALIGN
INFO_LANES = 256
INFO_TAIL = 128
INFO_NUSED = 224
VMEM_LIMIT = 56 * 1024 * 1024

ROPE_HALF = DA_QK // 2
ROPE_PAIR = ROPE_HALF // 2
GLA_QKV = 2 * GLA_KEY_WIDTH + GLA_WIDTH
GLA_SLAB = GLA_QKV + GLA_WIDTH
N_DT = 2 * SSD_HEADS
N_CODE = 2 * GLA_RANK
ADA_TN = 512

O_DT = 3 * DA_WIDTH + SSD_WIDTH + SSD_CONV_CH
O_GLA = O_DT + N_DT
O_CODE = O_GLA + GLA_SLAB
C_Q = 0
C_K = C_Q + DA_WIDTH
C_V = C_K + DA_WIDTH
C_Z = C_V + DA_WIDTH
C_XBC = C_Z + SSD_WIDTH
C_GLA = C_XBC + SSD_CONV_CH
C_MISC = C_GLA + GLA_SLAB
C_END = C_MISC + LANES
MISC_DT = 0
MISC_CODE = N_DT


def _sigmoid(x):
    return 1.0 / (1.0 + jnp.exp(-x))


def _softplus(x):
    return jnp.maximum(x, 0.0) + jnp.log(1.0 + jnp.exp(-jnp.abs(x)))


def _split2(a):
    hi = a.astype(BF16)
    lo = (a - hi.astype(F32)).astype(BF16)
    return hi, lo


def _split3(a):
    a1 = a.astype(BF16)
    r1 = a - a1.astype(F32)
    a2 = r1.astype(BF16)
    a3 = (r1 - a2.astype(F32)).astype(BF16)
    return a1, a2, a3


def _dot(a, b):
    return jnp.dot(a, b, preferred_element_type=F32)


def _dot_hi(a, b):
    a1, a2 = _split2(a)
    b1, b2 = _split2(b)
    return _dot(a1, b1) + (_dot(a1, b2) + _dot(a2, b1))


def _dot_exact_lhs(m, a):
    a1, a2, a3 = _split3(a)
    mb = m.astype(BF16)
    return _dot(mb, a1) + (_dot(mb, a2) + _dot(mb, a3))


def _dot_exact_rhs(a, m):
    a1, a2, a3 = _split3(a)
    mb = m.astype(BF16)
    return _dot(a1, mb) + (_dot(a2, mb) + _dot(a3, mb))


def _params(*sem):
    return pltpu.CompilerParams(dimension_semantics=sem, vmem_limit_bytes=VMEM_LIMIT)


def _ada_kernel(c_ref, w_ref, b_ref, o_ref):
    c = c_ref[...]
    s = c * _sigmoid(c)
    o_ref[0] = _dot_hi(s, w_ref[0]) + b_ref[0]


def _ada(cc, w_ada, b_ada):
    L, D, N = w_ada.shape
    tn = ADA_TN
    return pl.pallas_call(
        _ada_kernel,
        grid=(L, N // tn),
        in_specs=[pl.BlockSpec((8, D), lambda l, j: (0, 0)),
                  pl.BlockSpec((1, D, tn), lambda l, j: (l, 0, j)),
                  pl.BlockSpec((1, 1, tn), lambda l, j: (l, 0, j))],
        out_specs=pl.BlockSpec((1, 8, tn), lambda l, j: (l, 0, j)),
        out_shape=jax.ShapeDtypeStruct((L, 8, N), F32),
        compiler_params=_params("parallel", "parallel"),
        name="ada",
    )(cc, w_ada, b_ada.reshape(L, 1, N))


def _inproj_kernel(x_ref, xp_ref, xn_ref, nw_ref, sh_ref, sc_ref, cos_ref, sin_ref, w_ref, cw_ref, cb_ref,
                   q_ref, k_ref, v_ref, z_ref, xbc_ref, gla_ref, misc_ref, *, nlb, bpb, cpb):
    def modulated(xf):
        ms = jnp.mean(xf * xf, axis=-1, keepdims=True)
        return ((xf * lax.rsqrt(ms + EPS) * nw_ref[...]) * (1.0 + sc_ref[0]) + sh_ref[0]).astype(BF16)

    hb = modulated(x_ref[...])
    cos = cos_ref[...]
    sin = sin_ref[...]
    lane = lax.broadcasted_iota(I32, cos.shape, 1)
    first = (lane % ROPE_HALF) < ROPE_PAIR

    def rope(p):
        outs = []
        for hd in range(DA_HEADS):
            ph = p[:, hd * LANES:(hd + 1) * LANES]
            sw = jnp.where(first, pltpu.roll(ph, LANES - ROPE_PAIR, 1), pltpu.roll(ph, ROPE_PAIR, 1))
            outs.append(ph * cos + sw * sin)
        return jnp.concatenate(outs, axis=1)

    q = rope(_dot(hb, w_ref[:, C_Q:C_K])) * (DA_QK ** -0.5 * LOG2E)
    q_ref[...] = q.astype(BF16)
    k_ref[...] = rope(_dot(hb, w_ref[:, C_K:C_V])).astype(BF16)
    v_ref[...] = _dot(hb, w_ref[:, C_V:C_Z]).astype(BF16)
    z_ref[...] = _dot(hb, w_ref[:, C_Z:C_XBC])
    gla_ref[...] = _dot(hb, w_ref[:, C_GLA:C_MISC])
    misc_ref[...] = _dot(hb, w_ref[:, C_MISC:C_END])

    xbc = _dot(hb, w_ref[:, C_XBC:C_GLA])
    halo = _dot(modulated(jnp.concatenate([xp_ref[...], xn_ref[...]], axis=0)), w_ref[:, C_XBC:C_GLA])
    i = pl.program_id(0)
    pos = jnp.where(i < nlb, i % bpb, (i - nlb) % cpb)
    last = jnp.where(i < nlb, bpb - 1, cpb - 1)
    prev_row = halo[7:8, :] * (pos != 0).astype(F32)
    next_row = halo[8:9, :] * (pos != last).astype(F32)
    ridx = lax.broadcasted_iota(I32, xbc.shape, 0)
    xm1 = jnp.where(ridx == 0, prev_row, pltpu.roll(xbc, 1, 0))
    xp1 = jnp.where(ridx == xbc.shape[0] - 1, next_row, pltpu.roll(xbc, xbc.shape[0] - 1, 0))
    cw = cw_ref[...]
    y = cw[0:1] * xm1 + cw[1:2] * xbc + cw[2:3] * xp1 + cb_ref[...]
    xbc_ref[...] = y * _sigmoid(y)


def _inproj(x, nw, mod, cos_t, sin_t, w, cw, cb, lay):
    T, D = x.shape
    nlb, bpb, B = lay["nlb"], lay["bpb"], lay["B"]
    r8 = TM // 8
    mrow = lambda i: jnp.where(i < nlb, i // bpb, B)
    prow = lambda i: jnp.where(i < nlb, i % bpb, bpb)
    row = lambda n: pl.BlockSpec((TM, n), lambda i: (i, 0))
    spb = bpb + lay["cpb"]
    kvrow = lambda i: (jnp.where(i < nlb, (i // bpb) * spb + lay["cpb"] + i % bpb,
                                 ((i - nlb) // lay["cpb"]) * spb + (i - nlb) % lay["cpb"]), 0)
    kv = pl.BlockSpec((TM, DA_WIDTH), kvrow)
    outs = [(DA_WIDTH, BF16), (DA_WIDTH, BF16), (DA_WIDTH, BF16), (SSD_WIDTH, F32), (SSD_CONV_CH, F32),
            (GLA_SLAB, F32), (LANES, F32)]
    out_specs = [row(n) for n, _ in outs]
    out_specs[1] = kv
    out_specs[2] = kv
    return pl.pallas_call(
        functools.partial(_inproj_kernel, nlb=nlb, bpb=bpb, cpb=lay["cpb"]),
        grid=(T // TM,),
        in_specs=[row(D),
                  pl.BlockSpec((8, D), lambda i: (jnp.maximum(i * r8 - 1, 0), 0)),
                  pl.BlockSpec((8, D), lambda i: (jnp.minimum((i + 1) * r8, T // 8 - 1), 0)),
                  pl.BlockSpec((1, D), lambda i: (0, 0)),
                  pl.BlockSpec((1, 1, D), lambda i: (mrow(i), 0, 0)),
                  pl.BlockSpec((1, 1, D), lambda i: (mrow(i), 0, 1)),
                  pl.BlockSpec((TM, LANES), lambda i: (prow(i), 0)),
                  pl.BlockSpec((TM, LANES), lambda i: (prow(i), 0)),
                  pl.BlockSpec((D, C_END), lambda i: (0, 0)),
                  pl.BlockSpec((3, SSD_CONV_CH), lambda i: (0, 0)),
                  pl.BlockSpec((1, SSD_CONV_CH), lambda i: (0, 0))],
        out_specs=out_specs,
        out_shape=[jax.ShapeDtypeStruct((T, n), dt) for n, dt in outs],
        compiler_params=_params("parallel"),
        name="inproj",
    )(x, x, x, nw, mod, mod, cos_t, sin_t, w, cw, cb)


def _lane_fold(x, op):
    f = x[:, 0:LANES]
    for i in range(1, x.shape[1] // LANES):
        f = op(f, x[:, i * LANES:(i + 1) * LANES])
    return f


def _attn_stack_maps(q_ref, q2_sc):
    tq = q_ref.shape[0]
    q = q_ref[...]
    lane = lax.broadcasted_iota(I32, q.shape, 1)
    zero = jnp.zeros_like(q)
    q2_sc[0:tq, :] = jnp.where(lane < DA_QK, q, zero)
    q2_sc[tq:2 * tq, :] = jnp.where(lane >= DA_QK, q, zero)


def _attn_scores_tile(q2_sc, k_ref, s_sc, slot, mf_sc, j, kt):
    kj = k_ref[pl.ds(pl.multiple_of(j * kt, kt), kt), :]
    s = lax.dot_general(q2_sc[...], kj, (((1,), (1,)), ((), ())), preferred_element_type=F32)
    s_sc[slot, j, :, 0:kt] = s
    mf_sc[...] = jnp.maximum(mf_sc[...], _lane_fold(s, jnp.maximum))


def _attn_weigh_tile(v_ref, s_sc, slot, p_sc, l_sc, acc_sc, m, j, kt):
    sub = next(w for w in (512, 256) if kt % w == 0)
    for c0 in range(0, kt, sub):
        p = jnp.exp2(s_sc[slot, j, :, c0:c0 + sub] - m)
        l_sc[...] += _lane_fold(p, jnp.add)
        p_sc[:, c0:c0 + sub] = p.astype(BF16)
    vj = v_ref[pl.ds(pl.multiple_of(j * kt, kt), kt), :]
    acc_sc[...] += _dot(p_sc[:, 0:kt], vj)


def _attn_finish(lam_ref, w_ref, o_ref, l_sc, acc_sc, lam_init):
    tq = o_ref.shape[0]
    lp = lam_ref[...]
    lam = (jnp.exp(jnp.sum(lp[0:1] * lp[1:2], axis=1, keepdims=True))
           - jnp.exp(jnp.sum(lp[2:3] * lp[3:4], axis=1, keepdims=True)) + lam_init)
    acc = acc_sc[...]
    l = jnp.sum(l_sc[...], axis=1, keepdims=True)
    o = acc[0:tq] / l[0:tq] - lam * (acc[tq:2 * tq] / l[tq:2 * tq])
    ms = jnp.mean(o * o, axis=-1, keepdims=True)
    o_ref[...] = (o * lax.rsqrt(ms + EPS) * w_ref[...]) * (1.0 - lam_init)


def _attn_kernel(q_ref, k_ref, v_ref, lam_ref, w_ref, o_ref, q2_sc, s_sc, p_sc, mf_sc, l_sc, acc_sc,
                 *, nt, kt, lam_init):
    _attn_stack_maps(q_ref, q2_sc)
    mf_sc[...] = jnp.full(mf_sc.shape, -jnp.inf, F32)
    l_sc[...] = jnp.zeros(l_sc.shape, F32)
    acc_sc[...] = jnp.zeros(acc_sc.shape, F32)

    def scores(j, carry):
        _attn_scores_tile(q2_sc, k_ref, s_sc, 0, mf_sc, j, kt)
        return carry

    lax.fori_loop(0, nt, scores, 0)
    m = jnp.max(mf_sc[...], axis=1, keepdims=True)

    def weigh(j, carry):
        _attn_weigh_tile(v_ref, s_sc, 0, p_sc, l_sc, acc_sc, m, j, kt)
        return carry

    lax.fori_loop(0, nt, weigh, 0)
    _attn_finish(lam_ref, w_ref, o_ref, l_sc, acc_sc, lam_init)


def _attn_pipe_kernel(q_ref, k_ref, v_ref, lam_ref, w_ref, o_ref, q2_sc, s_sc, p_sc, mf_sc, m_sc, l_sc, acc_sc,
                      *, nq, nt, kt, lam_init):
    t = pl.program_id(2)

    @pl.when(t < nq)
    def _():
        _attn_stack_maps(q_ref, q2_sc)
        mf_sc[...] = jnp.full(mf_sc.shape, -jnp.inf, F32)

    @pl.when(t > 0)
    def _():
        l_sc[...] = jnp.zeros(l_sc.shape, F32)
        acc_sc[...] = jnp.zeros(acc_sc.shape, F32)

    def run(slot_a, slot_b):
        m_prev = None if slot_b is None else m_sc[slot_b]

        def body(j, carry):
            if slot_b is not None:
                _attn_weigh_tile(v_ref, s_sc, slot_b, p_sc, l_sc, acc_sc, m_prev, j, kt)
            if slot_a is not None:
                _attn_scores_tile(q2_sc, k_ref, s_sc, slot_a, mf_sc, j, kt)
            return carry

        lax.fori_loop(0, nt, body, 0)
        if slot_a is not None:
            m_sc[slot_a] = jnp.max(mf_sc[...], axis=1, keepdims=True)

    inner = (t > 0) & (t < nq)
    pl.when(t == 0)(lambda: run(0, None))
    pl.when(inner & (t % 2 == 1))(lambda: run(1, 0))
    pl.when(inner & (t % 2 == 0))(lambda: run(0, 1))
    pl.when(t == nq)(lambda: run(None, (nq - 1) % 2))

    @pl.when(t > 0)
    def _():
        _attn_finish(lam_ref, w_ref, o_ref, l_sc, acc_sc, lam_init)


def _attention(q, k, v, lam_p, subln_w, lay, lam_init, *, ctx_queries):
    B, S, Lc = lay["B"], lay["S"], lay["Lc"]
    nkeys = Lc + S
    assert nkeys % Lc == 0
    small = [pl.BlockSpec((4, DA_QK), lambda b, h, t: (0, 0)), pl.BlockSpec((1, DA_V), lambda b, h, t: (0, 0))]
    stacked = lambda n, dt: pltpu.VMEM((2 * ATT_TQ, n), dt)

    kt = next(t for t in ATT_KT if nkeys % t == 0)
    nq = S // ATT_TQ
    kmap = lambda b, h, t: (b, h)
    a_lat = pl.pallas_call(
        functools.partial(_attn_pipe_kernel, nq=nq, nt=nkeys // kt, kt=kt, lam_init=lam_init),
        grid=(B, DA_HEADS, nq + 1),
        in_specs=[pl.BlockSpec((ATT_TQ, LANES), lambda b, h, t: (b * nq + jnp.minimum(t, nq - 1), h)),
                  pl.BlockSpec((nkeys, LANES), kmap),
                  pl.BlockSpec((nkeys, LANES), kmap)] + small,
        out_specs=pl.BlockSpec((ATT_TQ, LANES), lambda b, h, t: (b * nq + jnp.maximum(t - 1, 0), h)),
        out_shape=jax.ShapeDtypeStruct((B * S, DA_WIDTH), F32),
        scratch_shapes=[stacked(LANES, BF16),
                        pltpu.VMEM((2, nkeys // kt, 2 * ATT_TQ, kt), F32),
                        stacked(kt, BF16), stacked(LANES, F32),
                        pltpu.VMEM((2, 2 * ATT_TQ, 1), F32),
                        stacked(LANES, F32), stacked(LANES, F32)],
        compiler_params=_params("parallel", "parallel", "arbitrary"),
        name="attn",
    )(q, k, v, lam_p, subln_w)
    if not ctx_queries:
        return a_lat

    ktc = next(t for t in ATT_KT if Lc % t == 0)
    nqc = Lc // ATT_TQ
    cmap = lambda b, h, t: (b * (nkeys // Lc), h)
    a_ctx = pl.pallas_call(
        functools.partial(_attn_kernel, nt=Lc // ktc, kt=ktc, lam_init=lam_init),
        grid=(B, DA_HEADS, nqc),
        in_specs=[pl.BlockSpec((ATT_TQ, LANES), lambda b, h, t: ((B * S) // ATT_TQ + b * nqc + t, h)),
                  pl.BlockSpec((Lc, LANES), cmap),
                  pl.BlockSpec((Lc, LANES), cmap)] + small,
        out_specs=pl.BlockSpec((ATT_TQ, LANES), lambda b, h, t: (b * nqc + t, h)),
        out_shape=jax.ShapeDtypeStruct((B * Lc, DA_WIDTH), F32),
        scratch_shapes=[stacked(LANES, BF16),
                        pltpu.VMEM((1, Lc // ktc, 2 * ATT_TQ, ktc), F32),
                        stacked(ktc, BF16), stacked(LANES, F32), stacked(LANES, F32), stacked(LANES, F32)],
        compiler_params=_params("parallel", "parallel", "arbitrary"),
        name="attn_ctx",
    )(q, k, v, lam_p, subln_w)
    return jnp.concatenate([a_lat, a_ctx], axis=0)


def _ssd_dir(xbc, misc, a128, bias128, hs_ref, dirn, rev):
    Q = xbc.shape[0]
    dt = _softplus(misc + bias128)
    a = dt * a128
    ii = lax.broadcasted_iota(I32, (Q, Q), 0)
    jj = lax.broadcasted_iota(I32, (Q, Q), 1)
    M = (jj >= ii) if rev else (jj <= ii)
    acs = _dot_exact_lhs(M.astype(F32), a)
    acs_t = acs.T
    dt_t = dt.T
    xs = xbc[:, 0:SSD_WIDTH]
    Bt = xbc[:, SSD_WIDTH:SSD_WIDTH + 128].T
    Cm = xbc[:, SSD_WIDTH + 128:SSD_WIDTH + 256]
    last = 0 if rev else Q - 1
    ys = []
    for g in range(2):
        Cg = Cm[:, g * SSD_STATE:(g + 1) * SSD_STATE]
        Btg = Bt[g * SSD_STATE:(g + 1) * SSD_STATE, :]
        CB = _dot(Cg, Btg)
        for hh in range(2):
            h = 2 * g + hh
            li = dirn * SSD_HEADS + h
            acs_c = acs[:, li:li + 1]
            acs_r = acs_t[li:li + 1, :]
            dt_r = dt_t[li:li + 1, :]
            decay = jnp.exp(jnp.where(M, acs_c - acs_r, -jnp.inf))
            sc = CB * decay * dt_r
            x_h = xs[:, h * SSD_HEAD_DIM:(h + 1) * SSD_HEAD_DIM]
            hs = hs_ref[h]
            y = _dot(sc, x_h) + _dot(Cg * jnp.exp(acs_c), hs)
            tot = acs_r[:, last:last + 1]
            w_end = jnp.exp(tot - acs_r) * dt_r
            hs_ref[h] = jnp.exp(tot) * hs + _dot(Btg * w_end, x_h)
            ys.append(y)
    return jnp.concatenate(ys, axis=1)


def _scan_maps(lay, rows):
    ncc, ncl = lay["Lc"] // rows, lay["S"] // rows
    base = (lay["B"] * lay["S"]) // rows
    fmap = lambda b, c: (jnp.where(c < ncc, base + b * ncc + c, b * ncl + c - ncc), 0)
    bmap = lambda b, c: (jnp.where(c < ncc, base + b * ncc + (ncc - 1 - c),
                                   b * ncl + (ncl - 1 - (c - ncc))), 0)
    return ncc + ncl, fmap, bmap


def _gla_block(blk, misc, wgk, bgk, s_ref, rev):
    R = blk.shape[0]
    kw = GLA_KEY_WIDTH
    q, k, v = blk[:, 0:kw], blk[:, kw:2 * kw], blk[:, 2 * kw:2 * kw + GLA_WIDTH]
    pre = _dot_hi(misc, wgk) + bgk
    gk = (jnp.minimum(pre, 0.0) - jnp.log(1.0 + jnp.exp(-jnp.abs(pre)))) / GLA_NORMALIZER
    ii = lax.broadcasted_iota(I32, (R, R), 0)
    jj = lax.broadcasted_iota(I32, (R, R), 1)
    same = (ii // GLA_Q) == (jj // GLA_Q)
    M = same & ((jj >= ii) if rev else (jj <= ii))
    b = _dot_exact_lhs(M.astype(F32), gk)
    tot = _dot_exact_lhs(same.astype(F32), gk)
    q_t = (q * (GLA_DK ** -0.5)) * jnp.exp(b)
    k_t = k * jnp.exp(-b)
    k_end_t = (k * jnp.exp(tot - b)).T
    gk_t = gk.T
    lane_k = lax.broadcasted_iota(I32, (1, kw), 1) // GLA_DK
    lane_v = lax.broadcasted_iota(I32, (1, GLA_WIDTH), 1) // GLA_DV
    o = jnp.zeros((R, GLA_WIDTH), F32)
    for h in range(GLA_HEADS):
        qh = jnp.where(lane_k == h, q_t, 0.0)
        att = lax.dot_general(qh, k_t, (((1,), (1,)), ((), ())), preferred_element_type=F32)
        o = o + _dot(jnp.where(M, att, 0.0), jnp.where(lane_v == h, v, 0.0))
    col_chunk = lax.broadcasted_iota(I32, (1, R), 1) // GLA_Q
    S = s_ref[...]
    bd = (lax.broadcasted_iota(I32, S.shape, 0) // GLA_DK) == (lax.broadcasted_iota(I32, S.shape, 1) // GLA_DV)
    nsub = R // GLA_Q
    inter = [None] * nsub
    for c in (range(nsub - 1, -1, -1) if rev else range(nsub)):
        sel = col_chunk == c
        inter[c] = _dot(q_t[c * GLA_Q:(c + 1) * GLA_Q], S)
        decay = jnp.exp(jnp.sum(jnp.where(sel, gk_t, 0.0), axis=1, keepdims=True))
        S = jnp.where(bd, S * decay + _dot(jnp.where(sel, k_end_t, 0.0), v), 0.0)
    s_ref[...] = S
    return o + jnp.concatenate(inter, axis=0)


def _scan_kernel(xf_ref, gf_ref, mf_ref, xb_ref, gb_ref, mb_ref, a_ref, bias_ref, wgk_ref, bgk_ref,
                 yf_ref, yb_ref, of_ref, ob_ref, hf_sc, hb_sc, sf_sc, sb_sc):
    @pl.when(pl.program_id(1) == 0)
    def _():
        for sc in (hf_sc, hb_sc, sf_sc, sb_sc):
            sc[...] = jnp.zeros(sc.shape, F32)

    a128 = a_ref[...]
    bias128 = bias_ref[...]
    mf = mf_ref[...]
    mb = mb_ref[...]
    nsub = xf_ref.shape[0] // SSD_Q
    for c in range(nsub):
        r0 = c * SSD_Q
        yf_ref[r0:r0 + SSD_Q, :] = _ssd_dir(xf_ref[r0:r0 + SSD_Q, :], mf[r0:r0 + SSD_Q], a128, bias128,
                                            hf_sc, 0, False)
        r1 = (nsub - 1 - c) * SSD_Q
        yb_ref[r1:r1 + SSD_Q, :] = _ssd_dir(xb_ref[r1:r1 + SSD_Q, :], mb[r1:r1 + SSD_Q], a128, bias128,
                                            hb_sc, 1, True)
    of_ref[...] = _gla_block(gf_ref[...], mf, wgk_ref[0], bgk_ref[0], sf_sc, False)
    ob_ref[...] = _gla_block(gb_ref[...], mb, wgk_ref[1], bgk_ref[1], sb_sc, True)


def _scans(xbc_act, gla_in, misc, a128, bias128, wgk, bgk, lay):
    T = xbc_act.shape[0]
    R = GLA_Q * GLA_SUB
    assert R % SSD_Q == 0
    nc, fmap, bmap = _scan_maps(lay, R)
    vec = pl.BlockSpec((1, LANES), lambda b, c: (0, 0))
    ins = lambda m: [pl.BlockSpec((R, SSD_CONV_CH), m), pl.BlockSpec((R, GLA_QKV), m), pl.BlockSpec((R, LANES), m)]
    outs = lambda n: [pl.BlockSpec((R, n), fmap), pl.BlockSpec((R, n), bmap)]
    return pl.pallas_call(
        _scan_kernel,
        grid=(lay["B"], nc),
        in_specs=ins(fmap) + ins(bmap) + [
            vec, vec,
            pl.BlockSpec((2, LANES, GLA_KEY_WIDTH), lambda b, c: (0, 0, 0)),
            pl.BlockSpec((2, 1, GLA_KEY_WIDTH), lambda b, c: (0, 0, 0))],
        out_specs=outs(SSD_WIDTH) + outs(GLA_WIDTH),
        out_shape=[jax.ShapeDtypeStruct((T, SSD_WIDTH), F32)] * 2 + [jax.ShapeDtypeStruct((T, GLA_WIDTH), F32)] * 2,
        scratch_shapes=[pltpu.VMEM((SSD_HEADS, SSD_STATE, SSD_HEAD_DIM), F32)] * 2
        + [pltpu.VMEM((GLA_KEY_WIDTH, GLA_WIDTH), F32)] * 2,
        compiler_params=_params("parallel", "arbitrary"),
        name="scans",
    )(xbc_act, gla_in, misc, xbc_act, gla_in, misc, a128, bias128, wgk, bgk)


def _outproj_kernel(x_ref, a_ref, yf_ref, yb_ref, xs_ref, z_ref, of_ref, ob_ref, g_ref,
                    dsk_ref, snw_ref, gnw_ref, w_ref, g1_ref, o_ref):
    y = yf_ref[...] + yb_ref[...] + dsk_ref[...] * xs_ref[...]
    z = z_ref[...]
    gs = y * (z * _sigmoid(z))
    half = SSD_WIDTH // 2
    parts = []
    for grp in range(2):
        seg = gs[:, grp * half:(grp + 1) * half]
        parts.append(seg * lax.rsqrt(jnp.mean(seg * seg, axis=-1, keepdims=True) + EPS))
    s = jnp.concatenate(parts, axis=1) * snw_ref[...]
    o = of_ref[...] + ob_ref[...]
    bd = ((lax.broadcasted_iota(I32, (GLA_WIDTH, GLA_WIDTH), 0) // GLA_DV)
          == (lax.broadcasted_iota(I32, (GLA_WIDTH, GLA_WIDTH), 1) // GLA_DV))
    ms = _dot_exact_rhs(o * o, jnp.where(bd, 1.0 / GLA_DV, 0.0))
    g = g_ref[...]
    c = (o * lax.rsqrt(ms + EPS) * gnw_ref[...]) * (g * _sigmoid(g))
    mix = jnp.concatenate([a_ref[...], s, c], axis=1).astype(BF16)
    o_ref[...] = x_ref[...] + g1_ref[0] * _dot(mix, w_ref[...])


def _outproj(x, a, yf, yb, xbc_act, z, of, ob, gla_in, dsk, snw, gnw, w_out, mod, lay, nrows):
    D = x.shape[1]
    nlb, bpb, B = lay["nlb"], lay["bpb"], lay["B"]
    mrow = lambda i: jnp.where(i < nlb, i // bpb, B)
    row = lambda n, cb=0: pl.BlockSpec((TM, n), lambda i: (i, cb))
    vec = lambda n: pl.BlockSpec((1, n), lambda i: (0, 0))
    return pl.pallas_call(
        _outproj_kernel,
        grid=(nrows // TM,),
        in_specs=[row(D), row(DA_WIDTH), row(SSD_WIDTH), row(SSD_WIDTH), row(SSD_WIDTH), row(SSD_WIDTH),
                  row(GLA_WIDTH), row(GLA_WIDTH), row(GLA_WIDTH, 2),
                  vec(SSD_WIDTH), vec(SSD_WIDTH), vec(GLA_WIDTH),
                  pl.BlockSpec((D, D), lambda i: (0, 0)),
                  pl.BlockSpec((1, 1, D), lambda i: (mrow(i), 0, 2))],
        out_specs=row(D),
        out_shape=jax.ShapeDtypeStruct((nrows, D), F32),
        compiler_params=_params("parallel"),
        name="outproj",
    )(x, a, yf, yb, xbc_act, z, of, ob, gla_in, dsk, snw, gnw, w_out, mod)


def _route_kernel(x_ref, nw_ref, sh_ref, sc_ref, wr_ref, br_ref,
                  h_ref, mi_ref, mf_ref, blk_ref, tot_ref, carry_sc):
    i = pl.program_id(0)

    @pl.when(i == 0)
    def _():
        carry_sc[...] = jnp.zeros(carry_sc.shape, F32)

    xf = x_ref[...]
    ms = jnp.mean(xf * xf, axis=-1, keepdims=True)
    h = (xf * lax.rsqrt(ms + EPS) * nw_ref[...]) * (1.0 + sc_ref[0]) + sh_ref[0]
    h_ref[...] = h
    tm = h.shape[0]
    lane = lax.broadcasted_iota(I32, (tm, LANES), 1)
    logits = jnp.where(lane < N_EXPERTS, _dot_hi(h, wr_ref[...]) + br_ref[...], -jnp.inf)
    vals, idxs, hots = [], [], []
    l = logits
    for _ in range(TOP_K):
        m = jnp.max(l, axis=1, keepdims=True)
        idx = jnp.min(jnp.where(l == m, lane, LANES), axis=1, keepdims=True)
        hot = lane == idx
        vals.append(m)
        idxs.append(idx)
        hots.append(hot)
        l = jnp.where(hot, -jnp.inf, l)
    es = [jnp.exp(v - vals[0]) for v in vals]
    den = es[0] + es[1] + es[2] + es[3]
    hot_all = jnp.zeros((tm, LANES), F32)
    for hot in hots:
        hot_all = hot_all + hot.astype(F32)
    ii = lax.broadcasted_iota(I32, (tm, tm), 0)
    jj = lax.broadcasted_iota(I32, (tm, tm), 1)
    before = _dot((jj < ii).astype(BF16), hot_all.astype(BF16))
    cnt = jnp.sum(hot_all, axis=0, keepdims=True)
    cntp = jnp.floor((cnt + (SEG_ALIGN - 1.0)) * (1.0 / SEG_ALIGN)) * SEG_ALIGN
    ei = lax.broadcasted_iota(I32, (LANES, LANES), 0)
    ej = lax.broadcasted_iota(I32, (LANES, LANES), 1)
    units = jnp.broadcast_to(cntp * (1.0 / SEG_ALIGN), (8, LANES)).astype(BF16)
    seg = _dot(units, (ei < ej).astype(BF16))[0:1] * SEG_ALIGN
    pos_e = seg + before
    mi = jnp.zeros((tm, LANES), I32)
    mf = jnp.zeros((tm, LANES), F32)
    for kk in range(TOP_K):
        spos = jnp.sum(jnp.where(hots[kk], pos_e, 0.0), axis=1, keepdims=True).astype(I32)
        mi = jnp.where(lane == kk, idxs[kk], mi)
        mi = jnp.where(lane == TOP_K + kk, spos, mi)
        mf = jnp.where(lane == kk, es[kk] / den, mf)
    mi_ref[...] = mi
    mf_ref[...] = mf
    rowi = lax.broadcasted_iota(I32, (8, LANES), 0)
    info = jnp.where(rowi == 0, cntp, jnp.where(rowi == 1, seg, jnp.where(rowi == 2, carry_sc[...], 0.0)))
    blk_ref[0] = info.astype(I32)
    carry_sc[...] = carry_sc[...] + cntp
    tot_ref[...] = carry_sc[...]


def _route(x, nw, mod, wr, br, lay):
    T, D = x.shape
    B, S = lay["B"], lay["S"]
    mrow = lambda i: jnp.where(i < (B * S) // TMR, i // (S // TMR), B)
    row = lambda n: pl.BlockSpec((TMR, n), lambda i: (i, 0))
    return pl.pallas_call(
        _route_kernel,
        grid=(T // TMR,),
        in_specs=[row(D),
                  pl.BlockSpec((1, D), lambda i: (0, 0)),
                  pl.BlockSpec((1, 1, D), lambda i: (mrow(i), 0, 3)),
                  pl.BlockSpec((1, 1, D), lambda i: (mrow(i), 0, 4)),
                  pl.BlockSpec((D, LANES), lambda i: (0, 0)),
                  pl.BlockSpec((1, LANES), lambda i: (0, 0))],
        out_specs=[row(D), row(LANES), row(LANES),
                   pl.BlockSpec((1, 8, LANES), lambda i: (i, 0, 0)),
                   pl.BlockSpec((1, LANES), lambda i: (0, 0))],
        out_shape=[jax.ShapeDtypeStruct((T, D), F32), jax.ShapeDtypeStruct((T, LANES), I32),
                   jax.ShapeDtypeStruct((T, LANES), F32),
                   jax.ShapeDtypeStruct((T // TMR, 8, LANES), I32),
                   jax.ShapeDtypeStruct((1, LANES), F32)],
        scratch_shapes=[pltpu.VMEM((1, LANES), F32)],
        compiler_params=_params("arbitrary"),
        name="route",
    )(x, nw, mod, mod, wr, br)


def _segment_starts(info_ref, base, make_copy):
    def start_expert(e, carry):
        n = info_ref[base + e] // SEG_ALIGN
        src0 = info_ref[base + N_EXPERTS + e]
        dst0 = info_ref[base + 2 * N_EXPERTS + e]

        def start_chunk(c, carry):
            make_copy(pl.multiple_of(src0 + c * SEG_ALIGN, SEG_ALIGN),
                      pl.multiple_of(dst0 + c * SEG_ALIGN, SEG_ALIGN)).start()
            return carry

        lax.fori_loop(0, n, start_chunk, 0)
        return carry

    lax.fori_loop(0, N_EXPERTS, start_expert, 0)


def _segment_waits(info_ref, base, make_copy):
    total = lax.fori_loop(0, N_EXPERTS, lambda e, n: n + info_ref[base + e] // SEG_ALIGN, 0)

    def wait_chunk(c, carry):
        make_copy(0, 0).wait()
        return carry

    lax.fori_loop(0, total, wait_chunk, 0)


def _segment_copies(info_ref, base, make_copy):
    _segment_starts(info_ref, base, make_copy)
    _segment_waits(info_ref, base, make_copy)


def _dispatch_kernel(info_ref, h_ref, mi_ref, xb_ref, stg_sc, sem):
    tm = h_ref.shape[0]
    rows = stg_sc.shape[0]
    seg_copy = lambda s, d: pltpu.make_async_copy(
        stg_sc.at[pl.ds(s, SEG_ALIGN)], xb_ref.at[pl.ds(d, SEG_ALIGN)], sem)

    @pl.when(pl.program_id(0) == 0)
    def _():
        stg_sc[0:MOE_BLK, :] = jnp.zeros((MOE_BLK, stg_sc.shape[1]), BF16)
        _segment_copies(info_ref, INFO_TAIL, seg_copy)

        def spare_copy(j):
            return pltpu.make_async_copy(stg_sc.at[pl.ds(0, MOE_BLK)],
                                         xb_ref.at[pl.ds(pl.multiple_of(j * MOE_BLK, MOE_BLK), MOE_BLK)], sem)

        def start_spare(j, carry):
            spare_copy(j).start()
            return carry

        def wait_spare(j, carry):
            spare_copy(j).wait()
            return carry

        first_spare = info_ref[INFO_NUSED]
        lax.fori_loop(first_spare, xb_ref.shape[0] // MOE_BLK, start_spare, 0)
        lax.fori_loop(first_spare, xb_ref.shape[0] // MOE_BLK, wait_spare, 0)

    spos_t = mi_ref[...].astype(F32).T
    r = lax.broadcasted_iota(I32, (rows, tm), 0).astype(F32)
    pm = r == spos_t[TOP_K:TOP_K + 1]
    for kk in range(1, TOP_K):
        pm = pm | (r == spos_t[TOP_K + kk:TOP_K + kk + 1])
    stg_sc[...] = _dot(pm.astype(BF16), h_ref[...].astype(BF16)).astype(BF16)
    _segment_copies(info_ref, 0, seg_copy)


def _dispatch(info, h, mi, rows):
    T, D = h.shape
    return pl.pallas_call(
        _dispatch_kernel,
        grid=(T // TMR,),
        in_specs=[pl.BlockSpec((INFO_LANES,), lambda i: (i,), memory_space=pltpu.SMEM),
                  pl.BlockSpec((TMR, D), lambda i: (i, 0)),
                  pl.BlockSpec((TMR, LANES), lambda i: (i, 0))],
        out_specs=pl.BlockSpec(memory_space=pl.ANY),
        out_shape=jax.ShapeDtypeStruct((rows, D), BF16),
        scratch_shapes=[pltpu.VMEM((STG_ROWS, D), BF16), pltpu.SemaphoreType.DMA],
        compiler_params=_params("arbitrary"),
        name="dispatch",
    )(info, h, mi)


def _gmm_kernel(be_ref, nu_ref, nxt_ref, slot_ref, x_ref, wgu_hbm, bgu_ref, wdn_hbm, bdn_ref, o_ref,
                gu_buf, dn_buf, wgu_sc, wdn_sc, sems, *, layer):
    i = pl.program_id(0)
    e = be_ref[i]
    prev = be_ref[jnp.maximum(i - 1, 0)]

    def fetch(expert, slot):
        return (pltpu.make_async_copy(wgu_hbm.at[layer, expert], gu_buf.at[slot], sems.at[0, slot]),
                pltpu.make_async_copy(wdn_hbm.at[layer, expert], dn_buf.at[slot], sems.at[1, slot]))

    @pl.when(i == 0)
    def _():
        for cp in fetch(e, slot_ref[0]):
            cp.start()

    @pl.when(((i == 0) | (e != prev)) & (i < nu_ref[0]))
    def _():
        slot = slot_ref[i]
        for cp in fetch(e, slot):
            cp.wait()
        wgu_sc[...] = gu_buf[slot].astype(BF16)
        wdn_sc[...] = dn_buf[slot].astype(BF16)

        @pl.when(nxt_ref[i] >= 0)
        def _():
            for cp in fetch(nxt_ref[i], 1 - slot):
                cp.start()

    @pl.when(i < nu_ref[0])
    def _():
        gu = _dot(x_ref[...], wgu_sc[...]) + bgu_ref[0, 0]
        glu = jnp.minimum(gu[:, 0:D_EXPERT], SWIGLU_LIMIT)
        lin = jnp.clip(gu[:, D_EXPERT:2 * D_EXPERT], -SWIGLU_LIMIT, SWIGLU_LIMIT)
        act = glu * _sigmoid(SWIGLU_ALPHA * glu) * (lin + 1.0)
        o_ref[...] = (_dot(act.astype(BF16), wdn_sc[...]) + bdn_ref[0, 0]).astype(BF16)

    @pl.when(i >= nu_ref[0])
    def _():
        o_ref[...] = jnp.zeros(o_ref.shape, BF16)


def _gmm(block_e, n_used, nxt, slot, xb, wgu, bgu, wdn, bdn, l):
    P, D = xb.shape
    L, E, _, F2 = wgu.shape
    xmap = lambda i, be, nu, nx, sl: (jnp.minimum(i, nu[0] - 1), 0)
    bmap = lambda i, be, nu, nx, sl: (l, be[i], 0, 0)
    return pl.pallas_call(
        functools.partial(_gmm_kernel, layer=l),
        grid_spec=pltpu.PrefetchScalarGridSpec(
            num_scalar_prefetch=4,
            grid=(P // MOE_BLK,),
            in_specs=[pl.BlockSpec((MOE_BLK, D), xmap),
                      pl.BlockSpec(memory_space=pl.ANY),
                      pl.BlockSpec((1, 1, 1, F2), bmap),
                      pl.BlockSpec(memory_space=pl.ANY),
                      pl.BlockSpec((1, 1, 1, D), bmap)],
            out_specs=pl.BlockSpec((MOE_BLK, D), lambda i, be, nu, nx, sl: (i, 0)),
            scratch_shapes=[pltpu.VMEM((2, D, F2), F32), pltpu.VMEM((2, F2 // 2, D), F32),
                            pltpu.VMEM((D, F2), BF16), pltpu.VMEM((F2 // 2, D), BF16),
                            pltpu.SemaphoreType.DMA((2, 2))]),
        out_shape=jax.ShapeDtypeStruct((P, D), BF16),
        compiler_params=_params("arbitrary"),
        name="gmm",
    )(block_e, n_used, nxt, slot, xb, wgu, bgu.reshape(L, E, 1, F2), wdn, bdn.reshape(L, E, 1, D))


def _combine_kernel(info_ref, x_ref, mi_ref, gate_ref, g2_ref, fw_ref, yb_ref, o_ref, stg_sc, sem, *, final):
    tm = x_ref.shape[0]
    rows = stg_sc.shape[0]

    @pl.when(pl.program_id(0) == 0)
    def _():
        stg_sc[...] = jnp.zeros(stg_sc.shape, BF16)

    _segment_copies(info_ref, 0, lambda s, d: pltpu.make_async_copy(
        yb_ref.at[pl.ds(d, SEG_ALIGN)], stg_sc.at[pl.ds(s, SEG_ALIGN)], sem))

    spos = mi_ref[...].astype(F32)
    gate = gate_ref[...]
    r = lax.broadcasted_iota(I32, (tm, rows), 1).astype(F32)
    g = jnp.where(r == spos[:, TOP_K:TOP_K + 1], gate[:, 0:1], 0.0)
    for kk in range(1, TOP_K):
        g = g + jnp.where(r == spos[:, TOP_K + kk:TOP_K + kk + 1], gate[:, kk:kk + 1], 0.0)
    g_hi, g_lo = _split2(g)
    y = stg_sc[...]
    out = x_ref[...] + g2_ref[0] * (_dot(g_hi, y) + _dot(g_lo, y))
    if final:
        ms = jnp.mean(out * out, axis=-1, keepdims=True)
        out = out * lax.rsqrt(ms + EPS) * fw_ref[...]
    o_ref[...] = out


def _combine(info, x, mi, gates, mod, fw, yb, lay, final):
    T, D = x.shape
    B, S = lay["B"], lay["S"]
    mrow = lambda i: jnp.where(i < (B * S) // TMR, i // (S // TMR), B)
    return pl.pallas_call(
        functools.partial(_combine_kernel, final=final),
        grid=(T // TMR,),
        in_specs=[pl.BlockSpec((INFO_LANES,), lambda i: (i,), memory_space=pltpu.SMEM),
                  pl.BlockSpec((TMR, D), lambda i: (i, 0)),
                  pl.BlockSpec((TMR, LANES), lambda i: (i, 0)),
                  pl.BlockSpec((TMR, LANES), lambda i: (i, 0)),
                  pl.BlockSpec((1, 1, D), lambda i: (mrow(i), 0, 5)),
                  pl.BlockSpec((1, D), lambda i: (0, 0)),
                  pl.BlockSpec(memory_space=pl.ANY)],
        out_specs=pl.BlockSpec((TMR, D), lambda i: (i, 0)),
        out_shape=jax.ShapeDtypeStruct((T, D), F32),
        scratch_shapes=[pltpu.VMEM((STG_ROWS, D), BF16), pltpu.SemaphoreType.DMA],
        compiler_params=_params("arbitrary"),
        name="combine",
    )(info, x, mi, gates, mod, fw, yb)


def _moe(x, nw, mod, wr, br, wgu, bgu, wdn, bdn, fw, lay, l, final):
    T, D = x.shape
    ntb = T // TMR
    h, mi, mf, blk, tot = _route(x, nw, mod, wr, br, lay)
    counts = tot[0, :N_EXPERTS].astype(I32)
    padded = (counts + MOE_BLK - 1) // MOE_BLK * MOE_BLK
    pad_ends = jnp.cumsum(padded)
    pad_starts = pad_ends - padded
    nblk = -(-(T * TOP_K + ntb * N_EXPERTS * (SEG_ALIGN - 1)) // MOE_BLK) + N_EXPERTS
    starts = jnp.arange(nblk, dtype=I32) * MOE_BLK
    block_e = jnp.minimum(jnp.sum((pad_ends[None, :] <= starts[:, None]).astype(I32), axis=1), N_EXPERTS - 1)
    n_used = (pad_ends[N_EXPERTS - 1:] // MOE_BLK).astype(I32)
    rep = lambda v: jnp.broadcast_to(v[None, :], (ntb, v.shape[0]))
    zeros_e = jnp.zeros((N_EXPERTS,), I32)
    info = jnp.concatenate([
        blk[:, 0, :N_EXPERTS], blk[:, 1, :N_EXPERTS], blk[:, 2, :N_EXPERTS] + pad_starts[None, :], rep(zeros_e),
        rep(padded - counts), rep(zeros_e), rep(pad_starts + counts),
        rep(jnp.concatenate([n_used, jnp.zeros((INFO_LANES - INFO_NUSED - 1,), I32)]))], axis=1).reshape(-1)
    eid = jnp.arange(N_EXPERTS, dtype=I32)
    has = padded > 0
    later = jnp.where(has[None, :] & (eid[None, :] > eid[:, None]), eid[None, :], N_EXPERTS)
    nxt_e = jnp.min(later, axis=1)
    nxt_e = jnp.where(nxt_e < N_EXPERTS, nxt_e, -1)
    slot_e = (jnp.cumsum(has.astype(I32)) - 1) % 2
    xb = _dispatch(info, h, mi, nblk * MOE_BLK)
    yb = _gmm(block_e, n_used, jnp.take(nxt_e, block_e), jnp.take(slot_e, block_e), xb, wgu, bgu, wdn, bdn, l)
    return _combine(info, x, mi, mf, mod, fw, yb, lay, final)


def _rope_tables(S):
    t = np.arange(S)
    row = (t // GRID_W).astype(np.float64)
    col = (t % GRID_W).astype(np.float64)
    lane = np.arange(LANES)
    j = lane % ROPE_PAIR
    inv = ROPE_THETA ** (-(j.astype(np.float32)) / np.float32(ROPE_PAIR))
    pos = np.where((lane % DA_QK) < ROPE_HALF, row[:, None], col[:, None]).astype(np.float32)
    ang = (pos * inv.astype(np.float32)[None, :]).astype(np.float32)
    sign = np.where((lane % ROPE_HALF) < ROPE_PAIR, -1.0, 1.0).astype(np.float32)
    cos_t = np.concatenate([np.cos(ang), np.ones((TM, LANES), np.float32)], axis=0)
    sin_t = np.concatenate([np.sin(ang) * sign[None, :], np.zeros((TM, LANES), np.float32)], axis=0)
    return cos_t.astype(np.float32), sin_t.astype(np.float32)


def kernel(x, c, ctx, c_ctx, w_ada, b_ada, norm1_w, w_in, da_lambda, da_subln_w, ssd_conv_w, ssd_conv_b,
           ssd_a_log, ssd_dt_bias, ssd_d, ssd_norm_w, gla_gk_up, gla_gk_b, gla_norm_w, w_out, norm2_w,
           w_router, b_router, w_gate_up, b_gate_up, w_down, b_down, final_norm_w):
    B, S, D = x.shape
    Lc = ctx.shape[1]
    depth = w_ada.shape[0]
    assert S % TMR == 0 and Lc % TM == 0 and (B * Lc) % TMR == 0 and S % GRID_W == 0
    lay = dict(B=B, S=S, Lc=Lc, nlb=(B * S) // TM, bpb=S // TM, cpb=Lc // TM)
    n_lat = B * S

    xs = jnp.concatenate([x.reshape(B * S, D), ctx.reshape(B * Lc, D)], axis=0)

    cc = jnp.zeros((8, D), F32).at[0:B].set(c).at[B].set(c_ctx)
    mod_all = _ada(cc, w_ada, b_ada)

    cos_t, sin_t = (jnp.asarray(t) for t in _rope_tables(S))

    for l in range(depth):
        last = l == depth - 1
        lam_init = 0.8 - 0.6 * math.exp(-0.3 * l)
        mod = mod_all[l, 0:B + 1].reshape(B + 1, 1, 6 * D)

        wi = w_in[l]
        misc_w = jnp.zeros((D, LANES), F32)
        misc_w = misc_w.at[:, MISC_DT:MISC_DT + N_DT].set(wi[:, O_DT:O_GLA])
        misc_w = misc_w.at[:, MISC_CODE:MISC_CODE + N_CODE].set(wi[:, O_CODE:O_CODE + N_CODE])
        w_re = jnp.concatenate([wi[:, 0:O_DT], wi[:, O_GLA:O_CODE], misc_w], axis=1).astype(BF16)
        a128 = jnp.zeros((1, LANES), F32).at[0, MISC_DT:MISC_DT + N_DT].set(
            -jnp.exp(ssd_a_log[l].astype(F32)).reshape(-1))
        bias128 = jnp.zeros((1, LANES), F32).at[0, MISC_DT:MISC_DT + N_DT].set(
            ssd_dt_bias[l].astype(F32).reshape(-1))
        wgk = jnp.zeros((2, LANES, GLA_KEY_WIDTH), F32)
        for d in range(2):
            wgk = wgk.at[d, MISC_CODE + d * GLA_RANK:MISC_CODE + (d + 1) * GLA_RANK, :].set(gla_gk_up[l, d])
        bgk = gla_gk_b[l].reshape(2, 1, GLA_KEY_WIDTH)
        dsk = jnp.repeat(ssd_d[l], SSD_HEAD_DIM).reshape(1, SSD_WIDTH)
        snw = ssd_norm_w[l].reshape(1, SSD_WIDTH)
        gnw = jnp.tile(gla_norm_w[l], GLA_HEADS).reshape(1, GLA_WIDTH)
        wr = jnp.zeros((D, LANES), F32).at[:, 0:N_EXPERTS].set(w_router[l])
        br = jnp.zeros((1, LANES), F32).at[0, 0:N_EXPERTS].set(b_router[l])

        q, k, v, z, xbc_act, gla_in, misc = _inproj(xs, norm1_w[l].reshape(1, D), mod, cos_t, sin_t, w_re,
                                                    ssd_conv_w[l], ssd_conv_b[l].reshape(1, SSD_CONV_CH), lay)

        a = _attention(q, k, v, da_lambda[l], da_subln_w[l].reshape(1, DA_V), lay, lam_init,
                       ctx_queries=not last)
        yf, yb, of, ob = _scans(xbc_act, gla_in, misc, a128, bias128, wgk, bgk, lay)

        nrows = n_lat if last else xs.shape[0]
        xs = _outproj(xs, a, yf, yb, xbc_act, z, of, ob, gla_in, dsk, snw, gnw,
                      w_out[l].astype(BF16), mod, lay, nrows)
        xs = _moe(xs, norm2_w[l].reshape(1, D), mod, wr, br, w_gate_up, b_gate_up, w_down, b_down,
                  final_norm_w.reshape(1, D), lay, l, last)

    return xs.reshape(B, S, D)
```

```python
import functools
import math

import numpy as np
import jax
import jax.numpy as jnp
from jax import lax
from jax.experimental import pallas as pl
from jax.experimental.pallas import tpu as pltpu

F32 = jnp.float32
BF16 = jnp.bfloat16
I32 = jnp.int32

GRID_W = 64
EPS = 1e-6
DA_HEADS = 4
DA_QK = 64
DA_V = 128
DA_WIDTH = 512
ROPE_THETA = 10000.0
SSD_HEADS = 4
SSD_HEAD_DIM = 64
SSD_WIDTH = 256
SSD_STATE = 64
SSD_CONV_CH = 512
GLA_HEADS = 4
GLA_DK = 32
GLA_DV = 64
GLA_KEY_WIDTH = 128
GLA_WIDTH = 256
GLA_RANK = 16
GLA_NORMALIZER = 16.0
N_EXPERTS = 32
TOP_K = 4
D_EXPERT = 1024
SWIGLU_LIMIT = 7.0
SWIGLU_ALPHA = 1.702

LANES = 128
TM = 256
SSD_Q = 256
GLA_Q = 64
GLA_SUB = 4
ATT_TQ = 256
ATT_KT = (2816, 768, 512, 256)
LOG2E = 1.4426950408889634
MOE_BLK = 512
TMR = 512
SEG_ALIGN = 16
STG_ROWS = TMR * TOP_K + N_EXPERTS * SEG_ALIGN
INFO_LANES = 256
INFO_TAIL = 128
INFO_NUSED = 224
VMEM_LIMIT = 56 * 1024 * 1024

ROPE_HALF = DA_QK // 2
ROPE_PAIR = ROPE_HALF // 2
GLA_QKV = 2 * GLA_KEY_WIDTH + GLA_WIDTH
GLA_SLAB = GLA_QKV + GLA_WIDTH
N_DT = 2 * SSD_HEADS
N_CODE = 2 * GLA_RANK
ADA_TN = 512

O_DT = 3 * DA_WIDTH + SSD_WIDTH + SSD_CONV_CH
O_GLA = O_DT + N_DT
O_CODE = O_GLA + GLA_SLAB
C_Q = 0
C_K = C_Q + DA_WIDTH
C_V = C_K + DA_WIDTH
C_Z = C_V + DA_WIDTH
C_XBC = C_Z + SSD_WIDTH
C_GLA = C_XBC + SSD_CONV_CH
C_MISC = C_GLA + GLA_SLAB
C_END = C_MISC + LANES
MISC_DT = 0
MISC_CODE = N_DT


def _sigmoid(x):
    return 1.0 / (1.0 + jnp.exp(-x))


def _softplus(x):
    return jnp.maximum(x, 0.0) + jnp.log(1.0 + jnp.exp(-jnp.abs(x)))


def _split2(a):
    hi = a.astype(BF16)
    lo = (a - hi.astype(F32)).astype(BF16)
    return hi, lo


def _split3(a):
    a1 = a.astype(BF16)
    r1 = a - a1.astype(F32)
    a2 = r1.astype(BF16)
    a3 = (r1 - a2.astype(F32)).astype(BF16)
    return a1, a2, a3


def _dot(a, b):
    return jnp.dot(a, b, preferred_element_type=F32)


def _dot_hi(a, b):
    a1, a2 = _split2(a)
    b1, b2 = _split2(b)
    return _dot(a1, b1) + (_dot(a1, b2) + _dot(a2, b1))


def _dot_exact_lhs(m, a):
    a1, a2, a3 = _split3(a)
    mb = m.astype(BF16)
    return _dot(mb, a1) + (_dot(mb, a2) + _dot(mb, a3))


def _dot_exact_rhs(a, m):
    a1, a2, a3 = _split3(a)
    mb = m.astype(BF16)
    return _dot(a1, mb) + (_dot(a2, mb) + _dot(a3, mb))


def _params(*sem):
    return pltpu.CompilerParams(dimension_semantics=sem, vmem_limit_bytes=VMEM_LIMIT)


def _ada_kernel(c_ref, w_ref, b_ref, o_ref):
    c = c_ref[...]
    s = c * _sigmoid(c)
    o_ref[0] = _dot_hi(s, w_ref[0]) + b_ref[0]


def _ada(cc, w_ada, b_ada):
    L, D, N = w_ada.shape
    tn = ADA_TN
    return pl.pallas_call(
        _ada_kernel,
        grid=(L, N // tn),
        in_specs=[pl.BlockSpec((8, D), lambda l, j: (0, 0)),
                  pl.BlockSpec((1, D, tn), lambda l, j: (l, 0, j)),
                  pl.BlockSpec((1, 1, tn), lambda l, j: (l, 0, j))],
        out_specs=pl.BlockSpec((1, 8, tn), lambda l, j: (l, 0, j)),
        out_shape=jax.ShapeDtypeStruct((L, 8, N), F32),
        compiler_params=_params("parallel", "parallel"),
        name="ada",
    )(cc, w_ada, b_ada.reshape(L, 1, N))


def _inproj_kernel(x_ref, xp_ref, xn_ref, nw_ref, sh_ref, sc_ref, cos_ref, sin_ref, w_ref, cw_ref, cb_ref,
                   q_ref, k_ref, v_ref, z_ref, xbc_ref, gla_ref, misc_ref, *, nlb, bpb, cpb):
    def modulated(xf):
        ms = jnp.mean(xf * xf, axis=-1, keepdims=True)
        return ((xf * lax.rsqrt(ms + EPS) * nw_ref[...]) * (1.0 + sc_ref[0]) + sh_ref[0]).astype(BF16)

    hb = modulated(x_ref[...])
    cos = cos_ref[...]
    sin = sin_ref[...]
    lane = lax.broadcasted_iota(I32, cos.shape, 1)
    first = (lane % ROPE_HALF) < ROPE_PAIR

    def rope(p):
        outs = []
        for hd in range(DA_HEADS):
            ph = p[:, hd * LANES:(hd + 1) * LANES]
            sw = jnp.where(first, pltpu.roll(ph, LANES - ROPE_PAIR, 1), pltpu.roll(ph, ROPE_PAIR, 1))
            outs.append(ph * cos + sw * sin)
        return jnp.concatenate(outs, axis=1)

    q = rope(_dot(hb, w_ref[:, C_Q:C_K])) * (DA_QK ** -0.5 * LOG2E)
    q_ref[...] = q.astype(BF16)
    k_ref[...] = rope(_dot(hb, w_ref[:, C_K:C_V])).astype(BF16)
    v_ref[...] = _dot(hb, w_ref[:, C_V:C_Z]).astype(BF16)
    z_ref[...] = _dot(hb, w_ref[:, C_Z:C_XBC])
    gla_ref[...] = _dot(hb, w_ref[:, C_GLA:C_MISC])
    misc_ref[...] = _dot(hb, w_ref[:, C_MISC:C_END])

    xbc = _dot(hb, w_ref[:, C_XBC:C_GLA])
    halo = _dot(modulated(jnp.concatenate([xp_ref[...], xn_ref[...]], axis=0)), w_ref[:, C_XBC:C_GLA])
    i = pl.program_id(0)
    pos = jnp.where(i < nlb, i % bpb, (i - nlb) % cpb)
    last = jnp.where(i < nlb, bpb - 1, cpb - 1)
    prev_row = halo[7:8, :] * (pos != 0).astype(F32)
    next_row = halo[8:9, :] * (pos != last).astype(F32)
    ridx = lax.broadcasted_iota(I32, xbc.shape, 0)
    xm1 = jnp.where(ridx == 0, prev_row, pltpu.roll(xbc, 1, 0))
    xp1 = jnp.where(ridx == xbc.shape[0] - 1, next_row, pltpu.roll(xbc, xbc.shape[0] - 1, 0))
    cw = cw_ref[...]
    y = cw[0:1] * xm1 + cw[1:2] * xbc + cw[2:3] * xp1 + cb_ref[...]
    xbc_ref[...] = y * _sigmoid(y)


def _inproj(x, nw, mod, cos_t, sin_t, w, cw, cb, lay):
    T, D = x.shape
    nlb, bpb, B = lay["nlb"], lay["bpb"], lay["B"]
    r8 = TM // 8
    mrow = lambda i: jnp.where(i < nlb, i // bpb, B)
    prow = lambda i: jnp.where(i < nlb, i % bpb, bpb)
    row = lambda n: pl.BlockSpec((TM, n), lambda i: (i, 0))
    spb = bpb + lay["cpb"]
    kvrow = lambda i: (jnp.where(i < nlb, (i // bpb) * spb + lay["cpb"] + i % bpb,
                                 ((i - nlb) // lay["cpb"]) * spb + (i - nlb) % lay["cpb"]), 0)
    kv = pl.BlockSpec((TM, DA_WIDTH), kvrow)
    outs = [(DA_WIDTH, BF16), (DA_WIDTH, BF16), (DA_WIDTH, BF16), (SSD_WIDTH, F32), (SSD_CONV_CH, F32),
            (GLA_SLAB, F32), (LANES, F32)]
    out_specs = [row(n) for n, _ in outs]
    out_specs[1] = kv
    out_specs[2] = kv
    return pl.pallas_call(
        functools.partial(_inproj_kernel, nlb=nlb, bpb=bpb, cpb=lay["cpb"]),
        grid=(T // TM,),
        in_specs=[row(D),
                  pl.BlockSpec((8, D), lambda i: (jnp.maximum(i * r8 - 1, 0), 0)),
                  pl.BlockSpec((8, D), lambda i: (jnp.minimum((i + 1) * r8, T // 8 - 1), 0)),
                  pl.BlockSpec((1, D), lambda i: (0, 0)),
                  pl.BlockSpec((1, 1, D), lambda i: (mrow(i), 0, 0)),
                  pl.BlockSpec((1, 1, D), lambda i: (mrow(i), 0, 1)),
                  pl.BlockSpec((TM, LANES), lambda i: (prow(i), 0)),
                  pl.BlockSpec((TM, LANES), lambda i: (prow(i), 0)),
                  pl.BlockSpec((D, C_END), lambda i: (0, 0)),
                  pl.BlockSpec((3, SSD_CONV_CH), lambda i: (0, 0)),
                  pl.BlockSpec((1, SSD_CONV_CH), lambda i: (0, 0))],
        out_specs=out_specs,
        out_shape=[jax.ShapeDtypeStruct((T, n), dt) for n, dt in outs],
        compiler_params=_params("parallel"),
        name="inproj",
    )(x, x, x, nw, mod, mod, cos_t, sin_t, w, cw, cb)


def _lane_fold(x, op):
    f = x[:, 0:LANES]
    for i in range(1, x.shape[1] // LANES):
        f = op(f, x[:, i * LANES:(i + 1) * LANES])
    return f


def _attn_stack_maps(q_ref, q2_sc):
    tq = q_ref.shape[0]
    q = q_ref[...]
    lane = lax.broadcasted_iota(I32, q.shape, 1)
    zero = jnp.zeros_like(q)
    q2_sc[0:tq, :] = jnp.where(lane < DA_QK, q, zero)
    q2_sc[tq:2 * tq, :] = jnp.where(lane >= DA_QK, q, zero)


def _attn_scores_tile(q2_sc, k_ref, s_sc, slot, mf_sc, j, kt):
    kj = k_ref[pl.ds(pl.multiple_of(j * kt, kt), kt), :]
    s = lax.dot_general(q2_sc[...], kj, (((1,), (1,)), ((), ())), preferred_element_type=F32)
    s_sc[slot, j, :, 0:kt] = s
    mf_sc[...] = jnp.maximum(mf_sc[...], _lane_fold(s, jnp.maximum))


def _attn_weigh_tile(v_ref, s_sc, slot, p_sc, l_sc, acc_sc, m, j, kt):
    sub = next(w for w in (512, 256) if kt % w == 0)
    for c0 in range(0, kt, sub):
        p = jnp.exp2(s_sc[slot, j, :, c0:c0 + sub] - m)
        l_sc[...] += _lane_fold(p, jnp.add)
        p_sc[:, c0:c0 + sub] = p.astype(BF16)
    vj = v_ref[pl.ds(pl.multiple_of(j * kt, kt), kt), :]
    acc_sc[...] += _dot(p_sc[:, 0:kt], vj)


def _attn_finish(lam_ref, w_ref, o_ref, l_sc, acc_sc, lam_init):
    tq = o_ref.shape[0]
    lp = lam_ref[...]
    lam = (jnp.exp(jnp.sum(lp[0:1] * lp[1:2], axis=1, keepdims=True))
           - jnp.exp(jnp.sum(lp[2:3] * lp[3:4], axis=1, keepdims=True)) + lam_init)
    acc = acc_sc[...]
    l = jnp.sum(l_sc[...], axis=1, keepdims=True)
    o = acc[0:tq] / l[0:tq] - lam * (acc[tq:2 * tq] / l[tq:2 * tq])
    ms = jnp.mean(o * o, axis=-1, keepdims=True)
    o_ref[...] = (o * lax.rsqrt(ms + EPS) * w_ref[...]) * (1.0 - lam_init)


def _attn_kernel(q_ref, k_ref, v_ref, lam_ref, w_ref, o_ref, q2_sc, s_sc, p_sc, mf_sc, l_sc, acc_sc,
                 *, nt, kt, lam_init):
    _attn_stack_maps(q_ref, q2_sc)
    mf_sc[...] = jnp.full(mf_sc.shape, -jnp.inf, F32)
    l_sc[...] = jnp.zeros(l_sc.shape, F32)
    acc_sc[...] = jnp.zeros(acc_sc.shape, F32)

    def scores(j, carry):
        _attn_scores_tile(q2_sc, k_ref, s_sc, 0, mf_sc, j, kt)
        return carry

    lax.fori_loop(0, nt, scores, 0)
    m = jnp.max(mf_sc[...], axis=1, keepdims=True)

    def weigh(j, carry):
        _attn_weigh_tile(v_ref, s_sc, 0, p_sc, l_sc, acc_sc, m, j, kt)
        return carry

    lax.fori_loop(0, nt, weigh, 0)
    _attn_finish(lam_ref, w_ref, o_ref, l_sc, acc_sc, lam_init)


def _attn_pipe_kernel(q_ref, k_ref, v_ref, lam_ref, w_ref, o_ref, q2_sc, s_sc, p_sc, mf_sc, m_sc, l_sc, acc_sc,
                      *, nq, nt, kt, lam_init):
    t = pl.program_id(2)

    @pl.when(t < nq)
    def _():
        _attn_stack_maps(q_ref, q2_sc)
        mf_sc[...] = jnp.full(mf_sc.shape, -jnp.inf, F32)

    @pl.when(t > 0)
    def _():
        l_sc[...] = jnp.zeros(l_sc.shape, F32)
        acc_sc[...] = jnp.zeros(acc_sc.shape, F32)

    def run(slot_a, slot_b):
        m_prev = None if slot_b is None else m_sc[slot_b]

        def body(j, carry):
            if slot_b is not None:
                _attn_weigh_tile(v_ref, s_sc, slot_b, p_sc, l_sc, acc_sc, m_prev, j, kt)
            if slot_a is not None:
                _attn_scores_tile(q2_sc, k_ref, s_sc, slot_a, mf_sc, j, kt)
            return carry

        lax.fori_loop(0, nt, body, 0)
        if slot_a is not None:
            m_sc[slot_a] = jnp.max(mf_sc[...], axis=1, keepdims=True)

    inner = (t > 0) & (t < nq)
    pl.when(t == 0)(lambda: run(0, None))
    pl.when(inner & (t % 2 == 1))(lambda: run(1, 0))
    pl.when(inner & (t % 2 == 0))(lambda: run(0, 1))
    pl.when(t == nq)(lambda: run(None, (nq - 1) % 2))

    @pl.when(t > 0)
    def _():
        _attn_finish(lam_ref, w_ref, o_ref, l_sc, acc_sc, lam_init)


def _attention(q, k, v, lam_p, subln_w, lay, lam_init, *, ctx_queries):
    B, S, Lc = lay["B"], lay["S"], lay["Lc"]
    nkeys = Lc + S
    assert nkeys % Lc == 0
    small = [pl.BlockSpec((4, DA_QK), lambda b, h, t: (0, 0)), pl.BlockSpec((1, DA_V), lambda b, h, t: (0, 0))]
    stacked = lambda n, dt: pltpu.VMEM((2 * ATT_TQ, n), dt)

    kt = next(t for t in ATT_KT if nkeys % t == 0)
    nq = S // ATT_TQ
    kmap = lambda b, h, t: (b, h)
    a_lat = pl.pallas_call(
        functools.partial(_attn_pipe_kernel, nq=nq, nt=nkeys // kt, kt=kt, lam_init=lam_init),
        grid=(B, DA_HEADS, nq + 1),
        in_specs=[pl.BlockSpec((ATT_TQ, LANES), lambda b, h, t: (b * nq + jnp.minimum(t, nq - 1), h)),
                  pl.BlockSpec((nkeys, LANES), kmap),
                  pl.BlockSpec((nkeys, LANES), kmap)] + small,
        out_specs=pl.BlockSpec((ATT_TQ, LANES), lambda b, h, t: (b * nq + jnp.maximum(t - 1, 0), h)),
        out_shape=jax.ShapeDtypeStruct((B * S, DA_WIDTH), F32),
        scratch_shapes=[stacked(LANES, BF16),
                        pltpu.VMEM((2, nkeys // kt, 2 * ATT_TQ, kt), F32),
                        stacked(kt, BF16), stacked(LANES, F32),
                        pltpu.VMEM((2, 2 * ATT_TQ, 1), F32),
                        stacked(LANES, F32), stacked(LANES, F32)],
        compiler_params=_params("parallel", "parallel", "arbitrary"),
        name="attn",
    )(q, k, v, lam_p, subln_w)
    if not ctx_queries:
        return a_lat

    ktc = next(t for t in ATT_KT if Lc % t == 0)
    nqc = Lc // ATT_TQ
    cmap = lambda b, h, t: (b * (nkeys // Lc), h)
    a_ctx = pl.pallas_call(
        functools.partial(_attn_kernel, nt=Lc // ktc, kt=ktc, lam_init=lam_init),
        grid=(B, DA_HEADS, nqc),
        in_specs=[pl.BlockSpec((ATT_TQ, LANES), lambda b, h, t: ((B * S) // ATT_TQ + b * nqc + t, h)),
                  pl.BlockSpec((Lc, LANES), cmap),
                  pl.BlockSpec((Lc, LANES), cmap)] + small,
        out_specs=pl.BlockSpec((ATT_TQ, LANES), lambda b, h, t: (b * nqc + t, h)),
        out_shape=jax.ShapeDtypeStruct((B * Lc, DA_WIDTH), F32),
        scratch_shapes=[stacked(LANES, BF16),
                        pltpu.VMEM((1, Lc // ktc, 2 * ATT_TQ, ktc), F32),
                        stacked(ktc, BF16), stacked(LANES, F32), stacked(LANES, F32), stacked(LANES, F32)],
        compiler_params=_params("parallel", "parallel", "arbitrary"),
        name="attn_ctx",
    )(q, k, v, lam_p, subln_w)
    return jnp.concatenate([a_lat, a_ctx], axis=0)


def _ssd_dir(xbc, misc, a128, bias128, hs_ref, dirn, rev):
    Q = xbc.shape[0]
    dt = _softplus(misc + bias128)
    a = dt * a128
    ii = lax.broadcasted_iota(I32, (Q, Q), 0)
    jj = lax.broadcasted_iota(I32, (Q, Q), 1)
    M = (jj >= ii) if rev else (jj <= ii)
    acs = _dot_exact_lhs(M.astype(F32), a)
    acs_t = acs.T
    dt_t = dt.T
    xs = xbc[:, 0:SSD_WIDTH]
    Bt = xbc[:, SSD_WIDTH:SSD_WIDTH + 128].T
    Cm = xbc[:, SSD_WIDTH + 128:SSD_WIDTH + 256]
    last = 0 if rev else Q - 1
    ys = []
    for g in range(2):
        Cg = Cm[:, g * SSD_STATE:(g + 1) * SSD_STATE]
        Btg = Bt[g * SSD_STATE:(g + 1) * SSD_STATE, :]
        CB = _dot(Cg, Btg)
        for hh in range(2):
            h = 2 * g + hh
            li = dirn * SSD_HEADS + h
            acs_c = acs[:, li:li + 1]
            acs_r = acs_t[li:li + 1, :]
            dt_r = dt_t[li:li + 1, :]
            decay = jnp.exp(jnp.where(M, acs_c - acs_r, -jnp.inf))
            sc = CB * decay * dt_r
            x_h = xs[:, h * SSD_HEAD_DIM:(h + 1) * SSD_HEAD_DIM]
            hs = hs_ref[h]
            y = _dot(sc, x_h) + _dot(Cg * jnp.exp(acs_c), hs)
            tot = acs_r[:, last:last + 1]
            w_end = jnp.exp(tot - acs_r) * dt_r
            hs_ref[h] = jnp.exp(tot) * hs + _dot(Btg * w_end, x_h)
            ys.append(y)
    return jnp.concatenate(ys, axis=1)


def _scan_maps(lay, rows):
    ncc, ncl = lay["Lc"] // rows, lay["S"] // rows
    base = (lay["B"] * lay["S"]) // rows
    fmap = lambda b, c: (jnp.where(c < ncc, base + b * ncc + c, b * ncl + c - ncc), 0)
    bmap = lambda b, c: (jnp.where(c < ncc, base + b * ncc + (ncc - 1 - c),
                                   b * ncl + (ncl - 1 - (c - ncc))), 0)
    return ncc + ncl, fmap, bmap


def _gla_block(blk, misc, wgk, bgk, s_ref, rev):
    R = blk.shape[0]
    kw = GLA_KEY_WIDTH
    q, k, v = blk[:, 0:kw], blk[:, kw:2 * kw], blk[:, 2 * kw:2 * kw + GLA_WIDTH]
    pre = _dot_hi(misc, wgk) + bgk
    gk = (jnp.minimum(pre, 0.0) - jnp.log(1.0 + jnp.exp(-jnp.abs(pre)))) / GLA_NORMALIZER
    ii = lax.broadcasted_iota(I32, (R, R), 0)
    jj = lax.broadcasted_iota(I32, (R, R), 1)
    same = (ii // GLA_Q) == (jj // GLA_Q)
    M = same & ((jj >= ii) if rev else (jj <= ii))
    b = _dot_exact_lhs(M.astype(F32), gk)
    tot = _dot_exact_lhs(same.astype(F32), gk)
    q_t = (q * (GLA_DK ** -0.5)) * jnp.exp(b)
    k_t = k * jnp.exp(-b)
    k_end_t = (k * jnp.exp(tot - b)).T
    gk_t = gk.T
    lane_k = lax.broadcasted_iota(I32, (1, kw), 1) // GLA_DK
    lane_v = lax.broadcasted_iota(I32, (1, GLA_WIDTH), 1) // GLA_DV
    o = jnp.zeros((R, GLA_WIDTH), F32)
    for h in range(GLA_HEADS):
        qh = jnp.where(lane_k == h, q_t, 0.0)
        att = lax.dot_general(qh, k_t, (((1,), (1,)), ((), ())), preferred_element_type=F32)
        o = o + _dot(jnp.where(M, att, 0.0), jnp.where(lane_v == h, v, 0.0))
    col_chunk = lax.broadcasted_iota(I32, (1, R), 1) // GLA_Q
    S = s_ref[...]
    bd = (lax.broadcasted_iota(I32, S.shape, 0) // GLA_DK) == (lax.broadcasted_iota(I32, S.shape, 1) // GLA_DV)
    nsub = R // GLA_Q
    inter = [None] * nsub
    for c in (range(nsub - 1, -1, -1) if rev else range(nsub)):
        sel = col_chunk == c
        inter[c] = _dot(q_t[c * GLA_Q:(c + 1) * GLA_Q], S)
        decay = jnp.exp(jnp.sum(jnp.where(sel, gk_t, 0.0), axis=1, keepdims=True))
        S = jnp.where(bd, S * decay + _dot(jnp.where(sel, k_end_t, 0.0), v), 0.0)
    s_ref[...] = S
    return o + jnp.concatenate(inter, axis=0)


def _scan_kernel(xf_ref, gf_ref, mf_ref, xb_ref, gb_ref, mb_ref, a_ref, bias_ref, wgk_ref, bgk_ref,
                 yf_ref, yb_ref, of_ref, ob_ref, hf_sc, hb_sc, sf_sc, sb_sc):
    @pl.when(pl.program_id(1) == 0)
    def _():
        for sc in (hf_sc, hb_sc, sf_sc, sb_sc):
            sc[...] = jnp.zeros(sc.shape, F32)

    a128 = a_ref[...]
    bias128 = bias_ref[...]
    mf = mf_ref[...]
    mb = mb_ref[...]
    nsub = xf_ref.shape[0] // SSD_Q
    for c in range(nsub):
        r0 = c * SSD_Q
        yf_ref[r0:r0 + SSD_Q, :] = _ssd_dir(xf_ref[r0:r0 + SSD_Q, :], mf[r0:r0 + SSD_Q], a128, bias128,
                                            hf_sc, 0, False)
        r1 = (nsub - 1 - c) * SSD_Q
        yb_ref[r1:r1 + SSD_Q, :] = _ssd_dir(xb_ref[r1:r1 + SSD_Q, :], mb[r1:r1 + SSD_Q], a128, bias128,
                                            hb_sc, 1, True)
    of_ref[...] = _gla_block(gf_ref[...], mf, wgk_ref[0], bgk_ref[0], sf_sc, False)
    ob_ref[...] = _gla_block(gb_ref[...], mb, wgk_ref[1], bgk_ref[1], sb_sc, True)


def _scans(xbc_act, gla_in, misc, a128, bias128, wgk, bgk, lay):
    T = xbc_act.shape[0]
    R = GLA_Q * GLA_SUB
    assert R % SSD_Q == 0
    nc, fmap, bmap = _scan_maps(lay, R)
    vec = pl.BlockSpec((1, LANES), lambda b, c: (0, 0))
    ins = lambda m: [pl.BlockSpec((R, SSD_CONV_CH), m), pl.BlockSpec((R, GLA_QKV), m), pl.BlockSpec((R, LANES), m)]
    outs = lambda n: [pl.BlockSpec((R, n), fmap), pl.BlockSpec((R, n), bmap)]
    return pl.pallas_call(
        _scan_kernel,
        grid=(lay["B"], nc),
        in_specs=ins(fmap) + ins(bmap) + [
            vec, vec,
            pl.BlockSpec((2, LANES, GLA_KEY_WIDTH), lambda b, c: (0, 0, 0)),
            pl.BlockSpec((2, 1, GLA_KEY_WIDTH), lambda b, c: (0, 0, 0))],
        out_specs=outs(SSD_WIDTH) + outs(GLA_WIDTH),
        out_shape=[jax.ShapeDtypeStruct((T, SSD_WIDTH), F32)] * 2 + [jax.ShapeDtypeStruct((T, GLA_WIDTH), F32)] * 2,
        scratch_shapes=[pltpu.VMEM((SSD_HEADS, SSD_STATE, SSD_HEAD_DIM), F32)] * 2
        + [pltpu.VMEM((GLA_KEY_WIDTH, GLA_WIDTH), F32)] * 2,
        compiler_params=_params("parallel", "arbitrary"),
        name="scans",
    )(xbc_act, gla_in, misc, xbc_act, gla_in, misc, a128, bias128, wgk, bgk)


def _outproj_kernel(x_ref, a_ref, yf_ref, yb_ref, xs_ref, z_ref, of_ref, ob_ref, g_ref,
                    dsk_ref, snw_ref, gnw_ref, w_ref, g1_ref, o_ref):
    y = yf_ref[...] + yb_ref[...] + dsk_ref[...] * xs_ref[...]
    z = z_ref[...]
    gs = y * (z * _sigmoid(z))
    half = SSD_WIDTH // 2
    parts = []
    for grp in range(2):
        seg = gs[:, grp * half:(grp + 1) * half]
        parts.append(seg * lax.rsqrt(jnp.mean(seg * seg, axis=-1, keepdims=True) + EPS))
    s = jnp.concatenate(parts, axis=1) * snw_ref[...]
    o = of_ref[...] + ob_ref[...]
    bd = ((lax.broadcasted_iota(I32, (GLA_WIDTH, GLA_WIDTH), 0) // GLA_DV)
          == (lax.broadcasted_iota(I32, (GLA_WIDTH, GLA_WIDTH), 1) // GLA_DV))
    ms = _dot_exact_rhs(o * o, jnp.where(bd, 1.0 / GLA_DV, 0.0))
    g = g_ref[...]
    c = (o * lax.rsqrt(ms + EPS) * gnw_ref[...]) * (g * _sigmoid(g))
    mix = jnp.concatenate([a_ref[...], s, c], axis=1).astype(BF16)
    o_ref[...] = x_ref[...] + g1_ref[0] * _dot(mix, w_ref[...])


def _outproj(x, a, yf, yb, xbc_act, z, of, ob, gla_in, dsk, snw, gnw, w_out, mod, lay, nrows):
    D = x.shape[1]
    nlb, bpb, B = lay["nlb"], lay["bpb"], lay["B"]
    mrow = lambda i: jnp.where(i < nlb, i // bpb, B)
    row = lambda n, cb=0: pl.BlockSpec((TM, n), lambda i: (i, cb))
    vec = lambda n: pl.BlockSpec((1, n), lambda i: (0, 0))
    return pl.pallas_call(
        _outproj_kernel,
        grid=(nrows // TM,),
        in_specs=[row(D), row(DA_WIDTH), row(SSD_WIDTH), row(SSD_WIDTH), row(SSD_WIDTH), row(SSD_WIDTH),
                  row(GLA_WIDTH), row(GLA_WIDTH), row(GLA_WIDTH, 2),
                  vec(SSD_WIDTH), vec(SSD_WIDTH), vec(GLA_WIDTH),
                  pl.BlockSpec((D, D), lambda i: (0, 0)),
                  pl.BlockSpec((1, 1, D), lambda i: (mrow(i), 0, 2))],
        out_specs=row(D),
        out_shape=jax.ShapeDtypeStruct((nrows, D), F32),
        compiler_params=_params("parallel"),
        name="outproj",
    )(x, a, yf, yb, xbc_act, z, of, ob, gla_in, dsk, snw, gnw, w_out, mod)


def _route_kernel(x_ref, nw_ref, sh_ref, sc_ref, wr_ref, br_ref,
                  h_ref, mi_ref, mf_ref, blk_ref, tot_ref, carry_sc):
    i = pl.program_id(0)

    @pl.when(i == 0)
    def _():
        carry_sc[...] = jnp.zeros(carry_sc.shape, F32)

    xf = x_ref[...]
    ms = jnp.mean(xf * xf, axis=-1, keepdims=True)
    h = (xf * lax.rsqrt(ms + EPS) * nw_ref[...]) * (1.0 + sc_ref[0]) + sh_ref[0]
    h_ref[...] = h
    tm = h.shape[0]
    lane = lax.broadcasted_iota(I32, (tm, LANES), 1)
    logits = jnp.where(lane < N_EXPERTS, _dot_hi(h, wr_ref[...]) + br_ref[...], -jnp.inf)
    vals, idxs, hots = [], [], []
    l = logits
    for _ in range(TOP_K):
        m = jnp.max(l, axis=1, keepdims=True)
        idx = jnp.min(jnp.where(l == m, lane, LANES), axis=1, keepdims=True)
        hot = lane == idx
        vals.append(m)
        idxs.append(idx)
        hots.append(hot)
        l = jnp.where(hot, -jnp.inf, l)
    es = [jnp.exp(v - vals[0]) for v in vals]
    den = es[0] + es[1] + es[2] + es[3]
    hot_all = jnp.zeros((tm, LANES), F32)
    for hot in hots:
        hot_all = hot_all + hot.astype(F32)
    ii = lax.broadcasted_iota(I32, (tm, tm), 0)
    jj = lax.broadcasted_iota(I32, (tm, tm), 1)
    before = _dot((jj < ii).astype(BF16), hot_all.astype(BF16))
    cnt = jnp.sum(hot_all, axis=0, keepdims=True)
    cntp = jnp.floor((cnt + (SEG_ALIGN - 1.0)) * (1.0 / SEG_ALIGN)) * SEG_ALIGN
    ei = lax.broadcasted_iota(I32, (LANES, LANES), 0)
    ej = lax.broadcasted_iota(I32, (LANES, LANES), 1)
    units = jnp.broadcast_to(cntp * (1.0 / SEG_ALIGN), (8, LANES)).astype(BF16)
    seg = _dot(units, (ei < ej).astype(BF16))[0:1] * SEG_ALIGN
    pos_e = seg + before
    mi = jnp.zeros((tm, LANES), I32)
    mf = jnp.zeros((tm, LANES), F32)
    for kk in range(TOP_K):
        spos = jnp.sum(jnp.where(hots[kk], pos_e, 0.0), axis=1, keepdims=True).astype(I32)
        mi = jnp.where(lane == kk, idxs[kk], mi)
        mi = jnp.where(lane == TOP_K + kk, spos, mi)
        mf = jnp.where(lane == kk, es[kk] / den, mf)
    mi_ref[...] = mi
    mf_ref[...] = mf
    rowi = lax.broadcasted_iota(I32, (8, LANES), 0)
    info = jnp.where(rowi == 0, cntp, jnp.where(rowi == 1, seg, jnp.where(rowi == 2, carry_sc[...], 0.0)))
    blk_ref[0] = info.astype(I32)
    carry_sc[...] = carry_sc[...] + cntp
    tot_ref[...] = carry_sc[...]


def _route(x, nw, mod, wr, br, lay):
    T, D = x.shape
    B, S = lay["B"], lay["S"]
    mrow = lambda i: jnp.where(i < (B * S) // TMR, i // (S // TMR), B)
    row = lambda n: pl.BlockSpec((TMR, n), lambda i: (i, 0))
    return pl.pallas_call(
        _route_kernel,
        grid=(T // TMR,),
        in_specs=[row(D),
                  pl.BlockSpec((1, D), lambda i: (0, 0)),
                  pl.BlockSpec((1, 1, D), lambda i: (mrow(i), 0, 3)),
                  pl.BlockSpec((1, 1, D), lambda i: (mrow(i), 0, 4)),
                  pl.BlockSpec((D, LANES), lambda i: (0, 0)),
                  pl.BlockSpec((1, LANES), lambda i: (0, 0))],
        out_specs=[row(D), row(LANES), row(LANES),
                   pl.BlockSpec((1, 8, LANES), lambda i: (i, 0, 0)),
                   pl.BlockSpec((1, LANES), lambda i: (0, 0))],
        out_shape=[jax.ShapeDtypeStruct((T, D), F32), jax.ShapeDtypeStruct((T, LANES), I32),
                   jax.ShapeDtypeStruct((T, LANES), F32),
                   jax.ShapeDtypeStruct((T // TMR, 8, LANES), I32),
                   jax.ShapeDtypeStruct((1, LANES), F32)],
        scratch_shapes=[pltpu.VMEM((1, LANES), F32)],
        compiler_params=_params("arbitrary"),
        name="route",
    )(x, nw, mod, mod, wr, br)


def _segment_starts(info_ref, base, make_copy):
    def start_expert(e, carry):
        n = info_ref[base + e] // SEG_ALIGN
        src0 = info_ref[base + N_EXPERTS + e]
        dst0 = info_ref[base + 2 * N_EXPERTS + e]

        def start_chunk(c, carry):
            make_copy(pl.multiple_of(src0 + c * SEG_ALIGN, SEG_ALIGN),
                      pl.multiple_of(dst0 + c * SEG_ALIGN, SEG_ALIGN)).start()
            return carry

        lax.fori_loop(0, n, start_chunk, 0)
        return carry

    lax.fori_loop(0, N_EXPERTS, start_expert, 0)


def _segment_waits(info_ref, base, make_copy):
    total = lax.fori_loop(0, N_EXPERTS, lambda e, n: n + info_ref[base + e] // SEG_ALIGN, 0)

    def wait_chunk(c, carry):
        make_copy(0, 0).wait()
        return carry

    lax.fori_loop(0, total, wait_chunk, 0)


def _segment_copies(info_ref, base, make_copy):
    _segment_starts(info_ref, base, make_copy)
    _segment_waits(info_ref, base, make_copy)


def _dispatch_kernel(info_ref, h_ref, mi_ref, xb_ref, stg_sc, sem):
    tm = h_ref.shape[0]
    rows = stg_sc.shape[0]
    seg_copy = lambda s, d: pltpu.make_async_copy(
        stg_sc.at[pl.ds(s, SEG_ALIGN)], xb_ref.at[pl.ds(d, SEG_ALIGN)], sem)

    @pl.when(pl.program_id(0) == 0)
    def _():
        stg_sc[0:MOE_BLK, :] = jnp.zeros((MOE_BLK, stg_sc.shape[1]), BF16)
        _segment_copies(info_ref, INFO_TAIL, seg_copy)

        def spare_copy(j):
            return pltpu.make_async_copy(stg_sc.at[pl.ds(0, MOE_BLK)],
                                         xb_ref.at[pl.ds(pl.multiple_of(j * MOE_BLK, MOE_BLK), MOE_BLK)], sem)

        def start_spare(j, carry):
            spare_copy(j).start()
            return carry

        def wait_spare(j, carry):
            spare_copy(j).wait()
            return carry

        first_spare = info_ref[INFO_NUSED]
        lax.fori_loop(first_spare, xb_ref.shape[0] // MOE_BLK, start_spare, 0)
        lax.fori_loop(first_spare, xb_ref.shape[0] // MOE_BLK, wait_spare, 0)

    spos_t = mi_ref[...].astype(F32).T
    r = lax.broadcasted_iota(I32, (rows, tm), 0).astype(F32)
    pm = r == spos_t[TOP_K:TOP_K + 1]
    for kk in range(1, TOP_K):
        pm = pm | (r == spos_t[TOP_K + kk:TOP_K + kk + 1])
    stg_sc[...] = _dot(pm.astype(BF16), h_ref[...].astype(BF16)).astype(BF16)
    _segment_copies(info_ref, 0, seg_copy)


def _dispatch(info, h, mi, rows):
    T, D = h.shape
    return pl.pallas_call(
        _dispatch_kernel,
        grid=(T // TMR,),
        in_specs=[pl.BlockSpec((INFO_LANES,), lambda i: (i,), memory_space=pltpu.SMEM),
                  pl.BlockSpec((TMR, D), lambda i: (i, 0)),
                  pl.BlockSpec((TMR, LANES), lambda i: (i, 0))],
        out_specs=pl.BlockSpec(memory_space=pl.ANY),
        out_shape=jax.ShapeDtypeStruct((rows, D), BF16),
        scratch_shapes=[pltpu.VMEM((STG_ROWS, D), BF16), pltpu.SemaphoreType.DMA],
        compiler_params=_params("arbitrary"),
        name="dispatch",
    )(info, h, mi)


def _gmm_kernel(be_ref, nu_ref, nxt_ref, slot_ref, x_ref, wgu_hbm, bgu_ref, wdn_hbm, bdn_ref, o_ref,
                gu_buf, dn_buf, wgu_sc, wdn_sc, sems, *, layer):
    i = pl.program_id(0)
    e = be_ref[i]
    prev = be_ref[jnp.maximum(i - 1, 0)]

    def fetch(expert, slot):
        return (pltpu.make_async_copy(wgu_hbm.at[layer, expert], gu_buf.at[slot], sems.at[0, slot]),
                pltpu.make_async_copy(wdn_hbm.at[layer, expert], dn_buf.at[slot], sems.at[1, slot]))

    @pl.when(i == 0)
    def _():
        for cp in fetch(e, slot_ref[0]):
            cp.start()

    @pl.when(((i == 0) | (e != prev)) & (i < nu_ref[0]))
    def _():
        slot = slot_ref[i]
        for cp in fetch(e, slot):
            cp.wait()
        wgu_sc[...] = gu_buf[slot].astype(BF16)
        wdn_sc[...] = dn_buf[slot].astype(BF16)

        @pl.when(nxt_ref[i] >= 0)
        def _():
            for cp in fetch(nxt_ref[i], 1 - slot):
                cp.start()

    @pl.when(i < nu_ref[0])
    def _():
        gu = _dot(x_ref[...], wgu_sc[...]) + bgu_ref[0, 0]
        glu = jnp.minimum(gu[:, 0:D_EXPERT], SWIGLU_LIMIT)
        lin = jnp.clip(gu[:, D_EXPERT:2 * D_EXPERT], -SWIGLU_LIMIT, SWIGLU_LIMIT)
        act = glu * _sigmoid(SWIGLU_ALPHA * glu) * (lin + 1.0)
        o_ref[...] = (_dot(act.astype(BF16), wdn_sc[...]) + bdn_ref[0, 0]).astype(BF16)

    @pl.when(i >= nu_ref[0])
    def _():
        o_ref[...] = jnp.zeros(o_ref.shape, BF16)


def _gmm(block_e, n_used, nxt, slot, xb, wgu, bgu, wdn, bdn, l):
    P, D = xb.shape
    L, E, _, F2 = wgu.shape
    xmap = lambda i, be, nu, nx, sl: (jnp.minimum(i, nu[0] - 1), 0)
    bmap = lambda i, be, nu, nx, sl: (l, be[i], 0, 0)
    return pl.pallas_call(
        functools.partial(_gmm_kernel, layer=l),
        grid_spec=pltpu.PrefetchScalarGridSpec(
            num_scalar_prefetch=4,
            grid=(P // MOE_BLK,),
            in_specs=[pl.BlockSpec((MOE_BLK, D), xmap),
                      pl.BlockSpec(memory_space=pl.ANY),
                      pl.BlockSpec((1, 1, 1, F2), bmap),
                      pl.BlockSpec(memory_space=pl.ANY),
                      pl.BlockSpec((1, 1, 1, D), bmap)],
            out_specs=pl.BlockSpec((MOE_BLK, D), lambda i, be, nu, nx, sl: (i, 0)),
            scratch_shapes=[pltpu.VMEM((2, D, F2), F32), pltpu.VMEM((2, F2 // 2, D), F32),
                            pltpu.VMEM((D, F2), BF16), pltpu.VMEM((F2 // 2, D), BF16),
                            pltpu.SemaphoreType.DMA((2, 2))]),
        out_shape=jax.ShapeDtypeStruct((P, D), BF16),
        compiler_params=_params("arbitrary"),
        name="gmm",
    )(block_e, n_used, nxt, slot, xb, wgu, bgu.reshape(L, E, 1, F2), wdn, bdn.reshape(L, E, 1, D))


def _combine_kernel(info_ref, x_ref, mi_ref, gate_ref, g2_ref, fw_ref, yb_ref, o_ref, stg_sc, sem, *, final):
    tm = x_ref.shape[0]
    rows = stg_sc.shape[0]

    @pl.when(pl.program_id(0) == 0)
    def _():
        stg_sc[...] = jnp.zeros(stg_sc.shape, BF16)

    _segment_copies(info_ref, 0, lambda s, d: pltpu.make_async_copy(
        yb_ref.at[pl.ds(d, SEG_ALIGN)], stg_sc.at[pl.ds(s, SEG_ALIGN)], sem))

    spos = mi_ref[...].astype(F32)
    gate = gate_ref[...]
    r = lax.broadcasted_iota(I32, (tm, rows), 1).astype(F32)
    g = jnp.where(r == spos[:, TOP_K:TOP_K + 1], gate[:, 0:1], 0.0)
    for kk in range(1, TOP_K):
        g = g + jnp.where(r == spos[:, TOP_K + kk:TOP_K + kk + 1], gate[:, kk:kk + 1], 0.0)
    g_hi, g_lo = _split2(g)
    y = stg_sc[...]
    out = x_ref[...] + g2_ref[0] * (_dot(g_hi, y) + _dot(g_lo, y))
    if final:
        ms = jnp.mean(out * out, axis=-1, keepdims=True)
        out = out * lax.rsqrt(ms + EPS) * fw_ref[...]
    o_ref[...] = out


def _combine(info, x, mi, gates, mod, fw, yb, lay, final):
    T, D = x.shape
    B, S = lay["B"], lay["S"]
    mrow = lambda i: jnp.where(i < (B * S) // TMR, i // (S // TMR), B)
    return pl.pallas_call(
        functools.partial(_combine_kernel, final=final),
        grid=(T // TMR,),
        in_specs=[pl.BlockSpec((INFO_LANES,), lambda i: (i,), memory_space=pltpu.SMEM),
                  pl.BlockSpec((TMR, D), lambda i: (i, 0)),
                  pl.BlockSpec((TMR, LANES), lambda i: (i, 0)),
                  pl.BlockSpec((TMR, LANES), lambda i: (i, 0)),
                  pl.BlockSpec((1, 1, D), lambda i: (mrow(i), 0, 5)),
                  pl.BlockSpec((1, D), lambda i: (0, 0)),
                  pl.BlockSpec(memory_space=pl.ANY)],
        out_specs=pl.BlockSpec((TMR, D), lambda i: (i, 0)),
        out_shape=jax.ShapeDtypeStruct((T, D), F32),
        scratch_shapes=[pltpu.VMEM((STG_ROWS, D), BF16), pltpu.SemaphoreType.DMA],
        compiler_params=_params("arbitrary"),
        name="combine",
    )(info, x, mi, gates, mod, fw, yb)


def _moe(x, nw, mod, wr, br, wgu, bgu, wdn, bdn, fw, lay, l, final):
    T, D = x.shape
    ntb = T // TMR
    h, mi, mf, blk, tot = _route(x, nw, mod, wr, br, lay)
    counts = tot[0, :N_EXPERTS].astype(I32)
    padded = (counts + MOE_BLK - 1) // MOE_BLK * MOE_BLK
    pad_ends = jnp.cumsum(padded)
    pad_starts = pad_ends - padded
    nblk = -(-(T * TOP_K + ntb * N_EXPERTS * (SEG_ALIGN - 1)) // MOE_BLK) + N_EXPERTS
    starts = jnp.arange(nblk, dtype=I32) * MOE_BLK
    block_e = jnp.minimum(jnp.sum((pad_ends[None, :] <= starts[:, None]).astype(I32), axis=1), N_EXPERTS - 1)
    n_used = (pad_ends[N_EXPERTS - 1:] // MOE_BLK).astype(I32)
    rep = lambda v: jnp.broadcast_to(v[None, :], (ntb, v.shape[0]))
    zeros_e = jnp.zeros((N_EXPERTS,), I32)
    info = jnp.concatenate([
        blk[:, 0, :N_EXPERTS], blk[:, 1, :N_EXPERTS], blk[:, 2, :N_EXPERTS] + pad_starts[None, :], rep(zeros_e),
        rep(padded - counts), rep(zeros_e), rep(pad_starts + counts),
        rep(jnp.concatenate([n_used, jnp.zeros((INFO_LANES - INFO_NUSED - 1,), I32)]))], axis=1).reshape(-1)
    eid = jnp.arange(N_EXPERTS, dtype=I32)
    has = padded > 0
    later = jnp.where(has[None, :] & (eid[None, :] > eid[:, None]), eid[None, :], N_EXPERTS)
    nxt_e = jnp.min(later, axis=1)
    nxt_e = jnp.where(nxt_e < N_EXPERTS, nxt_e, -1)
    slot_e = (jnp.cumsum(has.astype(I32)) - 1) % 2
    xb = _dispatch(info, h, mi, nblk * MOE_BLK)
    yb = _gmm(block_e, n_used, jnp.take(nxt_e, block_e), jnp.take(slot_e, block_e), xb, wgu, bgu, wdn, bdn, l)
    return _combine(info, x, mi, mf, mod, fw, yb, lay, final)


def _rope_tables(S):
    t = np.arange(S)
    row = (t // GRID_W).astype(np.float64)
    col = (t % GRID_W).astype(np.float64)
    lane = np.arange(LANES)
    j = lane % ROPE_PAIR
    inv = ROPE_THETA ** (-(j.astype(np.float32)) / np.float32(ROPE_PAIR))
    pos = np.where((lane % DA_QK) < ROPE_HALF, row[:, None], col[:, None]).astype(np.float32)
    ang = (pos * inv.astype(np.float32)[None, :]).astype(np.float32)
    sign = np.where((lane % ROPE_HALF) < ROPE_PAIR, -1.0, 1.0).astype(np.float32)
    cos_t = np.concatenate([np.cos(ang), np.ones((TM, LANES), np.float32)], axis=0)
    sin_t = np.concatenate([np.sin(ang) * sign[None, :], np.zeros((TM, LANES), np.float32)], axis=0)
    return cos_t.astype(np.float32), sin_t.astype(np.float32)


def kernel(x, c, ctx, c_ctx, w_ada, b_ada, norm1_w, w_in, da_lambda, da_subln_w, ssd_conv_w, ssd_conv_b,
           ssd_a_log, ssd_dt_bias, ssd_d, ssd_norm_w, gla_gk_up, gla_gk_b, gla_norm_w, w_out, norm2_w,
           w_router, b_router, w_gate_up, b_gate_up, w_down, b_down, final_norm_w):
    B, S, D = x.shape
    Lc = ctx.shape[1]
    depth = w_ada.shape[0]
    assert S % TMR == 0 and Lc % TM == 0 and (B * Lc) % TMR == 0 and S % GRID_W == 0
    lay = dict(B=B, S=S, Lc=Lc, nlb=(B * S) // TM, bpb=S // TM, cpb=Lc // TM)
    n_lat = B * S

    xs = jnp.concatenate([x.reshape(B * S, D), ctx.reshape(B * Lc, D)], axis=0)

    cc = jnp.zeros((8, D), F32).at[0:B].set(c).at[B].set(c_ctx)
    mod_all = _ada(cc, w_ada, b_ada)

    cos_t, sin_t = (jnp.asarray(t) for t in _rope_tables(S))

    for l in range(depth):
        last = l == depth - 1
        lam_init = 0.8 - 0.6 * math.exp(-0.3 * l)
        mod = mod_all[l, 0:B + 1].reshape(B + 1, 1, 6 * D)

        wi = w_in[l]
        misc_w = jnp.zeros((D, LANES), F32)
        misc_w = misc_w.at[:, MISC_DT:MISC_DT + N_DT].set(wi[:, O_DT:O_GLA])
        misc_w = misc_w.at[:, MISC_CODE:MISC_CODE + N_CODE].set(wi[:, O_CODE:O_CODE + N_CODE])
        w_re = jnp.concatenate([wi[:, 0:O_DT], wi[:, O_GLA:O_CODE], misc_w], axis=1).astype(BF16)
        a128 = jnp.zeros((1, LANES), F32).at[0, MISC_DT:MISC_DT + N_DT].set(
            -jnp.exp(ssd_a_log[l].astype(F32)).reshape(-1))
        bias128 = jnp.zeros((1, LANES), F32).at[0, MISC_DT:MISC_DT + N_DT].set(
            ssd_dt_bias[l].astype(F32).reshape(-1))
        wgk = jnp.zeros((2, LANES, GLA_KEY_WIDTH), F32)
        for d in range(2):
            wgk = wgk.at[d, MISC_CODE + d * GLA_RANK:MISC_CODE + (d + 1) * GLA_RANK, :].set(gla_gk_up[l, d])
        bgk = gla_gk_b[l].reshape(2, 1, GLA_KEY_WIDTH)
        dsk = jnp.repeat(ssd_d[l], SSD_HEAD_DIM).reshape(1, SSD_WIDTH)
        snw = ssd_norm_w[l].reshape(1, SSD_WIDTH)
        gnw = jnp.tile(gla_norm_w[l], GLA_HEADS).reshape(1, GLA_WIDTH)
        wr = jnp.zeros((D, LANES), F32).at[:, 0:N_EXPERTS].set(w_router[l])
        br = jnp.zeros((1, LANES), F32).at[0, 0:N_EXPERTS].set(b_router[l])

        q, k, v, z, xbc_act, gla_in, misc = _inproj(xs, norm1_w[l].reshape(1, D), mod, cos_t, sin_t, w_re,
                                                    ssd_conv_w[l], ssd_conv_b[l].reshape(1, SSD_CONV_CH), lay)

        a = _attention(q, k, v, da_lambda[l], da_subln_w[l].reshape(1, DA_V), lay, lam_init,
                       ctx_queries=not last)
        yf, yb, of, ob = _scans(xbc_act, gla_in, misc, a128, bias128, wgk, bgk, lay)

        nrows = n_lat if last else xs.shape[0]
        xs = _outproj(xs, a, yf, yb, xbc_act, z, of, ob, gla_in, dsk, snw, gnw,
                      w_out[l].astype(BF16), mod, lay, nrows)
        xs = _moe(xs, norm2_w[l].reshape(1, D), mod, wr, br, w_gate_up, b_gate_up, w_down, b_down,
                  final_norm_w.reshape(1, D), lay, l, last)

    return xs.reshape(B, S, D)
```
